```python
import math
import jax, jax.numpy as jnp
from jax import lax
import numpy as np

D_MODEL = 1024
BATCH = 8
SEQ = 4096
DEPTH = 1

HEAD_DIM = 64
FOX_HEADS = D_MODEL // (2 * HEAD_DIM)
FOX_WIDTH = FOX_HEADS * HEAD_DIM
MLA_HEADS = D_MODEL // (2 * HEAD_DIM)
MLA_NOPE_DIM = 64
MLA_ROPE_DIM = 32
MLA_QK_DIM = MLA_NOPE_DIM + MLA_ROPE_DIM
MLA_V_DIM = 64
MLA_WIDTH = MLA_HEADS * MLA_V_DIM
MIX_WIDTH = FOX_WIDTH + MLA_WIDTH
Q_LORA_RANK = 3 * D_MODEL // 8
KV_LORA_RANK = D_MODEL // 4
D_FF = 4 * D_MODEL
BLOCK_Q = 128
ROPE_THETA = 10000.0
EPS = 1e-6

OFF_FQ = 0
OFF_FK = OFF_FQ + FOX_WIDTH
OFF_FV = OFF_FK + FOX_WIDTH
OFF_FF = OFF_FV + FOX_WIDTH
OFF_CQ = OFF_FF + FOX_HEADS
OFF_CKV = OFF_CQ + Q_LORA_RANK
OFF_KR = OFF_CKV + KV_LORA_RANK
IN_COLS = OFF_KR + MLA_ROPE_DIM

kernel_name = "hymba_fox_mla_hybrid_block"


def rmsnorm(x, g):
    x32 = x.astype(jnp.float32)
    y = x32 * lax.rsqrt(jnp.mean(x32 * x32, axis=-1, keepdims=True) + EPS)
    return (y * g.astype(jnp.float32)).astype(x.dtype)


def rope_cos_sin(positions):
    inv_freq = ROPE_THETA ** (-jnp.arange(0, MLA_ROPE_DIM, 2, dtype=jnp.float32) / MLA_ROPE_DIM)
    ang = positions.astype(jnp.float32)[..., None] * inv_freq
    return jnp.cos(ang), jnp.sin(ang)


def apply_rope(x, cos, sin):
    half = x.shape[-1] // 2
    x1 = x[..., :half].astype(jnp.float32)
    x2 = x[..., half:].astype(jnp.float32)
    return jnp.concatenate([x1 * cos - x2 * sin, x2 * cos + x1 * sin], axis=-1).astype(x.dtype)


def causal_block_attention(q, k, v, scale, log_decay_cum=None):
    b, h, s, dk = q.shape
    nb = s // BLOCK_Q
    q_blocks = jnp.moveaxis(q.reshape(b, h, nb, BLOCK_Q, dk), 2, 0)
    starts = jnp.arange(nb, dtype=jnp.int32) * BLOCK_Q
    key_pos = jnp.arange(s, dtype=jnp.int32)

    def attend(q_blk, start, bias):
        logits = jnp.einsum('bhqd,bhkd->bhqk', q_blk, k).astype(jnp.float32) * scale
        if bias is not None:
            logits = logits + bias
        q_pos = start + jnp.arange(BLOCK_Q, dtype=jnp.int32)
        mask = key_pos[None, :] <= q_pos[:, None]
        logits = jnp.where(mask, logits, -jnp.inf)
        p = jax.nn.softmax(logits, axis=-1).astype(v.dtype)
        return jnp.einsum('bhqk,bhkd->bhqd', p, v)

    if log_decay_cum is None:
        out = lax.map(lambda xs: attend(xs[0], xs[1], None), (q_blocks, starts))
    else:
        F = log_decay_cum
        fq_blocks = jnp.moveaxis(F.reshape(b, h, nb, BLOCK_Q), 2, 0)
        out = lax.map(
            lambda xs: attend(xs[0], xs[1], xs[2][..., :, None] - F[:, :, None, :]),
            (q_blocks, starts, fq_blocks))
    return jnp.moveaxis(out, 0, 2).reshape(b, h, s, -1)


def setup_inputs(seed: int = 0) -> dict:
    key = jax.random.key(seed)
    ks = jax.random.split(key, 16)
    f32 = jnp.float32

    def w(k, shape, fan_in):
        return jax.random.normal(k, shape, f32) * (fan_in ** -0.5)

    def gain(k, shape):
        return 1.0 + 0.02 * jax.random.normal(k, shape, f32)

    x = jax.random.normal(ks[0], (BATCH, SEQ, D_MODEL), f32)
    offsets = jax.random.randint(ks[1], (BATCH, 1), 0, 64, dtype=jnp.int32)
    positions = jnp.arange(SEQ, dtype=jnp.int32)[None, :] + offsets
    return {
        "x": x,
        "positions": positions,
        "attn_norm_g": gain(ks[2], (DEPTH, D_MODEL)),
        "w_in": w(ks[3], (DEPTH, D_MODEL, IN_COLS), D_MODEL),
        "b_forget": 3.0 + 0.5 * jax.random.normal(ks[4], (DEPTH, FOX_HEADS), f32),
        "q_norm_g": gain(ks[5], (DEPTH, Q_LORA_RANK)),
        "w_uq": w(ks[6], (DEPTH, Q_LORA_RANK, MLA_HEADS * MLA_QK_DIM), Q_LORA_RANK),
        "kv_norm_g": gain(ks[7], (DEPTH, KV_LORA_RANK)),
        "w_ukv": w(ks[8], (DEPTH, KV_LORA_RANK, MLA_HEADS * (MLA_NOPE_DIM + MLA_V_DIM)), KV_LORA_RANK),
        "fox_out_g": gain(ks[9], (DEPTH, FOX_WIDTH)),
        "mla_out_g": gain(ks[10], (DEPTH, MLA_WIDTH)),
        "w_o": w(ks[11], (DEPTH, MIX_WIDTH, D_MODEL), MIX_WIDTH),
        "mlp_norm_g": gain(ks[12], (DEPTH, D_MODEL)),
        "w_up": w(ks[13], (DEPTH, D_MODEL, D_FF), D_MODEL),
        "w_down": w(ks[14], (DEPTH, D_FF, D_MODEL), D_FF),
        "final_norm_g": gain(ks[15], (D_MODEL,)),
    }


def reference(x, positions, attn_norm_g, w_in, b_forget, q_norm_g, w_uq, kv_norm_g, w_ukv,
              fox_out_g, mla_out_g, w_o, mlp_norm_g, w_up, w_down, final_norm_g):
    b, s, _ = x.shape
    cos, sin = rope_cos_sin(positions)
    fox_scale = 1.0 / math.sqrt(HEAD_DIM)
    mla_scale = 1.0 / math.sqrt(MLA_QK_DIM)

    for l in range(DEPTH):
        h = rmsnorm(x, attn_norm_g[l])
        proj = jnp.einsum('bsd,dc->bsc', h, w_in[l])

        fq = proj[..., OFF_FQ:OFF_FK].reshape(b, s, FOX_HEADS, HEAD_DIM).transpose(0, 2, 1, 3)
        fk = proj[..., OFF_FK:OFF_FV].reshape(b, s, FOX_HEADS, HEAD_DIM).transpose(0, 2, 1, 3)
        fv = proj[..., OFF_FV:OFF_FF].reshape(b, s, FOX_HEADS, HEAD_DIM).transpose(0, 2, 1, 3)
        f_logit = proj[..., OFF_FF:OFF_CQ].astype(jnp.float32) + b_forget[l].astype(jnp.float32)
        log_f = jax.nn.log_sigmoid(f_logit)
        F = jnp.cumsum(log_f, axis=1).transpose(0, 2, 1)
        fox = causal_block_attention(fq, fk, fv, fox_scale, F)
        fox = fox.transpose(0, 2, 1, 3).reshape(b, s, FOX_WIDTH)

        c_q = rmsnorm(proj[..., OFF_CQ:OFF_CKV], q_norm_g[l])
        c_kv = rmsnorm(proj[..., OFF_CKV:OFF_KR], kv_norm_g[l])
        k_rope = apply_rope(proj[..., OFF_KR:IN_COLS], cos, sin)
        q = jnp.einsum('bsr,rc->bsc', c_q, w_uq[l]).reshape(b, s, MLA_HEADS, MLA_QK_DIM)
        q_nope = q[..., :MLA_NOPE_DIM]
        q_rope = apply_rope(q[..., MLA_NOPE_DIM:], cos[:, :, None, :], sin[:, :, None, :])
        kv = jnp.einsum('bsr,rc->bsc', c_kv, w_ukv[l]).reshape(b, s, MLA_HEADS, MLA_NOPE_DIM + MLA_V_DIM)
        k_nope = kv[..., :MLA_NOPE_DIM]
        mv = kv[..., MLA_NOPE_DIM:]
        mq = jnp.concatenate([q_nope, q_rope], axis=-1).transpose(0, 2, 1, 3)
        mk = jnp.concatenate(
            [k_nope, jnp.broadcast_to(k_rope[:, :, None, :], (b, s, MLA_HEADS, MLA_ROPE_DIM))],
            axis=-1).transpose(0, 2, 1, 3)
        mv = mv.transpose(0, 2, 1, 3)
        mla = causal_block_attention(mq, mk, mv, mla_scale)
        mla = mla.transpose(0, 2, 1, 3).reshape(b, s, MLA_WIDTH)

        mixed = jnp.concatenate([rmsnorm(fox, fox_out_g[l]), rmsnorm(mla, mla_out_g[l])], axis=-1)
        x = x + jnp.einsum('bsc,cd->bsd', mixed, w_o[l])

        h = rmsnorm(x, mlp_norm_g[l])
        u = jnp.einsum('bsd,df->bsf', h, w_up[l])
        x = x + jnp.einsum('bsf,fd->bsd', jnp.square(jax.nn.relu(u)), w_down[l])

    return rmsnorm(x, final_norm_g)
```

```python
import functools
import math

import numpy as np
import jax
import jax.numpy as jnp
from jax import lax
from jax.experimental import pallas as pl
from jax.experimental.pallas import tpu as pltpu

D_MODEL = 1024
HEAD_DIM = 64
N_HEADS = 8
GROUP_WIDTH = N_HEADS * HEAD_DIM
ROPE_DIM = 32
ROPE_HALF = ROPE_DIM // 2
MLA_QK_DIM = HEAD_DIM + ROPE_DIM
Q_RANK = 384
KV_RANK = 256
D_FF = 4096
ROPE_THETA = 10000.0
EPS = 1e-6

OFF_FQ = 0
OFF_FK = OFF_FQ + GROUP_WIDTH
OFF_FV = OFF_FK + GROUP_WIDTH
OFF_FF = OFF_FV + GROUP_WIDTH
OFF_CQ = OFF_FF + N_HEADS
OFF_CKV = OFF_CQ + Q_RANK
OFF_KR = OFF_CKV + KV_RANK
IN_COLS = OFF_KR + ROPE_DIM

LANES = 128
VMEM_LIMIT_BYTES = 56 * 1024 * 1024

LOG2E = math.log2(math.e)
FOX_QSCALE = LOG2E / math.sqrt(HEAD_DIM)
MLA_QSCALE = LOG2E / math.sqrt(MLA_QK_DIM)

PROJ_ROWS = 512
ATTN_TQ = 512
ATTN_TK = 512
MLP_ROWS = 512
FF_CHUNK = 1024
SCAN_CHUNK = 128

F32 = jnp.float32
BF16 = jnp.bfloat16


def _rms(x, g):
    return x * lax.rsqrt(jnp.mean(x * x, axis=-1, keepdims=True) + EPS) * g


def _split3(x):
    hi = x.astype(BF16)
    r = x - hi.astype(F32)
    mid = r.astype(BF16)
    lo = (r - mid.astype(F32)).astype(BF16)
    return hi, mid, lo


def _rope_table_kernel(pos_ref, freq_ref, cos_ref, sin_ref):
    ang = pos_ref[...].astype(F32) * freq_ref[...]
    cos_ref[...] = jnp.cos(ang)
    sin_ref[...] = jnp.sin(ang)


def _rope_tables(positions, inv_freq):
    b, s = positions.shape
    per_row = LANES // ROPE_HALF
    rows = b * s // per_row
    pos_rep = jnp.repeat(positions.reshape(-1), ROPE_HALF).reshape(rows, LANES)
    freq_row = jnp.tile(inv_freq, per_row).reshape(1, LANES)
    blk = rows // 8
    cos, sin = pl.pallas_call(
        _rope_table_kernel,
        grid=(rows // blk,),
        in_specs=[pl.BlockSpec((blk, LANES), lambda i: (i, 0)),
                  pl.BlockSpec((1, LANES), lambda i: (0, 0))],
        out_specs=[pl.BlockSpec((blk, LANES), lambda i: (i, 0))] * 2,
        out_shape=[jax.ShapeDtypeStruct((rows, LANES), F32)] * 2,
        name="rope_tables",
    )(pos_rep, freq_row)
    cos_t = jnp.tile(cos.reshape(b, s, ROPE_HALF), (1, 1, per_row))
    sin_t = jnp.tile(sin.reshape(b, s, ROPE_HALF), (1, 1, per_row))
    return cos_t, sin_t


def _rope_rotate(x, cos_t, sin_signed):
    return x * cos_t + pltpu.roll(x, LANES // 2, axis=1) * sin_signed


def _proj_kernel(x_ref, cos_ref, sin_ref, g_attn_ref, wq_ref, wk_ref, wvt_ref, wff_ref,
                 bff_ref, wcq_ref, wckv_ref, wkr_ref, gq_ref, gkv_ref, wuqn_ref, wuqr_ref,
                 wuk_ref, wuvt_ref, tri_ref,
                 fq_ref, fk_ref, fvt_ref, faux_ref, mqn_ref, mqr_ref, mkn_ref, mkr_ref, mvt_ref,
                 h_sc, carry_sc):
    rows = x_ref.shape[1]

    @pl.when(pl.program_id(1) == 0)
    def _():
        carry_sc[...] = jnp.zeros_like(carry_sc)

    h_sc[...] = _rms(x_ref[0], g_attn_ref[...]).astype(BF16)
    h = h_sc[...]
    nt = (((1,), (1,)), ((), ()))

    fq_ref[0] = (jnp.dot(h, wq_ref[...], preferred_element_type=F32) * FOX_QSCALE).astype(BF16)
    fk_ref[0] = jnp.dot(h, wk_ref[...], preferred_element_type=F32).astype(BF16)
    fvt_ref[0] = lax.dot_general(wvt_ref[...], h, nt, preferred_element_type=F32).astype(BF16)

    lane = lax.broadcasted_iota(jnp.int32, (1, LANES), 1)
    cos_t = cos_ref[0]
    sin_s = sin_ref[0] * jnp.where(lane < LANES // 2, -1.0, 1.0)

    cq = jnp.dot(h, wcq_ref[...], preferred_element_type=F32)
    cq = _rms(cq, gq_ref[...]).astype(BF16)
    mqn_ref[0] = (jnp.dot(cq, wuqn_ref[...], preferred_element_type=F32) * MLA_QSCALE).astype(BF16)
    qr = jnp.dot(cq, wuqr_ref[...], preferred_element_type=F32)
    for g in range(qr.shape[1] // LANES):
        blk = _rope_rotate(qr[:, g * LANES:(g + 1) * LANES], cos_t, sin_s)
        mqr_ref[0, :, g * LANES:(g + 1) * LANES] = (blk * MLA_QSCALE).astype(BF16)

    ckv = jnp.dot(h, wckv_ref[...], preferred_element_type=F32)
    ckv = _rms(ckv, gkv_ref[...]).astype(BF16)
    mkn_ref[0] = jnp.dot(ckv, wuk_ref[...], preferred_element_type=F32).astype(BF16)
    mvt_ref[0] = lax.dot_general(wuvt_ref[...], ckv, nt, preferred_element_type=F32).astype(BF16)
    kr = jnp.dot(h, wkr_ref[...], preferred_element_type=F32)
    mkr_ref[0] = _rope_rotate(kr, cos_t, sin_s).astype(BF16)

    z = jnp.dot(h, wff_ref[...], preferred_element_type=F32) + bff_ref[...]
    log_f = jnp.where(lane < N_HEADS, jax.nn.log_sigmoid(z), 0.0)
    tri = tri_ref[...]
    carry = carry_sc[0:1, :]
    for c in range(rows // SCAN_CHUNK):
        hi, mid, lo = _split3(log_f[c * SCAN_CHUNK:(c + 1) * SCAN_CHUNK, :])
        cum = (jnp.dot(tri, hi, preferred_element_type=F32)
               + jnp.dot(tri, mid, preferred_element_type=F32)
               + jnp.dot(tri, lo, preferred_element_type=F32)) + carry
        carry = cum[SCAN_CHUNK - 1:SCAN_CHUNK, :]
        p_hi, p_mid, p_lo = _split3(cum * (-LOG2E))
        aux = (p_hi.astype(F32) + pltpu.roll(p_mid.astype(F32), N_HEADS, axis=1)
               + pltpu.roll(p_lo.astype(F32), 2 * N_HEADS, axis=1))
        faux_ref[0, c * SCAN_CHUNK:(c + 1) * SCAN_CHUNK, :] = aux.astype(BF16)
    carry_sc[...] = jnp.broadcast_to(carry, carry_sc.shape)


def _const_spec(shape):
    return pl.BlockSpec(shape, lambda *_: (0,) * len(shape))


def _projection(x, cos_t, sin_t, w):
    b, s, _ = x.shape
    tm = PROJ_ROWS
    row_spec = lambda width: pl.BlockSpec((1, tm, width), lambda i, j: (i, j, 0))
    col_spec = pl.BlockSpec((1, GROUP_WIDTH, tm), lambda i, j: (i, 0, j))
    consts = [w["g_attn"], w["wq"], w["wk"], w["wvt"], w["wff"], w["bff"], w["wcq"], w["wckv"],
              w["wkr"], w["gq"], w["gkv"], w["wuqn"], w["wuqr"], w["wuk"], w["wuvt"], w["tri"]]
    row_out = lambda width: jax.ShapeDtypeStruct((b, s, width), BF16)
    col_out = jax.ShapeDtypeStruct((b, GROUP_WIDTH, s), BF16)
    return pl.pallas_call(
        _proj_kernel,
        grid=(b, s // tm),
        in_specs=[row_spec(D_MODEL), row_spec(LANES), row_spec(LANES)]
                 + [_const_spec(c.shape) for c in consts],
        out_specs=[row_spec(GROUP_WIDTH), row_spec(GROUP_WIDTH), col_spec, row_spec(LANES),
                   row_spec(GROUP_WIDTH), row_spec(2 * LANES), row_spec(GROUP_WIDTH),
                   row_spec(LANES), col_spec],
        out_shape=[row_out(GROUP_WIDTH), row_out(GROUP_WIDTH), col_out, row_out(LANES),
                   row_out(GROUP_WIDTH), row_out(2 * LANES), row_out(GROUP_WIDTH),
                   row_out(LANES), col_out],
        scratch_shapes=[pltpu.VMEM((tm, D_MODEL), BF16), pltpu.VMEM((8, LANES), F32)],
        compiler_params=pltpu.CompilerParams(
            dimension_semantics=("arbitrary", "arbitrary"),
            vmem_limit_bytes=VMEM_LIMIT_BYTES),
        name="input_projection",
    )(x, cos_t, sin_t, *consts)


def _attn_kernel(qm_ref, qa_ref, km_ref, ka_ref, vt_ref, mask_ref, o_ref, bias_sc):
    s_len = qm_ref.shape[1]
    tq, tk = ATTN_TQ, ATTN_TK
    nt = (((1,), (1,)), ((), ()))

    key_idx = lax.broadcasted_iota(jnp.int32, (tk, tq), 0)
    qry_idx = lax.broadcasted_iota(jnp.int32, (tk, tq), 1)
    bias_sc[...] = jnp.where(key_idx <= qry_idx, 0.0, -jnp.inf)

    def kv_tile(k0):
        return jnp.concatenate([km_ref[0, pl.ds(k0, tk), :], ka_ref[0, pl.ds(k0, tk), :]], axis=-1)

    def q_tile(i, _):
        q0 = pl.multiple_of(i * tq, tq)
        qcat = jnp.concatenate([qm_ref[0, pl.ds(q0, tq), :], qa_ref[0, pl.ds(q0, tq), :]], axis=-1)
        qh = [qcat * mask_ref[0, a:a + 1, :].astype(BF16) for a in range(2)]

        kcat = kv_tile(q0)
        state = []
        for a in range(2):
            sc = lax.dot_general(kcat, qh[a], nt, preferred_element_type=F32) + bias_sc[...]
            m = jnp.max(sc, axis=0, keepdims=True)
            p = jnp.exp2(sc - m)
            l = jnp.sum(p, axis=0, keepdims=True)
            vt = vt_ref[0, a * HEAD_DIM:(a + 1) * HEAD_DIM, pl.ds(q0, tk)]
            acc = jnp.dot(vt, p.astype(BF16), preferred_element_type=F32)
            state += [m, l, acc]

        def kv_step(j, st):
            k0 = pl.multiple_of(j * tk, tk)
            kcat = kv_tile(k0)
            new = []
            for a in range(2):
                m, l, acc = st[3 * a:3 * a + 3]
                sc = lax.dot_general(kcat, qh[a], nt, preferred_element_type=F32)
                m_new = jnp.maximum(m, jnp.max(sc, axis=0, keepdims=True))
                alpha = jnp.exp2(m - m_new)
                p = jnp.exp2(sc - m_new)
                l = alpha * l + jnp.sum(p, axis=0, keepdims=True)
                vt = vt_ref[0, a * HEAD_DIM:(a + 1) * HEAD_DIM, pl.ds(k0, tk)]
                acc = alpha * acc + jnp.dot(vt, p.astype(BF16), preferred_element_type=F32)
                new += [m_new, l, acc]
            return tuple(new)

        st = lax.fori_loop(0, i, kv_step, tuple(state))
        out_t = jnp.concatenate([st[2] / st[1], st[5] / st[4]], axis=0)
        o_ref[0, pl.ds(q0, tq), :] = out_t.T
        return 0

    lax.fori_loop(0, s_len // tq, q_tile, 0)


def _attention(qm, qa, km, ka, vt, mask, qa_batched, qa_per_pairs):
    b, s, _ = qm.shape
    pairs = N_HEADS // 2
    tok = lambda f: pl.BlockSpec((1, s, LANES), f)
    qa_map = (lambda i, p: (i, 0, p // qa_per_pairs)) if qa_batched else (lambda i, p: (0, 0, 0))
    return pl.pallas_call(
        _attn_kernel,
        grid=(b, pairs),
        in_specs=[tok(lambda i, p: (i, 0, p)), tok(qa_map), tok(lambda i, p: (i, 0, p)),
                  tok(lambda i, p: (i, 0, 0)),
                  pl.BlockSpec((1, LANES, s), lambda i, p: (i, p, 0)),
                  pl.BlockSpec((1, 2, 2 * LANES), lambda i, p: (p, 0, 0))],
        out_specs=tok(lambda i, p: (i, 0, p)),
        out_shape=jax.ShapeDtypeStruct((b, s, GROUP_WIDTH), F32),
        scratch_shapes=[pltpu.VMEM((ATTN_TK, ATTN_TQ), F32)],
        compiler_params=pltpu.CompilerParams(
            dimension_semantics=("arbitrary", "arbitrary"),
            vmem_limit_bytes=VMEM_LIMIT_BYTES),
        name="causal_attention",
    )(qm, qa, km, ka, vt, mask)


def _mlp_kernel(x_ref, fox_ref, mla_ref, gfox_ref, gmla_ref, wo_ref, gmlp_ref, wup_ref,
                wdown_ref, gfin_ref, o_ref, *, final):
    mixed = jnp.concatenate([_rms(fox_ref[...], gfox_ref[...]).astype(BF16),
                             _rms(mla_ref[...], gmla_ref[...]).astype(BF16)], axis=-1)
    x1 = x_ref[...] + jnp.dot(mixed, wo_ref[...], preferred_element_type=F32)
    h = _rms(x1, gmlp_ref[...]).astype(BF16)
    y = x1
    for c in range(D_FF // FF_CHUNK):
        u = jnp.dot(h, wup_ref[:, c * FF_CHUNK:(c + 1) * FF_CHUNK], preferred_element_type=F32)
        act = jnp.square(jnp.maximum(u, 0.0)).astype(BF16)
        y = y + jnp.dot(act, wdown_ref[c * FF_CHUNK:(c + 1) * FF_CHUNK, :],
                        preferred_element_type=F32)
    o_ref[...] = _rms(y, gfin_ref[...]) if final else y


def _out_mlp(x2d, fox, mla, w, final):
    n = x2d.shape[0]
    tm = MLP_ROWS
    row = lambda width: pl.BlockSpec((tm, width), lambda i: (i, 0))
    resident = lambda a: pl.BlockSpec(a.shape, lambda i: (0,) * a.ndim,
                                      pipeline_mode=pl.Buffered(1))
    consts = [w["gfox"], w["gmla"], w["wo"], w["gmlp"], w["wup"], w["wdown"], w["gfin"]]
    return pl.pallas_call(
        functools.partial(_mlp_kernel, final=final),
        grid=(n // tm,),
        in_specs=[row(D_MODEL), row(GROUP_WIDTH), row(GROUP_WIDTH)] + [resident(c) for c in consts],
        out_specs=row(D_MODEL),
        out_shape=jax.ShapeDtypeStruct((n, D_MODEL), F32),
        compiler_params=pltpu.CompilerParams(
            dimension_semantics=("arbitrary",),
            vmem_limit_bytes=VMEM_LIMIT_BYTES),
        name="out_proj_mlp",
    )(x2d, fox, mla, *consts)


def _rope_split_cols(wr):
    k = wr.shape[0]
    heads = wr.shape[1]
    per_group = LANES // ROPE_DIM
    first = wr[:, :, :ROPE_HALF].reshape(k, heads // per_group, per_group * ROPE_HALF)
    second = wr[:, :, ROPE_HALF:].reshape(k, heads // per_group, per_group * ROPE_HALF)
    return jnp.concatenate([first, second], axis=-1).reshape(k, heads * ROPE_DIM)


def _prep_weights(attn_norm_g, w_in, b_forget, q_norm_g, w_uq, kv_norm_g, w_ukv, fox_out_g,
                  mla_out_g, w_o, mlp_norm_g, w_up, w_down, final_norm_g):
    row = lambda v: v.reshape(1, -1).astype(F32)
    pad_lanes = lambda a: jnp.pad(a, ((0, 0), (0, LANES - a.shape[1])))
    wkr = w_in[:, OFF_KR:IN_COLS].reshape(D_MODEL, 1, ROPE_DIM)
    wkr = jnp.broadcast_to(wkr, (D_MODEL, LANES // ROPE_DIM, ROPE_DIM))
    uq = w_uq.reshape(Q_RANK, N_HEADS, MLA_QK_DIM)
    ukv = w_ukv.reshape(KV_RANK, N_HEADS, 2 * HEAD_DIM)
    tri = np.tril(np.ones((SCAN_CHUNK, SCAN_CHUNK), np.float32))
    return {
        "g_attn": row(attn_norm_g),
        "wq": w_in[:, OFF_FQ:OFF_FK].astype(BF16),
        "wk": w_in[:, OFF_FK:OFF_FV].astype(BF16),
        "wvt": w_in[:, OFF_FV:OFF_FF].T.astype(BF16),
        "wff": pad_lanes(w_in[:, OFF_FF:OFF_CQ]).astype(BF16),
        "bff": pad_lanes(row(b_forget)),
        "wcq": w_in[:, OFF_CQ:OFF_CKV].astype(BF16),
        "wckv": w_in[:, OFF_CKV:OFF_KR].astype(BF16),
        "wkr": _rope_split_cols(wkr).astype(BF16),
        "gq": row(q_norm_g),
        "gkv": row(kv_norm_g),
        "wuqn": uq[:, :, :HEAD_DIM].reshape(Q_RANK, GROUP_WIDTH).astype(BF16),
        "wuqr": _rope_split_cols(uq[:, :, HEAD_DIM:]).astype(BF16),
        "wuk": ukv[:, :, :HEAD_DIM].reshape(KV_RANK, GROUP_WIDTH).astype(BF16),
        "wuvt": ukv[:, :, HEAD_DIM:].reshape(KV_RANK, GROUP_WIDTH).T.astype(BF16),
        "tri": jnp.asarray(tri, BF16),
        "gfox": row(fox_out_g),
        "gmla": row(mla_out_g),
        "wo": w_o.astype(BF16),
        "gmlp": row(mlp_norm_g),
        "wup": w_up.astype(BF16),
        "wdown": w_down.astype(BF16),
        "gfin": row(final_norm_g),
    }


def _head_masks():
    pairs = N_HEADS // 2
    fox = np.zeros((pairs, 2, 2 * LANES), np.float32)
    mla = np.zeros((pairs, 2, 2 * LANES), np.float32)
    per_group = LANES // ROPE_DIM
    for p in range(pairs):
        for a in range(2):
            h = 2 * p + a
            fox[p, a, a * HEAD_DIM:(a + 1) * HEAD_DIM] = 1.0
            mla[p, a, a * HEAD_DIM:(a + 1) * HEAD_DIM] = 1.0
            for piece in range(3):
                fox[p, a, LANES + piece * N_HEADS + h] = 1.0
            hh = h % per_group
            for half in range(2):
                lo = LANES + half * (LANES // 2) + hh * ROPE_HALF
                mla[p, a, lo:lo + ROPE_HALF] = 1.0
    return jnp.asarray(fox), jnp.asarray(mla)


def kernel(x, positions, attn_norm_g, w_in, b_forget, q_norm_g, w_uq, kv_norm_g, w_ukv,
           fox_out_g, mla_out_g, w_o, mlp_norm_g, w_up, w_down, final_norm_g):
    b, s, d = x.shape
    depth = w_in.shape[0]
    inv_freq = ROPE_THETA ** (-jnp.arange(0, ROPE_DIM, 2, dtype=F32) / ROPE_DIM)
    cos_t, sin_t = _rope_tables(positions, inv_freq)
    fox_mask, mla_mask = _head_masks()
    ones_aux = jnp.ones((1, s, LANES), BF16)

    for l in range(depth):
        w = _prep_weights(attn_norm_g[l], w_in[l], b_forget[l], q_norm_g[l], w_uq[l],
                          kv_norm_g[l], w_ukv[l], fox_out_g[l], mla_out_g[l], w_o[l],
                          mlp_norm_g[l], w_up[l], w_down[l], final_norm_g)
        fq, fk, fvt, faux, mqn, mqr, mkn, mkr, mvt = _projection(x, cos_t, sin_t, w)
        fox = _attention(fq, ones_aux, fk, faux, fvt, fox_mask, False, 1)
        mla = _attention(mqn, mqr, mkn, mkr, mvt, mla_mask, True, 2)
        y = _out_mlp(x.reshape(b * s, d), fox.reshape(b * s, GROUP_WIDTH),
                     mla.reshape(b * s, GROUP_WIDTH), w, l == depth - 1)
        x = y.reshape(b, s, d)
    return x
```

```python
import functools
import math

import numpy as np
import jax
import jax.numpy as jnp
from jax import lax
from jax.experimental import pallas as pl
from jax.experimental.pallas import tpu as pltpu

D_MODEL = 1024
HEAD_DIM = 64
N_HEADS = 8
GROUP_WIDTH = N_HEADS * HEAD_DIM
ROPE_DIM = 32
ROPE_HALF = ROPE_DIM // 2
MLA_QK_DIM = HEAD_DIM + ROPE_DIM
Q_RANK = 384
KV_RANK = 256
D_FF = 4096
ROPE_THETA = 10000.0
EPS = 1e-6

OFF_FQ = 0
OFF_FK = OFF_FQ + GROUP_WIDTH
OFF_FV = OFF_FK + GROUP_WIDTH
OFF_FF = OFF_FV + GROUP_WIDTH
OFF_CQ = OFF_FF + N_HEADS
OFF_CKV = OFF_CQ + Q_RANK
OFF_KR = OFF_CKV + KV_RANK
IN_COLS = OFF_KR + ROPE_DIM

LANES = 128
VMEM_LIMIT_BYTES = 56 * 1024 * 1024

LOG2E = math.log2(math.e)
FOX_QSCALE = LOG2E / math.sqrt(HEAD_DIM)
MLA_QSCALE = LOG2E / math.sqrt(MLA_QK_DIM)

PROJ_ROWS = 512
ATTN_TQ = 512
ATTN_TK = 512
MLP_ROWS = 512
FF_CHUNK = 1024
SCAN_CHUNK = 128
ONES_ROWS = 16

F32 = jnp.float32
BF16 = jnp.bfloat16


def _rms(x, g):
    return x * lax.rsqrt(jnp.mean(x * x, axis=-1, keepdims=True) + EPS) * g


def _split3(x):
    hi = x.astype(BF16)
    r = x - hi.astype(F32)
    mid = r.astype(BF16)
    lo = (r - mid.astype(F32)).astype(BF16)
    return hi, mid, lo


def _rope_table_kernel(pos_ref, freq_ref, cos_ref, sin_ref):
    ang = pos_ref[...].astype(F32) * freq_ref[...]
    cos_ref[...] = jnp.cos(ang)
    sin_ref[...] = jnp.sin(ang)


def _rope_tables(positions, inv_freq):
    b, s = positions.shape
    per_row = LANES // ROPE_HALF
    rows = b * s // per_row
    pos_rep = jnp.repeat(positions.reshape(-1), ROPE_HALF).reshape(rows, LANES)
    freq_row = jnp.tile(inv_freq, per_row).reshape(1, LANES)
    blk = rows // 8
    cos, sin = pl.pallas_call(
        _rope_table_kernel,
        grid=(rows // blk,),
        in_specs=[pl.BlockSpec((blk, LANES), lambda i: (i, 0)),
                  pl.BlockSpec((1, LANES), lambda i: (0, 0))],
        out_specs=[pl.BlockSpec((blk, LANES), lambda i: (i, 0))] * 2,
        out_shape=[jax.ShapeDtypeStruct((rows, LANES), F32)] * 2,
        name="rope_tables",
    )(pos_rep, freq_row)
    cos_t = jnp.tile(cos.reshape(b, s, ROPE_HALF), (1, 1, per_row))
    sin_t = jnp.tile(sin.reshape(b, s, ROPE_HALF), (1, 1, per_row))
    return cos_t, sin_t


def _rope_rotate(x, cos_t, sin_signed):
    return x * cos_t + pltpu.roll(x, LANES // 2, axis=1) * sin_signed


def _proj_kernel(x_ref, cos_ref, sin_ref, g_attn_ref, wq_ref, wk_ref, wvt_ref, wff_ref,
                 bff_ref, wcq_ref, wckv_ref, wkr_ref, gq_ref, gkv_ref, wuqn_ref, wuqr_ref,
                 wuk_ref, wuvt_ref, tri_ref,
                 fq_ref, fk_ref, fvt_ref, faux_ref, mqn_ref, mqr_ref, mkn_ref, mkr_ref, mvt_ref,
                 h_sc, carry_sc):
    rows = x_ref.shape[1]

    @pl.when(pl.program_id(1) == 0)
    def _():
        carry_sc[...] = jnp.zeros_like(carry_sc)

    h_sc[...] = _rms(x_ref[0], g_attn_ref[...]).astype(BF16)
    h = h_sc[...]
    nt = (((1,), (1,)), ((), ()))

    fq_ref[0] = (jnp.dot(h, wq_ref[...], preferred_element_type=F32) * FOX_QSCALE).astype(BF16)
    fk_ref[0] = jnp.dot(h, wk_ref[...], preferred_element_type=F32).astype(BF16)
    fvt_ref[0] = lax.dot_general(wvt_ref[...], h, nt, preferred_element_type=F32).astype(BF16)

    lane = lax.broadcasted_iota(jnp.int32, (1, LANES), 1)
    cos_t = cos_ref[0]
    sin_s = sin_ref[0] * jnp.where(lane < LANES // 2, -1.0, 1.0)

    cq = jnp.dot(h, wcq_ref[...], preferred_element_type=F32)
    cq = _rms(cq, gq_ref[...]).astype(BF16)
    mqn_ref[0] = (jnp.dot(cq, wuqn_ref[...], preferred_element_type=F32) * MLA_QSCALE).astype(BF16)
    qr = jnp.dot(cq, wuqr_ref[...], preferred_element_type=F32)
    for g in range(qr.shape[1] // LANES):
        blk = _rope_rotate(qr[:, g * LANES:(g + 1) * LANES], cos_t, sin_s)
        mqr_ref[0, :, g * LANES:(g + 1) * LANES] = (blk * MLA_QSCALE).astype(BF16)

    ckv = jnp.dot(h, wckv_ref[...], preferred_element_type=F32)
    ckv = _rms(ckv, gkv_ref[...]).astype(BF16)
    mkn_ref[0] = jnp.dot(ckv, wuk_ref[...], preferred_element_type=F32).astype(BF16)
    mvt_ref[0] = lax.dot_general(wuvt_ref[...], ckv, nt, preferred_element_type=F32).astype(BF16)
    kr = jnp.dot(h, wkr_ref[...], preferred_element_type=F32)
    mkr_ref[0] = _rope_rotate(kr, cos_t, sin_s).astype(BF16)

    z = jnp.dot(h, wff_ref[...], preferred_element_type=F32) + bff_ref[...]
    log_f = jnp.where(lane < N_HEADS, jax.nn.log_sigmoid(z), 0.0)
    tri = tri_ref[...]
    carry = carry_sc[0:1, :]
    for c in range(rows // SCAN_CHUNK):
        hi, mid, lo = _split3(log_f[c * SCAN_CHUNK:(c + 1) * SCAN_CHUNK, :])
        cum = (jnp.dot(tri, hi, preferred_element_type=F32)
               + jnp.dot(tri, mid, preferred_element_type=F32)
               + jnp.dot(tri, lo, preferred_element_type=F32)) + carry
        carry = cum[SCAN_CHUNK - 1:SCAN_CHUNK, :]
        p_hi, p_mid, p_lo = _split3(cum * (-LOG2E))
        aux = (p_hi.astype(F32) + pltpu.roll(p_mid.astype(F32), N_HEADS, axis=1)
               + pltpu.roll(p_lo.astype(F32), 2 * N_HEADS, axis=1))
        faux_ref[0, c * SCAN_CHUNK:(c + 1) * SCAN_CHUNK, :] = aux.astype(BF16)
    carry_sc[...] = jnp.broadcast_to(carry, carry_sc.shape)


def _const_spec(shape):
    return pl.BlockSpec(shape, lambda *_: (0,) * len(shape))


def _projection(x, cos_t, sin_t, w):
    b, s, _ = x.shape
    tm = PROJ_ROWS
    row_spec = lambda width: pl.BlockSpec((1, tm, width), lambda i, j: (i, j, 0))
    col_spec = pl.BlockSpec((1, GROUP_WIDTH, tm), lambda i, j: (i, 0, j))
    consts = [w["g_attn"], w["wq"], w["wk"], w["wvt"], w["wff"], w["bff"], w["wcq"], w["wckv"],
              w["wkr"], w["gq"], w["gkv"], w["wuqn"], w["wuqr"], w["wuk"], w["wuvt"], w["tri"]]
    row_out = lambda width: jax.ShapeDtypeStruct((b, s, width), BF16)
    col_out = jax.ShapeDtypeStruct((b, GROUP_WIDTH, s), BF16)
    return pl.pallas_call(
        _proj_kernel,
        grid=(b, s // tm),
        in_specs=[row_spec(D_MODEL), row_spec(LANES), row_spec(LANES)]
                 + [_const_spec(c.shape) for c in consts],
        out_specs=[row_spec(GROUP_WIDTH), row_spec(GROUP_WIDTH), col_spec, row_spec(LANES),
                   row_spec(GROUP_WIDTH), row_spec(2 * LANES), row_spec(GROUP_WIDTH),
                   row_spec(LANES), col_spec],
        out_shape=[row_out(GROUP_WIDTH), row_out(GROUP_WIDTH), col_out, row_out(LANES),
                   row_out(GROUP_WIDTH), row_out(2 * LANES), row_out(GROUP_WIDTH),
                   row_out(LANES), col_out],
        scratch_shapes=[pltpu.VMEM((tm, D_MODEL), BF16), pltpu.VMEM((8, LANES), F32)],
        compiler_params=pltpu.CompilerParams(
            dimension_semantics=("arbitrary", "arbitrary"),
            vmem_limit_bytes=VMEM_LIMIT_BYTES),
        name="input_projection",
    )(x, cos_t, sin_t, *consts)


def _attn_kernel(qm_ref, qa_ref, km_ref, ka_ref, vt_ref, mask_ref, o_ref, bias_sc, s0_sc, s1_sc,
                 acc_sc):
    s_len = qm_ref.shape[1]
    tq, tk = ATTN_TQ, ATTN_TK
    nt = (((1,), (1,)), ((), ()))

    key_idx = lax.broadcasted_iota(jnp.int32, (tk, tq), 0)
    qry_idx = lax.broadcasted_iota(jnp.int32, (tk, tq), 1)
    bias_sc[...] = jnp.where(key_idx <= qry_idx, 0.0, -jnp.inf)

    def kv_tile(k0):
        return jnp.concatenate([km_ref[0, pl.ds(k0, tk), :], ka_ref[0, pl.ds(k0, tk), :]], axis=-1)

    ones_rows = jnp.ones((ONES_ROWS, tk), BF16)

    def q_tile(i, _):
        q0 = pl.multiple_of(i * tq, tq)
        qcat = jnp.concatenate([qm_ref[0, pl.ds(q0, tq), :], qa_ref[0, pl.ds(q0, tq), :]], axis=-1)
        qh = [qcat * mask_ref[0, a:a + 1, :].astype(BF16) for a in range(2)]

        def produce(dst, k0, diagonal):
            kcat = kv_tile(k0)
            maxima = []
            for a in range(2):
                sc = lax.dot_general(kcat, qh[a], nt, preferred_element_type=F32)
                if diagonal:
                    sc = sc + bias_sc[...]
                dst[a] = sc
                maxima.append(jnp.max(sc, axis=0, keepdims=True))
            return maxima

        def consume(src, k0, maxima, m_run):
            k0 = pl.multiple_of(k0, tk)
            m_out = []
            for a in range(2):
                m_new = jnp.maximum(m_run[a], maxima[a])
                alpha = jnp.exp2(m_run[a] - m_new)
                p = jnp.exp2(src[a] - m_new).astype(BF16)
                vt = jnp.concatenate(
                    [vt_ref[0, a * HEAD_DIM:(a + 1) * HEAD_DIM, pl.ds(k0, tk)], ones_rows], axis=0)
                acc_sc[a] = alpha * acc_sc[a] + jnp.dot(vt, p, preferred_element_type=F32)
                m_out.append(m_new)
            return m_out

        def step(src, dst, j, carry):
            k_pending, mx_a, mx_b, m_a, m_b = carry
            k0 = pl.multiple_of(j * tk, tk)
            mx_new = produce(dst, k0, False)
            m_new = consume(src, k_pending, [mx_a, mx_b], [m_a, m_b])
            return k0, mx_new[0], mx_new[1], m_new[0], m_new[1]

        acc_sc[...] = jnp.zeros_like(acc_sc)
        neg_inf = jnp.full((1, tq), -jnp.inf, F32)
        mx = produce(s0_sc, q0, True)

        def pair_step(jj, carry):
            carry = step(s0_sc, s1_sc, 2 * jj, carry)
            return step(s1_sc, s0_sc, 2 * jj + 1, carry)

        carry = lax.fori_loop(0, lax.shift_right_logical(i, 1), pair_step,
                              (q0, mx[0], mx[1], neg_inf, neg_inf))
        odd = (i & 1) == 1

        @pl.when(odd)
        def _():
            k_pending, mx_a, mx_b, m_a, m_b = step(s0_sc, s1_sc, i - 1, carry)
            consume(s1_sc, k_pending, [mx_a, mx_b], [m_a, m_b])

        @pl.when(jnp.logical_not(odd))
        def _():
            k_pending, mx_a, mx_b, m_a, m_b = carry
            consume(s0_sc, k_pending, [mx_a, mx_b], [m_a, m_b])

        out_t = jnp.concatenate(
            [acc_sc[a, 0:HEAD_DIM, :] / acc_sc[a, HEAD_DIM:HEAD_DIM + 1, :] for a in range(2)],
            axis=0)
        o_ref[0, pl.ds(q0, tq), :] = out_t.T
        return 0

    lax.fori_loop(0, s_len // tq, q_tile, 0)


def _attention(qm, qa, km, ka, vt, mask, qa_batched, qa_per_pairs):
    b, s, _ = qm.shape
    pairs = N_HEADS // 2
    tok = lambda f: pl.BlockSpec((1, s, LANES), f)
    qa_map = (lambda i, p: (i, 0, p // qa_per_pairs)) if qa_batched else (lambda i, p: (0, 0, 0))
    return pl.pallas_call(
        _attn_kernel,
        grid=(b, pairs),
        in_specs=[tok(lambda i, p: (i, 0, p)), tok(qa_map), tok(lambda i, p: (i, 0, p)),
                  tok(lambda i, p: (i, 0, 0)),
                  pl.BlockSpec((1, LANES, s), lambda i, p: (i, p, 0)),
                  pl.BlockSpec((1, 2, 2 * LANES), lambda i, p: (p, 0, 0))],
        out_specs=tok(lambda i, p: (i, 0, p)),
        out_shape=jax.ShapeDtypeStruct((b, s, GROUP_WIDTH), F32),
        scratch_shapes=[pltpu.VMEM((ATTN_TK, ATTN_TQ), F32),
                        pltpu.VMEM((2, ATTN_TK, ATTN_TQ), F32),
                        pltpu.VMEM((2, ATTN_TK, ATTN_TQ), F32),
                        pltpu.VMEM((2, HEAD_DIM + ONES_ROWS, ATTN_TQ), F32)],
        compiler_params=pltpu.CompilerParams(
            dimension_semantics=("arbitrary", "arbitrary"),
            vmem_limit_bytes=VMEM_LIMIT_BYTES),
        name="causal_attention",
    )(qm, qa, km, ka, vt, mask)


def _mlp_kernel(x_ref, fox_ref, mla_ref, gfox_ref, gmla_ref, wo_ref, gmlp_ref, wup_ref,
                wdown_ref, gfin_ref, o_ref, *, final):
    mixed = jnp.concatenate([_rms(fox_ref[...], gfox_ref[...]).astype(BF16),
                             _rms(mla_ref[...], gmla_ref[...]).astype(BF16)], axis=-1)
    x1 = x_ref[...] + jnp.dot(mixed, wo_ref[...], preferred_element_type=F32)
    h = _rms(x1, gmlp_ref[...]).astype(BF16)
    y = x1
    for c in range(D_FF // FF_CHUNK):
        u = jnp.dot(h, wup_ref[:, c * FF_CHUNK:(c + 1) * FF_CHUNK], preferred_element_type=F32)
        act = jnp.square(jnp.maximum(u, 0.0)).astype(BF16)
        y = y + jnp.dot(act, wdown_ref[c * FF_CHUNK:(c + 1) * FF_CHUNK, :],
                        preferred_element_type=F32)
    o_ref[...] = _rms(y, gfin_ref[...]) if final else y


def _out_mlp(x2d, fox, mla, w, final):
    n = x2d.shape[0]
    tm = MLP_ROWS
    row = lambda width: pl.BlockSpec((tm, width), lambda i: (i, 0))
    resident = lambda a: pl.BlockSpec(a.shape, lambda i: (0,) * a.ndim,
                                      pipeline_mode=pl.Buffered(1))
    consts = [w["gfox"], w["gmla"], w["wo"], w["gmlp"], w["wup"], w["wdown"], w["gfin"]]
    return pl.pallas_call(
        functools.partial(_mlp_kernel, final=final),
        grid=(n // tm,),
        in_specs=[row(D_MODEL), row(GROUP_WIDTH), row(GROUP_WIDTH)] + [resident(c) for c in consts],
        out_specs=row(D_MODEL),
        out_shape=jax.ShapeDtypeStruct((n, D_MODEL), F32),
        compiler_params=pltpu.CompilerParams(
            dimension_semantics=("arbitrary",),
            vmem_limit_bytes=VMEM_LIMIT_BYTES),
        name="out_proj_mlp",
    )(x2d, fox, mla, *consts)


def _rope_split_cols(wr):
    k = wr.shape[0]
    heads = wr.shape[1]
    per_group = LANES // ROPE_DIM
    first = wr[:, :, :ROPE_HALF].reshape(k, heads // per_group, per_group * ROPE_HALF)
    second = wr[:, :, ROPE_HALF:].reshape(k, heads // per_group, per_group * ROPE_HALF)
    return jnp.concatenate([first, second], axis=-1).reshape(k, heads * ROPE_DIM)


def _prep_weights(attn_norm_g, w_in, b_forget, q_norm_g, w_uq, kv_norm_g, w_ukv, fox_out_g,
                  mla_out_g, w_o, mlp_norm_g, w_up, w_down, final_norm_g):
    row = lambda v: v.reshape(1, -1).astype(F32)
    pad_lanes = lambda a: jnp.pad(a, ((0, 0), (0, LANES - a.shape[1])))
    wkr = w_in[:, OFF_KR:IN_COLS].reshape(D_MODEL, 1, ROPE_DIM)
    wkr = jnp.broadcast_to(wkr, (D_MODEL, LANES // ROPE_DIM, ROPE_DIM))
    uq = w_uq.reshape(Q_RANK, N_HEADS, MLA_QK_DIM)
    ukv = w_ukv.reshape(KV_RANK, N_HEADS, 2 * HEAD_DIM)
    tri = np.tril(np.ones((SCAN_CHUNK, SCAN_CHUNK), np.float32))
    return {
        "g_attn": row(attn_norm_g),
        "wq": w_in[:, OFF_FQ:OFF_FK].astype(BF16),
        "wk": w_in[:, OFF_FK:OFF_FV].astype(BF16),
        "wvt": w_in[:, OFF_FV:OFF_FF].T.astype(BF16),
        "wff": pad_lanes(w_in[:, OFF_FF:OFF_CQ]).astype(BF16),
        "bff": pad_lanes(row(b_forget)),
        "wcq": w_in[:, OFF_CQ:OFF_CKV].astype(BF16),
        "wckv": w_in[:, OFF_CKV:OFF_KR].astype(BF16),
        "wkr": _rope_split_cols(wkr).astype(BF16),
        "gq": row(q_norm_g),
        "gkv": row(kv_norm_g),
        "wuqn": uq[:, :, :HEAD_DIM].reshape(Q_RANK, GROUP_WIDTH).astype(BF16),
        "wuqr": _rope_split_cols(uq[:, :, HEAD_DIM:]).astype(BF16),
        "wuk": ukv[:, :, :HEAD_DIM].reshape(KV_RANK, GROUP_WIDTH).astype(BF16),
        "wuvt": ukv[:, :, HEAD_DIM:].reshape(KV_RANK, GROUP_WIDTH).T.astype(BF16),
        "tri": jnp.asarray(tri, BF16),
        "gfox": row(fox_out_g),
        "gmla": row(mla_out_g),
        "wo": w_o.astype(BF16),
        "gmlp": row(mlp_norm_g),
        "wup": w_up.astype(BF16),
        "wdown": w_down.astype(BF16),
        "gfin": row(final_norm_g),
    }


def _head_masks():
    pairs = N_HEADS // 2
    fox = np.zeros((pairs, 2, 2 * LANES), np.float32)
    mla = np.zeros((pairs, 2, 2 * LANES), np.float32)
    per_group = LANES // ROPE_DIM
    for p in range(pairs):
        for a in range(2):
            h = 2 * p + a
            fox[p, a, a * HEAD_DIM:(a + 1) * HEAD_DIM] = 1.0
            mla[p, a, a * HEAD_DIM:(a + 1) * HEAD_DIM] = 1.0
            for piece in range(3):
                fox[p, a, LANES + piece * N_HEADS + h] = 1.0
            hh = h % per_group
            for half in range(2):
                lo = LANES + half * (LANES // 2) + hh * ROPE_HALF
                mla[p, a, lo:lo + ROPE_HALF] = 1.0
    return jnp.asarray(fox), jnp.asarray(mla)


def kernel(x, positions, attn_norm_g, w_in, b_forget, q_norm_g, w_uq, kv_norm_g, w_ukv,
           fox_out_g, mla_out_g, w_o, mlp_norm_g, w_up, w_down, final_norm_g):
    b, s, d = x.shape
    depth = w_in.shape[0]
    inv_freq = ROPE_THETA ** (-jnp.arange(0, ROPE_DIM, 2, dtype=F32) / ROPE_DIM)
    cos_t, sin_t = _rope_tables(positions, inv_freq)
    fox_mask, mla_mask = _head_masks()
    ones_aux = jnp.ones((1, s, LANES), BF16)

    for l in range(depth):
        w = _prep_weights(attn_norm_g[l], w_in[l], b_forget[l], q_norm_g[l], w_uq[l],
                          kv_norm_g[l], w_ukv[l], fox_out_g[l], mla_out_g[l], w_o[l],
                          mlp_norm_g[l], w_up[l], w_down[l], final_norm_g)
        fq, fk, fvt, faux, mqn, mqr, mkn, mkr, mvt = _projection(x, cos_t, sin_t, w)
        fox = _attention(fq, ones_aux, fk, faux, fvt, fox_mask, False, 1)
        mla = _attention(mqn, mqr, mkn, mkr, mvt, mla_mask, True, 2)
        y = _out_mlp(x.reshape(b * s, d), fox.reshape(b * s, GROUP_WIDTH),
                     mla.reshape(b * s, GROUP_WIDTH), w, l == depth - 1)
        x = y.reshape(b, s, d)
    return x
```

```python
import functools
import math

import numpy as np
import jax
import jax.numpy as jnp
from jax import lax
from jax.experimental import pallas as pl
from jax.experimental.pallas import tpu as pltpu

D_MODEL = 1024
HEAD_DIM = 64
N_HEADS = 8
GROUP_WIDTH = N_HEADS * HEAD_DIM
ROPE_DIM = 32
ROPE_HALF = ROPE_DIM // 2
MLA_QK_DIM = HEAD_DIM + ROPE_DIM
Q_RANK = 384
KV_RANK = 256
D_FF = 4096
ROPE_THETA = 10000.0
EPS = 1e-6

OFF_FQ = 0
OFF_FK = OFF_FQ + GROUP_WIDTH
OFF_FV = OFF_FK + GROUP_WIDTH
OFF_FF = OFF_FV + GROUP_WIDTH
OFF_CQ = OFF_FF + N_HEADS
OFF_CKV = OFF_CQ + Q_RANK
OFF_KR = OFF_CKV + KV_RANK
IN_COLS = OFF_KR + ROPE_DIM

LANES = 128
VMEM_LIMIT_BYTES = 56 * 1024 * 1024

LOG2E = math.log2(math.e)
FOX_QSCALE = LOG2E / math.sqrt(HEAD_DIM)
MLA_QSCALE = LOG2E / math.sqrt(MLA_QK_DIM)

PROJ_ROWS = 512
ATTN_TQ = 512
ATTN_TK = 512
MLP_ROWS = 512
FF_CHUNK = 1024
SCAN_CHUNK = 128
ONES_ROWS = 16
UNROLL_STEPS = 14

F32 = jnp.float32
BF16 = jnp.bfloat16


def _rms(x, g):
    return x * lax.rsqrt(jnp.mean(x * x, axis=-1, keepdims=True) + EPS) * g


def _split3(x):
    hi = x.astype(BF16)
    r = x - hi.astype(F32)
    mid = r.astype(BF16)
    lo = (r - mid.astype(F32)).astype(BF16)
    return hi, mid, lo


def _rope_table_kernel(pos_ref, freq_ref, cos_ref, sin_ref):
    ang = pos_ref[...].astype(F32) * freq_ref[...]
    cos_ref[...] = jnp.cos(ang)
    sin_ref[...] = jnp.sin(ang)


def _rope_tables(positions, inv_freq):
    b, s = positions.shape
    per_row = LANES // ROPE_HALF
    rows = b * s // per_row
    pos_rep = jnp.repeat(positions.reshape(-1), ROPE_HALF).reshape(rows, LANES)
    freq_row = jnp.tile(inv_freq, per_row).reshape(1, LANES)
    blk = rows // 8
    cos, sin = pl.pallas_call(
        _rope_table_kernel,
        grid=(rows // blk,),
        in_specs=[pl.BlockSpec((blk, LANES), lambda i: (i, 0)),
                  pl.BlockSpec((1, LANES), lambda i: (0, 0))],
        out_specs=[pl.BlockSpec((blk, LANES), lambda i: (i, 0))] * 2,
        out_shape=[jax.ShapeDtypeStruct((rows, LANES), F32)] * 2,
        name="rope_tables",
    )(pos_rep, freq_row)
    cos_t = jnp.tile(cos.reshape(b, s, ROPE_HALF), (1, 1, per_row))
    sin_t = jnp.tile(sin.reshape(b, s, ROPE_HALF), (1, 1, per_row))
    return cos_t, sin_t


def _rope_rotate(x, cos_t, sin_signed):
    return x * cos_t + pltpu.roll(x, LANES // 2, axis=1) * sin_signed


def _proj_kernel(x_ref, cos_ref, sin_ref, g_attn_ref, wq_ref, wk_ref, wvt_ref, wff_ref,
                 bff_ref, wcq_ref, wckv_ref, wkr_ref, gq_ref, gkv_ref, wuqn_ref, wuqr_ref,
                 wuk_ref, wuvt_ref, tri_ref,
                 fq_ref, fk_ref, fvt_ref, faux_ref, mqn_ref, mqr_ref, mkn_ref, mkr_ref, mvt_ref,
                 h_sc, carry_sc):
    rows = x_ref.shape[1]

    @pl.when(pl.program_id(1) == 0)
    def _():
        carry_sc[...] = jnp.zeros_like(carry_sc)

    h_sc[...] = _rms(x_ref[0], g_attn_ref[...]).astype(BF16)
    h = h_sc[...]
    nt = (((1,), (1,)), ((), ()))

    fq_ref[0] = (jnp.dot(h, wq_ref[...], preferred_element_type=F32) * FOX_QSCALE).astype(BF16)
    fk_ref[0] = jnp.dot(h, wk_ref[...], preferred_element_type=F32).astype(BF16)
    fvt_ref[0] = lax.dot_general(wvt_ref[...], h, nt, preferred_element_type=F32).astype(BF16)

    lane = lax.broadcasted_iota(jnp.int32, (1, LANES), 1)
    cos_t = cos_ref[0]
    sin_s = sin_ref[0] * jnp.where(lane < LANES // 2, -1.0, 1.0)

    cq = jnp.dot(h, wcq_ref[...], preferred_element_type=F32)
    cq = _rms(cq, gq_ref[...]).astype(BF16)
    mqn_ref[0] = (jnp.dot(cq, wuqn_ref[...], preferred_element_type=F32) * MLA_QSCALE).astype(BF16)
    qr = jnp.dot(cq, wuqr_ref[...], preferred_element_type=F32)
    for g in range(qr.shape[1] // LANES):
        blk = _rope_rotate(qr[:, g * LANES:(g + 1) * LANES], cos_t, sin_s)
        mqr_ref[0, :, g * LANES:(g + 1) * LANES] = (blk * MLA_QSCALE).astype(BF16)

    ckv = jnp.dot(h, wckv_ref[...], preferred_element_type=F32)
    ckv = _rms(ckv, gkv_ref[...]).astype(BF16)
    mkn_ref[0] = jnp.dot(ckv, wuk_ref[...], preferred_element_type=F32).astype(BF16)
    mvt_ref[0] = lax.dot_general(wuvt_ref[...], ckv, nt, preferred_element_type=F32).astype(BF16)
    kr = jnp.dot(h, wkr_ref[...], preferred_element_type=F32)
    mkr_ref[0] = _rope_rotate(kr, cos_t, sin_s).astype(BF16)

    z = jnp.dot(h, wff_ref[...], preferred_element_type=F32) + bff_ref[...]
    log_f = jnp.where(lane < N_HEADS, jax.nn.log_sigmoid(z), 0.0)
    tri = tri_ref[...]
    carry = carry_sc[0:1, :]
    for c in range(rows // SCAN_CHUNK):
        hi, mid, lo = _split3(log_f[c * SCAN_CHUNK:(c + 1) * SCAN_CHUNK, :])
        cum = (jnp.dot(tri, hi, preferred_element_type=F32)
               + jnp.dot(tri, mid, preferred_element_type=F32)
               + jnp.dot(tri, lo, preferred_element_type=F32)) + carry
        carry = cum[SCAN_CHUNK - 1:SCAN_CHUNK, :]
        p_hi, p_mid, p_lo = _split3(cum * (-LOG2E))
        aux = (p_hi.astype(F32) + pltpu.roll(p_mid.astype(F32), N_HEADS, axis=1)
               + pltpu.roll(p_lo.astype(F32), 2 * N_HEADS, axis=1))
        faux_ref[0, c * SCAN_CHUNK:(c + 1) * SCAN_CHUNK, :] = aux.astype(BF16)
    carry_sc[...] = jnp.broadcast_to(carry, carry_sc.shape)


def _const_spec(shape):
    return pl.BlockSpec(shape, lambda *_: (0,) * len(shape))


def _projection(x, cos_t, sin_t, w):
    b, s, _ = x.shape
    tm = PROJ_ROWS
    row_spec = lambda width: pl.BlockSpec((1, tm, width), lambda i, j: (i, j, 0))
    col_spec = pl.BlockSpec((1, GROUP_WIDTH, tm), lambda i, j: (i, 0, j))
    consts = [w["g_attn"], w["wq"], w["wk"], w["wvt"], w["wff"], w["bff"], w["wcq"], w["wckv"],
              w["wkr"], w["gq"], w["gkv"], w["wuqn"], w["wuqr"], w["wuk"], w["wuvt"], w["tri"]]
    row_out = lambda width: jax.ShapeDtypeStruct((b, s, width), BF16)
    col_out = jax.ShapeDtypeStruct((b, GROUP_WIDTH, s), BF16)
    return pl.pallas_call(
        _proj_kernel,
        grid=(b, s // tm),
        in_specs=[row_spec(D_MODEL), row_spec(LANES), row_spec(LANES)]
                 + [_const_spec(c.shape) for c in consts],
        out_specs=[row_spec(GROUP_WIDTH), row_spec(GROUP_WIDTH), col_spec, row_spec(LANES),
                   row_spec(GROUP_WIDTH), row_spec(2 * LANES), row_spec(GROUP_WIDTH),
                   row_spec(LANES), col_spec],
        out_shape=[row_out(GROUP_WIDTH), row_out(GROUP_WIDTH), col_out, row_out(LANES),
                   row_out(GROUP_WIDTH), row_out(2 * LANES), row_out(GROUP_WIDTH),
                   row_out(LANES), col_out],
        scratch_shapes=[pltpu.VMEM((tm, D_MODEL), BF16), pltpu.VMEM((8, LANES), F32)],
        compiler_params=pltpu.CompilerParams(
            dimension_semantics=("arbitrary", "arbitrary"),
            vmem_limit_bytes=VMEM_LIMIT_BYTES),
        name="input_projection",
    )(x, cos_t, sin_t, *consts)


def _attn_kernel(tasks_ref, qm_ref, qa_ref, km_ref, ka_ref, vt_ref, mask_ref, o_ref,
                 bias_sc, qh_sc, s0_sc, s1_sc, acc_sc, m_sc):
    s_len = qm_ref.shape[1]
    tq, tk = ATTN_TQ, ATTN_TK
    nt = (((1,), (1,)), ((), ()))

    key_idx = lax.broadcasted_iota(jnp.int32, (tk, tq), 0)
    qry_idx = lax.broadcasted_iota(jnp.int32, (tk, tq), 1)
    bias_sc[...] = jnp.where(key_idx <= qry_idx, 0.0, -jnp.inf)

    def kv_tile(k0):
        return jnp.concatenate([km_ref[0, pl.ds(k0, tk), :], ka_ref[0, pl.ds(k0, tk), :]], axis=-1)

    ones_rows = jnp.ones((ONES_ROWS, tk), BF16)

    qcat = jnp.concatenate([qm_ref[0], qa_ref[0]], axis=-1)
    for a in range(2):
        qh_sc[a] = qcat * mask_ref[0, a:a + 1, :].astype(BF16)
    acc_sc[...] = jnp.zeros_like(acc_sc)
    m_sc[...] = jnp.full(m_sc.shape, -jnp.inf, F32)

    def produce(dst, t, a, diagonal):
        q0 = pl.multiple_of(tasks_ref[0, t] * tq, tq)
        k0 = pl.multiple_of(tasks_ref[1, t] * tk, tk)
        sc = lax.dot_general(kv_tile(k0), qh_sc[a, pl.ds(q0, tq), :], nt,
                             preferred_element_type=F32)
        if diagonal:
            sc = sc + bias_sc[...]
        dst[a] = sc
        return jnp.max(sc, axis=0, keepdims=True)

    def consume(src, t, a, maxima):
        i = tasks_ref[0, t]
        k0 = pl.multiple_of(tasks_ref[1, t] * tk, tk)
        m_run = m_sc[i, a]
        m_new = jnp.maximum(m_run, maxima)
        alpha = jnp.exp2(m_run - m_new)
        p = jnp.exp2(src[a] - m_new).astype(BF16)
        vt = jnp.concatenate(
            [vt_ref[0, a * HEAD_DIM:(a + 1) * HEAD_DIM, pl.ds(k0, tk)], ones_rows], axis=0)
        acc_sc[i, a] = alpha * acc_sc[i, a] + jnp.dot(vt, p, preferred_element_type=F32)
        m_sc[i, a] = m_new

    bufs = (s0_sc, s1_sc)

    def step(t, parity, diagonal, mx):
        mx_next = tuple(produce(bufs[1 - parity], t + 1, a, diagonal) for a in range(2))
        for a in range(2):
            consume(bufs[parity], t, a, mx[a])
        return mx_next

    def steps(first, count, diagonal):
        def body(it, mx):
            t = first + it * count
            for u in range(count):
                mx = step(t + u, (first + u) % 2, diagonal, mx)
            return mx
        return body

    n_diag = s_len // tq
    n_tasks = n_diag * (n_diag + 1) // 2
    n_off = n_tasks - n_diag
    assert n_diag % 2 == 0 and n_off % UNROLL_STEPS == 0 and UNROLL_STEPS % 2 == 0
    mx = tuple(produce(s0_sc, 0, a, True) for a in range(2))
    mx = lax.fori_loop(0, n_diag // 2 - 1, steps(0, 2, True), mx)
    mx = step(n_diag - 2, 0, True, mx)
    mx = lax.fori_loop(0, n_off // UNROLL_STEPS, steps(n_diag - 1, UNROLL_STEPS, False), mx)
    for a in range(2):
        consume(bufs[(n_tasks - 1) % 2], n_tasks - 1, a, mx[a])

    def finalize(i, _):
        q0 = pl.multiple_of(i * tq, tq)
        out_t = jnp.concatenate(
            [acc_sc[i, a, 0:HEAD_DIM, :] / acc_sc[i, a, HEAD_DIM:HEAD_DIM + 1, :]
             for a in range(2)], axis=0)
        o_ref[0, pl.ds(q0, tq), :] = out_t.T
        return 0

    lax.fori_loop(0, n_diag, finalize, 0)


def _attention(qm, qa, km, ka, vt, mask, qa_batched, qa_per_pairs):
    b, s, _ = qm.shape
    pairs = N_HEADS // 2
    tok = lambda f: pl.BlockSpec((1, s, LANES), f)
    qa_map = (lambda i, p: (i, 0, p // qa_per_pairs)) if qa_batched else (lambda i, p: (0, 0, 0))
    n_q = s // ATTN_TQ
    task_list = [(i, i) for i in range(n_q)] + [(i, j) for i in range(n_q) for j in range(i)]
    tasks = jnp.asarray(np.array(task_list, np.int32).T)
    return pl.pallas_call(
        _attn_kernel,
        grid=(b, pairs),
        in_specs=[pl.BlockSpec(memory_space=pltpu.SMEM),
                  tok(lambda i, p: (i, 0, p)), tok(qa_map), tok(lambda i, p: (i, 0, p)),
                  tok(lambda i, p: (i, 0, 0)),
                  pl.BlockSpec((1, LANES, s), lambda i, p: (i, p, 0)),
                  pl.BlockSpec((1, 2, 2 * LANES), lambda i, p: (p, 0, 0))],
        out_specs=tok(lambda i, p: (i, 0, p)),
        out_shape=jax.ShapeDtypeStruct((b, s, GROUP_WIDTH), F32),
        scratch_shapes=[pltpu.VMEM((ATTN_TK, ATTN_TQ), F32),
                        pltpu.VMEM((2, s, 2 * LANES), BF16),
                        pltpu.VMEM((2, ATTN_TK, ATTN_TQ), F32),
                        pltpu.VMEM((2, ATTN_TK, ATTN_TQ), F32),
                        pltpu.VMEM((n_q, 2, HEAD_DIM + ONES_ROWS, ATTN_TQ), F32),
                        pltpu.VMEM((n_q, 2, 1, ATTN_TQ), F32)],
        compiler_params=pltpu.CompilerParams(
            dimension_semantics=("arbitrary", "arbitrary"),
            vmem_limit_bytes=VMEM_LIMIT_BYTES),
        name="causal_attention",
    )(tasks, qm, qa, km, ka, vt, mask)


def _mlp_kernel(x_ref, fox_ref, mla_ref, gfox_ref, gmla_ref, wo_ref, gmlp_ref, wup_ref,
                wdown_ref, gfin_ref, o_ref, *, final):
    mixed = jnp.concatenate([_rms(fox_ref[...], gfox_ref[...]).astype(BF16),
                             _rms(mla_ref[...], gmla_ref[...]).astype(BF16)], axis=-1)
    x1 = x_ref[...] + jnp.dot(mixed, wo_ref[...], preferred_element_type=F32)
    h = _rms(x1, gmlp_ref[...]).astype(BF16)
    y = x1
    for c in range(D_FF // FF_CHUNK):
        u = jnp.dot(h, wup_ref[:, c * FF_CHUNK:(c + 1) * FF_CHUNK], preferred_element_type=F32)
        act = jnp.square(jnp.maximum(u, 0.0)).astype(BF16)
        y = y + jnp.dot(act, wdown_ref[c * FF_CHUNK:(c + 1) * FF_CHUNK, :],
                        preferred_element_type=F32)
    o_ref[...] = _rms(y, gfin_ref[...]) if final else y


def _out_mlp(x2d, fox, mla, w, final):
    n = x2d.shape[0]
    tm = MLP_ROWS
    row = lambda width: pl.BlockSpec((tm, width), lambda i: (i, 0))
    resident = lambda a: pl.BlockSpec(a.shape, lambda i: (0,) * a.ndim,
                                      pipeline_mode=pl.Buffered(1))
    consts = [w["gfox"], w["gmla"], w["wo"], w["gmlp"], w["wup"], w["wdown"], w["gfin"]]
    return pl.pallas_call(
        functools.partial(_mlp_kernel, final=final),
        grid=(n // tm,),
        in_specs=[row(D_MODEL), row(GROUP_WIDTH), row(GROUP_WIDTH)] + [resident(c) for c in consts],
        out_specs=row(D_MODEL),
        out_shape=jax.ShapeDtypeStruct((n, D_MODEL), F32),
        compiler_params=pltpu.CompilerParams(
            dimension_semantics=("arbitrary",),
            vmem_limit_bytes=VMEM_LIMIT_BYTES),
        name="out_proj_mlp",
    )(x2d, fox, mla, *consts)


def _rope_split_cols(wr):
    k = wr.shape[0]
    heads = wr.shape[1]
    per_group = LANES // ROPE_DIM
    first = wr[:, :, :ROPE_HALF].reshape(k, heads // per_group, per_group * ROPE_HALF)
    second = wr[:, :, ROPE_HALF:].reshape(k, heads // per_group, per_group * ROPE_HALF)
    return jnp.concatenate([first, second], axis=-1).reshape(k, heads * ROPE_DIM)


def _prep_weights(attn_norm_g, w_in, b_forget, q_norm_g, w_uq, kv_norm_g, w_ukv, fox_out_g,
                  mla_out_g, w_o, mlp_norm_g, w_up, w_down, final_norm_g):
    row = lambda v: v.reshape(1, -1).astype(F32)
    pad_lanes = lambda a: jnp.pad(a, ((0, 0), (0, LANES - a.shape[1])))
    wkr = w_in[:, OFF_KR:IN_COLS].reshape(D_MODEL, 1, ROPE_DIM)
    wkr = jnp.broadcast_to(wkr, (D_MODEL, LANES // ROPE_DIM, ROPE_DIM))
    uq = w_uq.reshape(Q_RANK, N_HEADS, MLA_QK_DIM)
    ukv = w_ukv.reshape(KV_RANK, N_HEADS, 2 * HEAD_DIM)
    tri = np.tril(np.ones((SCAN_CHUNK, SCAN_CHUNK), np.float32))
    return {
        "g_attn": row(attn_norm_g),
        "wq": w_in[:, OFF_FQ:OFF_FK].astype(BF16),
        "wk": w_in[:, OFF_FK:OFF_FV].astype(BF16),
        "wvt": w_in[:, OFF_FV:OFF_FF].T.astype(BF16),
        "wff": pad_lanes(w_in[:, OFF_FF:OFF_CQ]).astype(BF16),
        "bff": pad_lanes(row(b_forget)),
        "wcq": w_in[:, OFF_CQ:OFF_CKV].astype(BF16),
        "wckv": w_in[:, OFF_CKV:OFF_KR].astype(BF16),
        "wkr": _rope_split_cols(wkr).astype(BF16),
        "gq": row(q_norm_g),
        "gkv": row(kv_norm_g),
        "wuqn": uq[:, :, :HEAD_DIM].reshape(Q_RANK, GROUP_WIDTH).astype(BF16),
        "wuqr": _rope_split_cols(uq[:, :, HEAD_DIM:]).astype(BF16),
        "wuk": ukv[:, :, :HEAD_DIM].reshape(KV_RANK, GROUP_WIDTH).astype(BF16),
        "wuvt": ukv[:, :, HEAD_DIM:].reshape(KV_RANK, GROUP_WIDTH).T.astype(BF16),
        "tri": jnp.asarray(tri, BF16),
        "gfox": row(fox_out_g),
        "gmla": row(mla_out_g),
        "wo": w_o.astype(BF16),
        "gmlp": row(mlp_norm_g),
        "wup": w_up.astype(BF16),
        "wdown": w_down.astype(BF16),
        "gfin": row(final_norm_g),
    }


def _head_masks():
    pairs = N_HEADS // 2
    fox = np.zeros((pairs, 2, 2 * LANES), np.float32)
    mla = np.zeros((pairs, 2, 2 * LANES), np.float32)
    per_group = LANES // ROPE_DIM
    for p in range(pairs):
        for a in range(2):
            h = 2 * p + a
            fox[p, a, a * HEAD_DIM:(a + 1) * HEAD_DIM] = 1.0
            mla[p, a, a * HEAD_DIM:(a + 1) * HEAD_DIM] = 1.0
            for piece in range(3):
                fox[p, a, LANES + piece * N_HEADS + h] = 1.0
            hh = h % per_group
            for half in range(2):
                lo = LANES + half * (LANES // 2) + hh * ROPE_HALF
                mla[p, a, lo:lo + ROPE_HALF] = 1.0
    return jnp.asarray(fox), jnp.asarray(mla)


def kernel(x, positions, attn_norm_g, w_in, b_forget, q_norm_g, w_uq, kv_norm_g, w_ukv,
           fox_out_g, mla_out_g, w_o, mlp_norm_g, w_up, w_down, final_norm_g):
    b, s, d = x.shape
    depth = w_in.shape[0]
    inv_freq = ROPE_THETA ** (-jnp.arange(0, ROPE_DIM, 2, dtype=F32) / ROPE_DIM)
    cos_t, sin_t = _rope_tables(positions, inv_freq)
    fox_mask, mla_mask = _head_masks()
    ones_aux = jnp.ones((1, s, LANES), BF16)

    for l in range(depth):
        w = _prep_weights(attn_norm_g[l], w_in[l], b_forget[l], q_norm_g[l], w_uq[l],
                          kv_norm_g[l], w_ukv[l], fox_out_g[l], mla_out_g[l], w_o[l],
                          mlp_norm_g[l], w_up[l], w_down[l], final_norm_g)
        fq, fk, fvt, faux, mqn, mqr, mkn, mkr, mvt = _projection(x, cos_t, sin_t, w)
        fox = _attention(fq, ones_aux, fk, faux, fvt, fox_mask, False, 1)
        mla = _attention(mqn, mqr, mkn, mkr, mvt, mla_mask, True, 2)
        y = _out_mlp(x.reshape(b * s, d), fox.reshape(b * s, GROUP_WIDTH),
                     mla.reshape(b * s, GROUP_WIDTH), w, l == depth - 1)
        x = y.reshape(b, s, d)
    return x
```

```python
import functools
import math

import numpy as np
import jax
import jax.numpy as jnp
from jax import lax
from jax.experimental import pallas as pl
from jax.experimental.pallas import tpu as pltpu

D_MODEL = 1024
HEAD_DIM = 64
N_HEADS = 8
GROUP_WIDTH = N_HEADS * HEAD_DIM
ROPE_DIM = 32
ROPE_HALF = ROPE_DIM // 2
MLA_QK_DIM = HEAD_DIM + ROPE_DIM
Q_RANK = 384
KV_RANK = 256
D_FF = 4096
ROPE_THETA = 10000.0
EPS = 1e-6

OFF_FQ = 0
OFF_FK = OFF_FQ + GROUP_WIDTH
OFF_FV = OFF_FK + GROUP_WIDTH
OFF_FF = OFF_FV + GROUP_WIDTH
OFF_CQ = OFF_FF + N_HEADS
OFF_CKV = OFF_CQ + Q_RANK
OFF_KR = OFF_CKV + KV_RANK
IN_COLS = OFF_KR + ROPE_DIM

LANES = 128
VMEM_LIMIT_BYTES = 56 * 1024 * 1024

LOG2E = math.log2(math.e)
FOX_QSCALE = LOG2E / math.sqrt(HEAD_DIM)
MLA_QSCALE = LOG2E / math.sqrt(MLA_QK_DIM)

PROJ_ROWS = 512
ATTN_TQ = 512
ATTN_TK = 512
MLP_ROWS = 512
FF_CHUNK = 1024
SCAN_CHUNK = 128
ONES_ROWS = 16
UNROLL_STEPS = 14

F32 = jnp.float32
BF16 = jnp.bfloat16
NT_DIMS = (((1,), (1,)), ((), ()))


def _rms(x, g):
    return x * lax.rsqrt(jnp.mean(x * x, axis=-1, keepdims=True) + EPS) * g


def _split3(x):
    hi = x.astype(BF16).astype(F32)
    mid = (x - hi).astype(BF16).astype(F32)
    lo = (x - hi - mid).astype(BF16).astype(F32)
    return hi, mid, lo


def _pack3(x):
    hi, mid, lo = _split3(x)
    packed = hi + pltpu.roll(mid, N_HEADS, axis=1) + pltpu.roll(lo, 2 * N_HEADS, axis=1)
    return packed.astype(BF16)


def _rope_table_kernel(pos_ref, freq_ref, cos_ref, sin_ref):
    ang = pos_ref[0].astype(F32) * freq_ref[...]
    cos_ref[0] = jnp.cos(ang)
    sin_ref[0] = jnp.sin(ang)


def _rope_tables(positions, inv_freq):
    b, s = positions.shape
    table = pl.BlockSpec((1, ROPE_HALF, s), lambda i: (i, 0, 0))
    return pl.pallas_call(
        _rope_table_kernel,
        grid=(b,),
        in_specs=[pl.BlockSpec((1, 1, s), lambda i: (i, 0, 0)),
                  pl.BlockSpec((ROPE_HALF, 1), lambda i: (0, 0))],
        out_specs=[table, table],
        out_shape=[jax.ShapeDtypeStruct((b, ROPE_HALF, s), F32)] * 2,
        name="rope_tables",
    )(positions.reshape(b, 1, s), inv_freq.reshape(ROPE_HALF, 1))


def _proj_kernel(x_ref, cos_ref, sin_ref, g_attn_ref, w_rows_ref, w_cols_ref, bff_ref, gq_ref,
                 gkv_ref, wuq_ref, wuk_ref, wuvt_ref, tri_ref,
                 fqt_ref, fk_ref, fvt_ref, faux_ref, mqnt_ref, mqrt_ref, mkn_ref, mkr_ref, mvt_ref,
                 h_sc, carry_sc):
    rows = x_ref.shape[1]

    @pl.when(pl.program_id(1) == 0)
    def _():
        carry_sc[...] = jnp.zeros_like(carry_sc)

    h_sc[...] = _rms(x_ref[0], g_attn_ref[...]).astype(BF16)
    h = h_sc[...]

    groups = LANES // ROPE_HALF
    row = lax.broadcasted_iota(jnp.int32, (LANES, rows), 0)
    cos_t = jnp.tile(cos_ref[0], (groups, 1))
    sin_t = jnp.tile(sin_ref[0], (groups, 1))
    sin_s = jnp.where(row < LANES // 2, -sin_t, sin_t)

    def rope_t(xt):
        return xt * cos_t + pltpu.roll(xt, LANES // 2, axis=0) * sin_s

    ft = lax.dot_general(w_rows_ref[...], h, NT_DIMS, preferred_element_type=F32)
    fqt_ref[0] = (ft[0:GROUP_WIDTH] * FOX_QSCALE).astype(BF16)
    fvt_ref[0] = ft[GROUP_WIDTH:2 * GROUP_WIDTH].astype(BF16)
    mkr_ref[0] = rope_t(ft[2 * GROUP_WIDTH:2 * GROUP_WIDTH + LANES]).T.astype(BF16)

    tk = jnp.dot(h, w_cols_ref[...], preferred_element_type=F32)
    o_cq, o_ckv, o_ff = GROUP_WIDTH, GROUP_WIDTH + Q_RANK, GROUP_WIDTH + Q_RANK + KV_RANK
    fk_ref[0] = tk[:, 0:o_cq].astype(BF16)

    cq = _rms(tk[:, o_cq:o_ckv], gq_ref[...]).astype(BF16)
    qt = lax.dot_general(wuq_ref[...], cq, NT_DIMS, preferred_element_type=F32)
    mqnt_ref[0] = (qt[0:GROUP_WIDTH] * MLA_QSCALE).astype(BF16)
    for g in range(2 * N_HEADS * ROPE_HALF // LANES):
        lo = GROUP_WIDTH + g * LANES
        mqrt_ref[0, g * LANES:(g + 1) * LANES, :] = (
            rope_t(qt[lo:lo + LANES]) * MLA_QSCALE).astype(BF16)

    ckv = _rms(tk[:, o_ckv:o_ff], gkv_ref[...]).astype(BF16)
    mkn_ref[0] = jnp.dot(ckv, wuk_ref[...], preferred_element_type=F32).astype(BF16)
    mvt_ref[0] = lax.dot_general(wuvt_ref[...], ckv, NT_DIMS,
                                 preferred_element_type=F32).astype(BF16)

    lane = lax.broadcasted_iota(jnp.int32, (1, LANES), 1)
    head_lane = lane < N_HEADS
    z = tk[:, o_ff:o_ff + LANES] + bff_ref[...]
    log_f = jnp.where(head_lane, jax.nn.log_sigmoid(z), 0.0)
    tri = tri_ref[...]
    carry = carry_sc[0:1, :]
    for c in range(rows // SCAN_CHUNK):
        part = jnp.dot(tri, _pack3(log_f[c * SCAN_CHUNK:(c + 1) * SCAN_CHUNK, :]),
                       preferred_element_type=F32)
        part = (part + pltpu.roll(part, LANES - N_HEADS, axis=1)
                + pltpu.roll(part, LANES - 2 * N_HEADS, axis=1))
        cum = jnp.where(head_lane, part, 0.0) + carry
        carry = cum[SCAN_CHUNK - 1:SCAN_CHUNK, :]
        faux_ref[0, c * SCAN_CHUNK:(c + 1) * SCAN_CHUNK, :] = _pack3(cum * (-LOG2E))
    carry_sc[...] = jnp.broadcast_to(carry, carry_sc.shape)


def _const_spec(shape):
    return pl.BlockSpec(shape, lambda *_: (0,) * len(shape))


def _projection(x, cos_t, sin_t, w):
    b, s, _ = x.shape
    tm = PROJ_ROWS
    row_spec = lambda width: pl.BlockSpec((1, tm, width), lambda i, j: (i, j, 0))
    col_spec = lambda height: pl.BlockSpec((1, height, tm), lambda i, j: (i, 0, j))
    consts = [w["g_attn"], w["w_rows"], w["w_cols"], w["bff"], w["gq"], w["gkv"], w["wuq"],
              w["wuk"], w["wuvt"], w["tri"]]
    row_out = lambda width: jax.ShapeDtypeStruct((b, s, width), BF16)
    col_out = lambda height: jax.ShapeDtypeStruct((b, height, s), BF16)
    return pl.pallas_call(
        _proj_kernel,
        grid=(b, s // tm),
        in_specs=[row_spec(D_MODEL), col_spec(ROPE_HALF), col_spec(ROPE_HALF)]
                 + [_const_spec(c.shape) for c in consts],
        out_specs=[col_spec(GROUP_WIDTH), row_spec(GROUP_WIDTH), col_spec(GROUP_WIDTH),
                   row_spec(LANES), col_spec(GROUP_WIDTH), col_spec(2 * LANES),
                   row_spec(GROUP_WIDTH), row_spec(LANES), col_spec(GROUP_WIDTH)],
        out_shape=[col_out(GROUP_WIDTH), row_out(GROUP_WIDTH), col_out(GROUP_WIDTH),
                   row_out(LANES), col_out(GROUP_WIDTH), col_out(2 * LANES),
                   row_out(GROUP_WIDTH), row_out(LANES), col_out(GROUP_WIDTH)],
        scratch_shapes=[pltpu.VMEM((tm, D_MODEL), BF16), pltpu.VMEM((8, LANES), F32)],
        compiler_params=pltpu.CompilerParams(
            dimension_semantics=("arbitrary", "arbitrary"),
            vmem_limit_bytes=VMEM_LIMIT_BYTES),
        name="input_projection",
    )(x, cos_t, sin_t, *consts)


def _attn_kernel(tasks_ref, qm_ref, qa_ref, km_ref, ka_ref, vt_ref, mask_ref, o_ref,
                 bias_sc, qh_sc, s0_sc, s1_sc, acc_sc, m_sc):
    s_len = km_ref.shape[1]
    tq, tk = ATTN_TQ, ATTN_TK

    key_idx = lax.broadcasted_iota(jnp.int32, (tk, tq), 0)
    qry_idx = lax.broadcasted_iota(jnp.int32, (tk, tq), 1)
    bias_sc[...] = jnp.where(key_idx <= qry_idx, 0.0, -jnp.inf)

    def kv_tile(k0):
        return jnp.concatenate([km_ref[0, pl.ds(k0, tk), :], ka_ref[0, pl.ds(k0, tk), :]], axis=-1)

    ones_rows = jnp.ones((ONES_ROWS, tk), BF16)

    qcat = jnp.concatenate([qm_ref[0], qa_ref[0]], axis=0)
    for a in range(2):
        qh_sc[a] = qcat * jnp.tile(mask_ref[0, a], (1, s_len // LANES))
    acc_sc[...] = jnp.zeros_like(acc_sc)
    m_sc[...] = jnp.full(m_sc.shape, -jnp.inf, F32)

    def produce(dst, t, a, diagonal):
        q0 = pl.multiple_of(tasks_ref[0, t] * tq, tq)
        k0 = pl.multiple_of(tasks_ref[1, t] * tk, tk)
        sc = jnp.dot(kv_tile(k0), qh_sc[a, :, pl.ds(q0, tq)], preferred_element_type=F32)
        if diagonal:
            sc = sc + bias_sc[...]
        dst[a] = sc
        return jnp.max(sc, axis=0, keepdims=True)

    def consume(src, t, a, maxima):
        i = tasks_ref[0, t]
        k0 = pl.multiple_of(tasks_ref[1, t] * tk, tk)
        m_run = m_sc[i, a]
        m_new = jnp.maximum(m_run, maxima)
        alpha = jnp.exp2(m_run - m_new)
        p = jnp.exp2(src[a] - m_new).astype(BF16)
        vt = jnp.concatenate(
            [vt_ref[0, a * HEAD_DIM:(a + 1) * HEAD_DIM, pl.ds(k0, tk)], ones_rows], axis=0)
        acc_sc[i, a] = alpha * acc_sc[i, a] + jnp.dot(vt, p, preferred_element_type=F32)
        m_sc[i, a] = m_new

    bufs = (s0_sc, s1_sc)

    def step(t, parity, diagonal, mx):
        mx_next = tuple(produce(bufs[1 - parity], t + 1, a, diagonal) for a in range(2))
        for a in range(2):
            consume(bufs[parity], t, a, mx[a])
        return mx_next

    def steps(first, count, diagonal):
        def body(it, mx):
            t = first + it * count
            for u in range(count):
                mx = step(t + u, (first + u) % 2, diagonal, mx)
            return mx
        return body

    n_diag = s_len // tq
    n_tasks = n_diag * (n_diag + 1) // 2
    n_off = n_tasks - n_diag
    assert n_diag % 2 == 0 and n_off % UNROLL_STEPS == 0 and UNROLL_STEPS % 2 == 0
    mx = tuple(produce(s0_sc, 0, a, True) for a in range(2))
    mx = lax.fori_loop(0, n_diag // 2 - 1, steps(0, 2, True), mx)
    mx = step(n_diag - 2, 0, True, mx)
    mx = lax.fori_loop(0, n_off // UNROLL_STEPS, steps(n_diag - 1, UNROLL_STEPS, False), mx)
    for a in range(2):
        consume(bufs[(n_tasks - 1) % 2], n_tasks - 1, a, mx[a])

    def finalize(i, _):
        q0 = pl.multiple_of(i * tq, tq)
        out_t = jnp.concatenate(
            [acc_sc[i, a, 0:HEAD_DIM, :] / acc_sc[i, a, HEAD_DIM:HEAD_DIM + 1, :]
             for a in range(2)], axis=0)
        o_ref[0, pl.ds(q0, tq), :] = out_t.T
        return 0

    lax.fori_loop(0, n_diag, finalize, 0)


def _attention(qmt, qat, km, ka, vt, mask, qa_batched, qa_per_pairs):
    b, s, _ = km.shape
    pairs = N_HEADS // 2
    tok = lambda f: pl.BlockSpec((1, s, LANES), f)
    feat = lambda f: pl.BlockSpec((1, LANES, s), f)
    qa_map = (lambda i, p: (i, p // qa_per_pairs, 0)) if qa_batched else (lambda i, p: (0, 0, 0))
    n_q = s // ATTN_TQ
    task_list = [(i, i) for i in range(n_q)] + [(i, j) for i in range(n_q) for j in range(i)]
    tasks = jnp.asarray(np.array(task_list, np.int32).T)
    return pl.pallas_call(
        _attn_kernel,
        grid=(b, pairs),
        in_specs=[pl.BlockSpec(memory_space=pltpu.SMEM),
                  feat(lambda i, p: (i, p, 0)), feat(qa_map), tok(lambda i, p: (i, 0, p)),
                  tok(lambda i, p: (i, 0, 0)), feat(lambda i, p: (i, p, 0)),
                  pl.BlockSpec((1, 2, 2 * LANES, LANES), lambda i, p: (p, 0, 0, 0))],
        out_specs=tok(lambda i, p: (i, 0, p)),
        out_shape=jax.ShapeDtypeStruct((b, s, GROUP_WIDTH), F32),
        scratch_shapes=[pltpu.VMEM((ATTN_TK, ATTN_TQ), F32),
                        pltpu.VMEM((2, 2 * LANES, s), BF16),
                        pltpu.VMEM((2, ATTN_TK, ATTN_TQ), F32),
                        pltpu.VMEM((2, ATTN_TK, ATTN_TQ), F32),
                        pltpu.VMEM((n_q, 2, HEAD_DIM + ONES_ROWS, ATTN_TQ), F32),
                        pltpu.VMEM((n_q, 2, 1, ATTN_TQ), F32)],
        compiler_params=pltpu.CompilerParams(
            dimension_semantics=("arbitrary", "arbitrary"),
            vmem_limit_bytes=VMEM_LIMIT_BYTES),
        name="causal_attention",
    )(tasks, qmt, qat, km, ka, vt, mask)


def _mlp_kernel(x_ref, fox_ref, mla_ref, gfox_ref, gmla_ref, wo_ref, gmlp_ref, wup_ref,
                wdown_ref, gfin_ref, o_ref, *, final):
    mixed = jnp.concatenate([_rms(fox_ref[...], gfox_ref[...]).astype(BF16),
                             _rms(mla_ref[...], gmla_ref[...]).astype(BF16)], axis=-1)
    x1 = x_ref[...] + jnp.dot(mixed, wo_ref[...], preferred_element_type=F32)
    h = _rms(x1, gmlp_ref[...]).astype(BF16)
    y = x1
    for c in range(D_FF // FF_CHUNK):
        u = jnp.dot(h, wup_ref[:, c * FF_CHUNK:(c + 1) * FF_CHUNK], preferred_element_type=F32)
        act = jnp.square(jnp.maximum(u, 0.0)).astype(BF16)
        y = y + jnp.dot(act, wdown_ref[c * FF_CHUNK:(c + 1) * FF_CHUNK, :],
                        preferred_element_type=F32)
    o_ref[...] = _rms(y, gfin_ref[...]) if final else y


def _out_mlp(x2d, fox, mla, w, final):
    n = x2d.shape[0]
    tm = MLP_ROWS
    row = lambda width: pl.BlockSpec((tm, width), lambda i: (i, 0))
    resident = lambda a: pl.BlockSpec(a.shape, lambda i: (0,) * a.ndim,
                                      pipeline_mode=pl.Buffered(1))
    consts = [w["gfox"], w["gmla"], w["wo"], w["gmlp"], w["wup"], w["wdown"], w["gfin"]]
    return pl.pallas_call(
        functools.partial(_mlp_kernel, final=final),
        grid=(n // tm,),
        in_specs=[row(D_MODEL), row(GROUP_WIDTH), row(GROUP_WIDTH)] + [resident(c) for c in consts],
        out_specs=row(D_MODEL),
        out_shape=jax.ShapeDtypeStruct((n, D_MODEL), F32),
        compiler_params=pltpu.CompilerParams(
            dimension_semantics=("arbitrary",),
            vmem_limit_bytes=VMEM_LIMIT_BYTES),
        name="out_proj_mlp",
    )(x2d, fox, mla, *consts)


def _rope_split_cols(wr):
    k = wr.shape[0]
    heads = wr.shape[1]
    per_group = LANES // ROPE_DIM
    first = wr[:, :, :ROPE_HALF].reshape(k, heads // per_group, per_group * ROPE_HALF)
    second = wr[:, :, ROPE_HALF:].reshape(k, heads // per_group, per_group * ROPE_HALF)
    return jnp.concatenate([first, second], axis=-1).reshape(k, heads * ROPE_DIM)


def _prep_weights(attn_norm_g, w_in, b_forget, q_norm_g, w_uq, kv_norm_g, w_ukv, fox_out_g,
                  mla_out_g, w_o, mlp_norm_g, w_up, w_down, final_norm_g):
    row = lambda v: v.reshape(1, -1).astype(F32)
    pad_lanes = lambda a: jnp.pad(a, ((0, 0), (0, LANES - a.shape[1])))
    wkr = w_in[:, OFF_KR:IN_COLS].reshape(D_MODEL, 1, ROPE_DIM)
    wkr = _rope_split_cols(jnp.broadcast_to(wkr, (D_MODEL, LANES // ROPE_DIM, ROPE_DIM)))
    uq = w_uq.reshape(Q_RANK, N_HEADS, MLA_QK_DIM)
    ukv = w_ukv.reshape(KV_RANK, N_HEADS, 2 * HEAD_DIM)
    tri = np.tril(np.ones((SCAN_CHUNK, SCAN_CHUNK), np.float32))
    w_rows = jnp.concatenate([w_in[:, OFF_FQ:OFF_FK], w_in[:, OFF_FV:OFF_FF], wkr], axis=1)
    w_cols = jnp.concatenate([w_in[:, OFF_FK:OFF_FV], w_in[:, OFF_CQ:OFF_CKV],
                              w_in[:, OFF_CKV:OFF_KR], pad_lanes(w_in[:, OFF_FF:OFF_CQ])], axis=1)
    wuq = jnp.concatenate([uq[:, :, :HEAD_DIM].reshape(Q_RANK, GROUP_WIDTH),
                           _rope_split_cols(uq[:, :, HEAD_DIM:])], axis=1)
    return {
        "g_attn": row(attn_norm_g),
        "w_rows": w_rows.T.astype(BF16),
        "w_cols": w_cols.astype(BF16),
        "bff": pad_lanes(row(b_forget)),
        "gq": row(q_norm_g),
        "gkv": row(kv_norm_g),
        "wuq": wuq.T.astype(BF16),
        "wuk": ukv[:, :, :HEAD_DIM].reshape(KV_RANK, GROUP_WIDTH).astype(BF16),
        "wuvt": ukv[:, :, HEAD_DIM:].reshape(KV_RANK, GROUP_WIDTH).T.astype(BF16),
        "tri": jnp.asarray(tri, BF16),
        "gfox": row(fox_out_g),
        "gmla": row(mla_out_g),
        "wo": w_o.astype(BF16),
        "gmlp": row(mlp_norm_g),
        "wup": w_up.astype(BF16),
        "wdown": w_down.astype(BF16),
        "gfin": row(final_norm_g),
    }


def _head_masks():
    pairs = N_HEADS // 2
    fox = np.zeros((pairs, 2, 2 * LANES), np.float32)
    mla = np.zeros((pairs, 2, 2 * LANES), np.float32)
    per_group = LANES // ROPE_DIM
    for p in range(pairs):
        for a in range(2):
            h = 2 * p + a
            fox[p, a, a * HEAD_DIM:(a + 1) * HEAD_DIM] = 1.0
            mla[p, a, a * HEAD_DIM:(a + 1) * HEAD_DIM] = 1.0
            for piece in range(3):
                fox[p, a, LANES + piece * N_HEADS + h] = 1.0
            hh = h % per_group
            for half in range(2):
                lo = LANES + half * (LANES // 2) + hh * ROPE_HALF
                mla[p, a, lo:lo + ROPE_HALF] = 1.0
    rep = lambda m: jnp.asarray(np.repeat(m[..., None], LANES, axis=-1), BF16)
    return rep(fox), rep(mla)


def kernel(x, positions, attn_norm_g, w_in, b_forget, q_norm_g, w_uq, kv_norm_g, w_ukv,
           fox_out_g, mla_out_g, w_o, mlp_norm_g, w_up, w_down, final_norm_g):
    b, s, d = x.shape
    depth = w_in.shape[0]
    inv_freq = ROPE_THETA ** (-jnp.arange(0, ROPE_DIM, 2, dtype=F32) / ROPE_DIM)
    cos_t, sin_t = _rope_tables(positions, inv_freq)
    fox_mask, mla_mask = _head_masks()
    ones_aux = jnp.ones((1, LANES, s), BF16)

    for l in range(depth):
        w = _prep_weights(attn_norm_g[l], w_in[l], b_forget[l], q_norm_g[l], w_uq[l],
                          kv_norm_g[l], w_ukv[l], fox_out_g[l], mla_out_g[l], w_o[l],
                          mlp_norm_g[l], w_up[l], w_down[l], final_norm_g)
        fqt, fk, fvt, faux, mqnt, mqrt, mkn, mkr, mvt = _projection(x, cos_t, sin_t, w)
        fox = _attention(fqt, ones_aux, fk, faux, fvt, fox_mask, False, 1)
        mla = _attention(mqnt, mqrt, mkn, mkr, mvt, mla_mask, True, 2)
        y = _out_mlp(x.reshape(b * s, d), fox.reshape(b * s, GROUP_WIDTH),
                     mla.reshape(b * s, GROUP_WIDTH), w, l == depth - 1)
        x = y.reshape(b, s, d)
    return x
```

```python
import functools
import math

import numpy as np
import jax
import jax.numpy as jnp
from jax import lax
from jax.experimental import pallas as pl
from jax.experimental.pallas import tpu as pltpu

D_MODEL = 1024
HEAD_DIM = 64
N_HEADS = 8
GROUP_WIDTH = N_HEADS * HEAD_DIM
ROPE_DIM = 32
ROPE_HALF = ROPE_DIM // 2
MLA_QK_DIM = HEAD_DIM + ROPE_DIM
Q_RANK = 384
KV_RANK = 256
D_FF = 4096
ROPE_THETA = 10000.0
EPS = 1e-6

OFF_FQ = 0
OFF_FK = OFF_FQ + GROUP_WIDTH
OFF_FV = OFF_FK + GROUP_WIDTH
OFF_FF = OFF_FV + GROUP_WIDTH
OFF_CQ = OFF_FF + N_HEADS
OFF_CKV = OFF_CQ + Q_RANK
OFF_KR = OFF_CKV + KV_RANK
IN_COLS = OFF_KR + ROPE_DIM

LANES = 128
VMEM_LIMIT_BYTES = 56 * 1024 * 1024

LOG2E = math.log2(math.e)
FOX_QSCALE = LOG2E / math.sqrt(HEAD_DIM)
MLA_QSCALE = LOG2E / math.sqrt(MLA_QK_DIM)

PROJ_ROWS = 1024
ATTN_TQ = 512
ATTN_TK = 512
MLP_ROWS = 512
FF_CHUNK = 1024
SCAN_CHUNK = 128
ONES_ROWS = 16
UNROLL_STEPS = 14

F32 = jnp.float32
BF16 = jnp.bfloat16
NT_DIMS = (((1,), (1,)), ((), ()))


def _rms(x, g):
    return x * lax.rsqrt(jnp.mean(x * x, axis=-1, keepdims=True) + EPS) * g


def _split3(x):
    hi = x.astype(BF16).astype(F32)
    mid = (x - hi).astype(BF16).astype(F32)
    lo = (x - hi - mid).astype(BF16).astype(F32)
    return hi, mid, lo


def _pack3(x):
    hi, mid, lo = _split3(x)
    packed = hi + pltpu.roll(mid, N_HEADS, axis=1) + pltpu.roll(lo, 2 * N_HEADS, axis=1)
    return packed.astype(BF16)


def _rope_table_kernel(pos_ref, freq_ref, cos_ref, sin_ref):
    ang = pos_ref[0].astype(F32) * freq_ref[...]
    cos_ref[0] = jnp.cos(ang)
    sin_ref[0] = jnp.sin(ang)


def _rope_tables(positions, inv_freq):
    b, s = positions.shape
    table = pl.BlockSpec((1, ROPE_HALF, s), lambda i: (i, 0, 0))
    return pl.pallas_call(
        _rope_table_kernel,
        grid=(b,),
        in_specs=[pl.BlockSpec((1, 1, s), lambda i: (i, 0, 0)),
                  pl.BlockSpec((ROPE_HALF, 1), lambda i: (0, 0))],
        out_specs=[table, table],
        out_shape=[jax.ShapeDtypeStruct((b, ROPE_HALF, s), F32)] * 2,
        name="rope_tables",
    )(positions.reshape(b, 1, s), inv_freq.reshape(ROPE_HALF, 1))


def _proj_kernel(x_ref, cos_ref, sin_ref, g_attn_ref, w_rows_ref, w_cols_ref, bff_ref, gq_ref,
                 gkv_ref, wuq_ref, wuk_ref, wuvt_ref, tri_ref,
                 fqt_ref, fk_ref, fvt_ref, faux_ref, mqnt_ref, mqrt_ref, mkn_ref, mkr_ref, mvt_ref,
                 h_sc, carry_sc):
    rows = x_ref.shape[1]

    @pl.when(pl.program_id(1) == 0)
    def _():
        carry_sc[...] = jnp.zeros_like(carry_sc)

    h_sc[...] = _rms(x_ref[0], g_attn_ref[...]).astype(BF16)
    h = h_sc[...]

    groups = LANES // ROPE_HALF
    row = lax.broadcasted_iota(jnp.int32, (LANES, rows), 0)
    cos_t = jnp.tile(cos_ref[0], (groups, 1))
    sin_t = jnp.tile(sin_ref[0], (groups, 1))
    sin_s = jnp.where(row < LANES // 2, -sin_t, sin_t)

    def rope_t(xt):
        return xt * cos_t + pltpu.roll(xt, LANES // 2, axis=0) * sin_s

    ft = lax.dot_general(w_rows_ref[...], h, NT_DIMS, preferred_element_type=F32)
    fqt_ref[0] = (ft[0:GROUP_WIDTH] * FOX_QSCALE).astype(BF16)
    fvt_ref[0] = ft[GROUP_WIDTH:2 * GROUP_WIDTH].astype(BF16)
    mkr_ref[0] = rope_t(ft[2 * GROUP_WIDTH:2 * GROUP_WIDTH + LANES]).T.astype(BF16)

    tk = jnp.dot(h, w_cols_ref[...], preferred_element_type=F32)
    o_cq, o_ckv, o_ff = GROUP_WIDTH, GROUP_WIDTH + Q_RANK, GROUP_WIDTH + Q_RANK + KV_RANK
    fk_ref[0] = tk[:, 0:o_cq].astype(BF16)

    cq = _rms(tk[:, o_cq:o_ckv], gq_ref[...]).astype(BF16)
    qt = lax.dot_general(wuq_ref[...], cq, NT_DIMS, preferred_element_type=F32)
    mqnt_ref[0] = (qt[0:GROUP_WIDTH] * MLA_QSCALE).astype(BF16)
    for g in range(2 * N_HEADS * ROPE_HALF // LANES):
        lo = GROUP_WIDTH + g * LANES
        mqrt_ref[0, g * LANES:(g + 1) * LANES, :] = (
            rope_t(qt[lo:lo + LANES]) * MLA_QSCALE).astype(BF16)

    ckv = _rms(tk[:, o_ckv:o_ff], gkv_ref[...]).astype(BF16)
    mkn_ref[0] = jnp.dot(ckv, wuk_ref[...], preferred_element_type=F32).astype(BF16)
    mvt_ref[0] = lax.dot_general(wuvt_ref[...], ckv, NT_DIMS,
                                 preferred_element_type=F32).astype(BF16)

    lane = lax.broadcasted_iota(jnp.int32, (1, LANES), 1)
    head_lane = lane < N_HEADS
    z = tk[:, o_ff:o_ff + LANES] + bff_ref[...]
    log_f = jnp.where(head_lane, jax.nn.log_sigmoid(z), 0.0)
    tri = tri_ref[...]
    carry = carry_sc[0:1, :]
    for c in range(rows // SCAN_CHUNK):
        part = jnp.dot(tri, _pack3(log_f[c * SCAN_CHUNK:(c + 1) * SCAN_CHUNK, :]),
                       preferred_element_type=F32)
        part = (part + pltpu.roll(part, LANES - N_HEADS, axis=1)
                + pltpu.roll(part, LANES - 2 * N_HEADS, axis=1))
        cum = jnp.where(head_lane, part, 0.0) + carry
        carry = cum[SCAN_CHUNK - 1:SCAN_CHUNK, :]
        faux_ref[0, c * SCAN_CHUNK:(c + 1) * SCAN_CHUNK, :] = _pack3(cum * (-LOG2E))
    carry_sc[...] = jnp.broadcast_to(carry, carry_sc.shape)


def _const_spec(shape):
    return pl.BlockSpec(shape, lambda *_: (0,) * len(shape))


def _projection(x, cos_t, sin_t, w):
    b, s, _ = x.shape
    tm = PROJ_ROWS
    row_spec = lambda width: pl.BlockSpec((1, tm, width), lambda i, j: (i, j, 0))
    col_spec = lambda height: pl.BlockSpec((1, height, tm), lambda i, j: (i, 0, j))
    consts = [w["g_attn"], w["w_rows"], w["w_cols"], w["bff"], w["gq"], w["gkv"], w["wuq"],
              w["wuk"], w["wuvt"], w["tri"]]
    row_out = lambda width: jax.ShapeDtypeStruct((b, s, width), BF16)
    col_out = lambda height: jax.ShapeDtypeStruct((b, height, s), BF16)
    return pl.pallas_call(
        _proj_kernel,
        grid=(b, s // tm),
        in_specs=[row_spec(D_MODEL), col_spec(ROPE_HALF), col_spec(ROPE_HALF)]
                 + [_const_spec(c.shape) for c in consts],
        out_specs=[col_spec(GROUP_WIDTH), row_spec(GROUP_WIDTH), col_spec(GROUP_WIDTH),
                   row_spec(LANES), col_spec(GROUP_WIDTH), col_spec(2 * LANES),
                   row_spec(GROUP_WIDTH), row_spec(LANES), col_spec(GROUP_WIDTH)],
        out_shape=[col_out(GROUP_WIDTH), row_out(GROUP_WIDTH), col_out(GROUP_WIDTH),
                   row_out(LANES), col_out(GROUP_WIDTH), col_out(2 * LANES),
                   row_out(GROUP_WIDTH), row_out(LANES), col_out(GROUP_WIDTH)],
        scratch_shapes=[pltpu.VMEM((tm, D_MODEL), BF16), pltpu.VMEM((8, LANES), F32)],
        compiler_params=pltpu.CompilerParams(
            dimension_semantics=("arbitrary", "arbitrary"),
            vmem_limit_bytes=VMEM_LIMIT_BYTES),
        name="input_projection",
    )(x, cos_t, sin_t, *consts)


def _attn_kernel(tasks_ref, qm_ref, qa_ref, km_ref, ka_ref, vt_ref, mask_ref, o_ref,
                 bias_sc, qh_sc, s0_sc, s1_sc, acc_sc, m_sc):
    s_len = km_ref.shape[1]
    tq, tk = ATTN_TQ, ATTN_TK

    @pl.when((pl.program_id(0) == 0) & (pl.program_id(1) == 0))
    def _():
        key_idx = lax.broadcasted_iota(jnp.int32, (tk, tq), 0)
        qry_idx = lax.broadcasted_iota(jnp.int32, (tk, tq), 1)
        bias_sc[...] = jnp.where(key_idx <= qry_idx, 0.0, -jnp.inf)

    def kv_tile(k0):
        return jnp.concatenate([km_ref[0, pl.ds(k0, tk), :], ka_ref[0, pl.ds(k0, tk), :]], axis=-1)

    ones_rows = jnp.ones((ONES_ROWS, tk), BF16)

    qcat = jnp.concatenate([qm_ref[0], qa_ref[0]], axis=0)
    for a in range(2):
        qh_sc[a] = qcat * jnp.tile(mask_ref[0, a], (1, s_len // LANES))
    acc_sc[...] = jnp.zeros_like(acc_sc)
    m_sc[...] = jnp.full(m_sc.shape, -jnp.inf, F32)

    def produce(dst, t, a, diagonal):
        q0 = pl.multiple_of(tasks_ref[0, t] * tq, tq)
        k0 = pl.multiple_of(tasks_ref[1, t] * tk, tk)
        sc = jnp.dot(kv_tile(k0), qh_sc[a, :, pl.ds(q0, tq)], preferred_element_type=F32)
        if diagonal:
            sc = sc + bias_sc[...]
        dst[a] = sc
        return jnp.max(sc, axis=0, keepdims=True)

    def consume(src, t, a, maxima):
        i = tasks_ref[0, t]
        k0 = pl.multiple_of(tasks_ref[1, t] * tk, tk)
        m_run = m_sc[i, a]
        m_new = jnp.maximum(m_run, maxima)
        alpha = jnp.exp2(m_run - m_new)
        p = jnp.exp2(src[a] - m_new).astype(BF16)
        vt = jnp.concatenate(
            [vt_ref[0, a * HEAD_DIM:(a + 1) * HEAD_DIM, pl.ds(k0, tk)], ones_rows], axis=0)
        acc_sc[i, a] = alpha * acc_sc[i, a] + jnp.dot(vt, p, preferred_element_type=F32)
        m_sc[i, a] = m_new

    bufs = (s0_sc, s1_sc)

    def step(t, parity, diagonal, mx):
        mx_next = tuple(produce(bufs[1 - parity], t + 1, a, diagonal) for a in range(2))
        for a in range(2):
            consume(bufs[parity], t, a, mx[a])
        return mx_next

    def steps(first, count, diagonal):
        def body(it, mx):
            t = first + it * count
            for u in range(count):
                mx = step(t + u, (first + u) % 2, diagonal, mx)
            return mx
        return body

    n_diag = s_len // tq
    n_tasks = n_diag * (n_diag + 1) // 2
    n_off = n_tasks - n_diag
    assert n_diag % 2 == 0 and n_off % UNROLL_STEPS == 0 and UNROLL_STEPS % 2 == 0
    mx = tuple(produce(s0_sc, 0, a, True) for a in range(2))
    for t in range(n_diag - 1):
        mx = step(t, t % 2, True, mx)
    mx = lax.fori_loop(0, n_off // UNROLL_STEPS, steps(n_diag - 1, UNROLL_STEPS, False), mx)
    for a in range(2):
        consume(bufs[(n_tasks - 1) % 2], n_tasks - 1, a, mx[a])

    def finalize(i, _):
        q0 = pl.multiple_of(i * tq, tq)
        out_t = jnp.concatenate(
            [acc_sc[i, a, 0:HEAD_DIM, :] / acc_sc[i, a, HEAD_DIM:HEAD_DIM + 1, :]
             for a in range(2)], axis=0)
        o_ref[0, pl.ds(q0, tq), :] = out_t.T
        return 0

    lax.fori_loop(0, n_diag, finalize, 0)


def _attention(qmt, qat, km, ka, vt, mask, qa_batched, qa_per_pairs):
    b, s, _ = km.shape
    pairs = N_HEADS // 2
    tok = lambda f: pl.BlockSpec((1, s, LANES), f)
    feat = lambda f: pl.BlockSpec((1, LANES, s), f)
    qa_map = (lambda i, p: (i, p // qa_per_pairs, 0)) if qa_batched else (lambda i, p: (0, 0, 0))
    n_q = s // ATTN_TQ
    task_list = [(i, i) for i in range(n_q)] + [(i, j) for i in range(n_q) for j in range(i)]
    tasks = jnp.asarray(np.array(task_list, np.int32).T)
    return pl.pallas_call(
        _attn_kernel,
        grid=(b, pairs),
        in_specs=[pl.BlockSpec(memory_space=pltpu.SMEM),
                  feat(lambda i, p: (i, p, 0)), feat(qa_map), tok(lambda i, p: (i, 0, p)),
                  tok(lambda i, p: (i, 0, 0)), feat(lambda i, p: (i, p, 0)),
                  pl.BlockSpec((1, 2, 2 * LANES, LANES), lambda i, p: (p, 0, 0, 0))],
        out_specs=tok(lambda i, p: (i, 0, p)),
        out_shape=jax.ShapeDtypeStruct((b, s, GROUP_WIDTH), F32),
        scratch_shapes=[pltpu.VMEM((ATTN_TK, ATTN_TQ), F32),
                        pltpu.VMEM((2, 2 * LANES, s), BF16),
                        pltpu.VMEM((2, ATTN_TK, ATTN_TQ), F32),
                        pltpu.VMEM((2, ATTN_TK, ATTN_TQ), F32),
                        pltpu.VMEM((n_q, 2, HEAD_DIM + ONES_ROWS, ATTN_TQ), F32),
                        pltpu.VMEM((n_q, 2, 1, ATTN_TQ), F32)],
        compiler_params=pltpu.CompilerParams(
            dimension_semantics=("arbitrary", "arbitrary"),
            vmem_limit_bytes=VMEM_LIMIT_BYTES),
        name="causal_attention",
    )(tasks, qmt, qat, km, ka, vt, mask)


def _mlp_kernel(x_ref, fox_ref, mla_ref, gfox_ref, gmla_ref, wo_ref, gmlp_ref, wup_ref,
                wdown_ref, gfin_ref, o_ref, *, final):
    mixed = jnp.concatenate([_rms(fox_ref[...], gfox_ref[...]).astype(BF16),
                             _rms(mla_ref[...], gmla_ref[...]).astype(BF16)], axis=-1)
    x1 = x_ref[...] + jnp.dot(mixed, wo_ref[...], preferred_element_type=F32)
    h = _rms(x1, gmlp_ref[...]).astype(BF16)
    y = x1
    for c in range(D_FF // FF_CHUNK):
        u = jnp.dot(h, wup_ref[:, c * FF_CHUNK:(c + 1) * FF_CHUNK], preferred_element_type=F32)
        act = jnp.square(jnp.maximum(u, 0.0)).astype(BF16)
        y = y + jnp.dot(act, wdown_ref[c * FF_CHUNK:(c + 1) * FF_CHUNK, :],
                        preferred_element_type=F32)
    o_ref[...] = _rms(y, gfin_ref[...]) if final else y


def _out_mlp(x2d, fox, mla, w, final):
    n = x2d.shape[0]
    tm = MLP_ROWS
    row = lambda width: pl.BlockSpec((tm, width), lambda i: (i, 0))
    resident = lambda a: pl.BlockSpec(a.shape, lambda i: (0,) * a.ndim,
                                      pipeline_mode=pl.Buffered(1))
    consts = [w["gfox"], w["gmla"], w["wo"], w["gmlp"], w["wup"], w["wdown"], w["gfin"]]
    return pl.pallas_call(
        functools.partial(_mlp_kernel, final=final),
        grid=(n // tm,),
        in_specs=[row(D_MODEL), row(GROUP_WIDTH), row(GROUP_WIDTH)] + [resident(c) for c in consts],
        out_specs=row(D_MODEL),
        out_shape=jax.ShapeDtypeStruct((n, D_MODEL), F32),
        compiler_params=pltpu.CompilerParams(
            dimension_semantics=("arbitrary",),
            vmem_limit_bytes=VMEM_LIMIT_BYTES),
        name="out_proj_mlp",
    )(x2d, fox, mla, *consts)


def _rope_split_cols(wr):
    k = wr.shape[0]
    heads = wr.shape[1]
    per_group = LANES // ROPE_DIM
    first = wr[:, :, :ROPE_HALF].reshape(k, heads // per_group, per_group * ROPE_HALF)
    second = wr[:, :, ROPE_HALF:].reshape(k, heads // per_group, per_group * ROPE_HALF)
    return jnp.concatenate([first, second], axis=-1).reshape(k, heads * ROPE_DIM)


def _prep_weights(attn_norm_g, w_in, b_forget, q_norm_g, w_uq, kv_norm_g, w_ukv, fox_out_g,
                  mla_out_g, w_o, mlp_norm_g, w_up, w_down, final_norm_g):
    row = lambda v: v.reshape(1, -1).astype(F32)
    pad_lanes = lambda a: jnp.pad(a, ((0, 0), (0, LANES - a.shape[1])))
    wkr = w_in[:, OFF_KR:IN_COLS].reshape(D_MODEL, 1, ROPE_DIM)
    wkr = _rope_split_cols(jnp.broadcast_to(wkr, (D_MODEL, LANES // ROPE_DIM, ROPE_DIM)))
    uq = w_uq.reshape(Q_RANK, N_HEADS, MLA_QK_DIM)
    ukv = w_ukv.reshape(KV_RANK, N_HEADS, 2 * HEAD_DIM)
    tri = np.tril(np.ones((SCAN_CHUNK, SCAN_CHUNK), np.float32))
    w_rows = jnp.concatenate([w_in[:, OFF_FQ:OFF_FK], w_in[:, OFF_FV:OFF_FF], wkr], axis=1)
    w_cols = jnp.concatenate([w_in[:, OFF_FK:OFF_FV], w_in[:, OFF_CQ:OFF_CKV],
                              w_in[:, OFF_CKV:OFF_KR], pad_lanes(w_in[:, OFF_FF:OFF_CQ])], axis=1)
    wuq = jnp.concatenate([uq[:, :, :HEAD_DIM].reshape(Q_RANK, GROUP_WIDTH),
                           _rope_split_cols(uq[:, :, HEAD_DIM:])], axis=1)
    return {
        "g_attn": row(attn_norm_g),
        "w_rows": w_rows.T.astype(BF16),
        "w_cols": w_cols.astype(BF16),
        "bff": pad_lanes(row(b_forget)),
        "gq": row(q_norm_g),
        "gkv": row(kv_norm_g),
        "wuq": wuq.T.astype(BF16),
        "wuk": ukv[:, :, :HEAD_DIM].reshape(KV_RANK, GROUP_WIDTH).astype(BF16),
        "wuvt": ukv[:, :, HEAD_DIM:].reshape(KV_RANK, GROUP_WIDTH).T.astype(BF16),
        "tri": jnp.asarray(tri, BF16),
        "gfox": row(fox_out_g),
        "gmla": row(mla_out_g),
        "wo": w_o.astype(BF16),
        "gmlp": row(mlp_norm_g),
        "wup": w_up.astype(BF16),
        "wdown": w_down.astype(BF16),
        "gfin": row(final_norm_g),
    }


def _head_masks():
    pairs = N_HEADS // 2
    fox = np.zeros((pairs, 2, 2 * LANES), np.float32)
    mla = np.zeros((pairs, 2, 2 * LANES), np.float32)
    per_group = LANES // ROPE_DIM
    for p in range(pairs):
        for a in range(2):
            h = 2 * p + a
            fox[p, a, a * HEAD_DIM:(a + 1) * HEAD_DIM] = 1.0
            mla[p, a, a * HEAD_DIM:(a + 1) * HEAD_DIM] = 1.0
            for piece in range(3):
                fox[p, a, LANES + piece * N_HEADS + h] = 1.0
            hh = h % per_group
            for half in range(2):
                lo = LANES + half * (LANES // 2) + hh * ROPE_HALF
                mla[p, a, lo:lo + ROPE_HALF] = 1.0
    rep = lambda m: jnp.asarray(np.repeat(m[..., None], LANES, axis=-1), BF16)
    return rep(fox), rep(mla)


def kernel(x, positions, attn_norm_g, w_in, b_forget, q_norm_g, w_uq, kv_norm_g, w_ukv,
           fox_out_g, mla_out_g, w_o, mlp_norm_g, w_up, w_down, final_norm_g):
    b, s, d = x.shape
    depth = w_in.shape[0]
    inv_freq = ROPE_THETA ** (-jnp.arange(0, ROPE_DIM, 2, dtype=F32) / ROPE_DIM)
    cos_t, sin_t = _rope_tables(positions, inv_freq)
    fox_mask, mla_mask = _head_masks()
    ones_aux = jnp.ones((1, LANES, s), BF16)

    for l in range(depth):
        w = _prep_weights(attn_norm_g[l], w_in[l], b_forget[l], q_norm_g[l], w_uq[l],
                          kv_norm_g[l], w_ukv[l], fox_out_g[l], mla_out_g[l], w_o[l],
                          mlp_norm_g[l], w_up[l], w_down[l], final_norm_g)
        fqt, fk, fvt, faux, mqnt, mqrt, mkn, mkr, mvt = _projection(x, cos_t, sin_t, w)
        fox = _attention(fqt, ones_aux, fk, faux, fvt, fox_mask, False, 1)
        mla = _attention(mqnt, mqrt, mkn, mkr, mvt, mla_mask, True, 2)
        y = _out_mlp(x.reshape(b * s, d), fox.reshape(b * s, GROUP_WIDTH),
                     mla.reshape(b * s, GROUP_WIDTH), w, l == depth - 1)
        x = y.reshape(b, s, d)
    return x
```

```python
import functools
import math

import numpy as np
import jax
import jax.numpy as jnp
from jax import lax
from jax.experimental import pallas as pl
from jax.experimental.pallas import tpu as pltpu

D_MODEL = 1024
HEAD_DIM = 64
N_HEADS = 8
GROUP_WIDTH = N_HEADS * HEAD_DIM
ROPE_DIM = 32
ROPE_HALF = ROPE_DIM // 2
MLA_QK_DIM = HEAD_DIM + ROPE_DIM
Q_RANK = 384
KV_RANK = 256
D_FF = 4096
ROPE_THETA = 10000.0
EPS = 1e-6

OFF_FQ = 0
OFF_FK = OFF_FQ + GROUP_WIDTH
OFF_FV = OFF_FK + GROUP_WIDTH
OFF_FF = OFF_FV + GROUP_WIDTH
OFF_CQ = OFF_FF + N_HEADS
OFF_CKV = OFF_CQ + Q_RANK
OFF_KR = OFF_CKV + KV_RANK
IN_COLS = OFF_KR + ROPE_DIM

LANES = 128
VMEM_LIMIT_BYTES = 56 * 1024 * 1024

LOG2E = math.log2(math.e)
FOX_QSCALE = LOG2E / math.sqrt(HEAD_DIM)
MLA_QSCALE = LOG2E / math.sqrt(MLA_QK_DIM)

PROJ_ROWS = 1024
ATTN_TQ = 512
ATTN_TK = 512
MLP_ROWS = 512
FF_CHUNK = 1024
SCAN_CHUNK = 128
ONES_ROWS = 16
UNROLL_STEPS = 14

F32 = jnp.float32
BF16 = jnp.bfloat16
NT_DIMS = (((1,), (1,)), ((), ()))


def _rms(x, g):
    return x * lax.rsqrt(jnp.mean(x * x, axis=-1, keepdims=True) + EPS) * g


def _split3(x):
    hi = x.astype(BF16).astype(F32)
    mid = (x - hi).astype(BF16).astype(F32)
    lo = (x - hi - mid).astype(BF16).astype(F32)
    return hi, mid, lo


def _pack3(x):
    hi, mid, lo = _split3(x)
    packed = hi + pltpu.roll(mid, N_HEADS, axis=1) + pltpu.roll(lo, 2 * N_HEADS, axis=1)
    return packed.astype(BF16)


def _rope_table_kernel(pos_ref, freq_ref, cos_ref, sin_ref):
    ang = pos_ref[0].astype(F32) * freq_ref[...]
    cos_ref[0] = jnp.cos(ang)
    sin_ref[0] = jnp.sin(ang)


def _rope_tables(positions, inv_freq):
    b, s = positions.shape
    table = pl.BlockSpec((1, ROPE_HALF, s), lambda i: (i, 0, 0))
    return pl.pallas_call(
        _rope_table_kernel,
        grid=(b,),
        in_specs=[pl.BlockSpec((1, 1, s), lambda i: (i, 0, 0)),
                  pl.BlockSpec((ROPE_HALF, 1), lambda i: (0, 0))],
        out_specs=[table, table],
        out_shape=[jax.ShapeDtypeStruct((b, ROPE_HALF, s), F32)] * 2,
        name="rope_tables",
    )(positions.reshape(b, 1, s), inv_freq.reshape(ROPE_HALF, 1))


def _proj_kernel(x_ref, cos_ref, sin_ref, g_attn_ref, w_rows_ref, w_cols_ref, bff_ref, gq_ref,
                 gkv_ref, wuq_ref, wuk_ref, wuvt_ref, tri_ref,
                 fqt_ref, fk_ref, fvt_ref, faux_ref, mqnt_ref, mqrt_ref, mkn_ref, mkr_ref, mvt_ref,
                 h_sc, carry_sc):
    rows = x_ref.shape[1]

    @pl.when(pl.program_id(1) == 0)
    def _():
        carry_sc[...] = jnp.zeros_like(carry_sc)

    h_sc[...] = _rms(x_ref[0], g_attn_ref[...]).astype(BF16)
    h = h_sc[...]

    groups = LANES // ROPE_HALF
    row = lax.broadcasted_iota(jnp.int32, (LANES, rows), 0)
    cos_t = jnp.tile(cos_ref[0], (groups, 1))
    sin_t = jnp.tile(sin_ref[0], (groups, 1))
    sin_s = jnp.where(row < LANES // 2, -sin_t, sin_t)

    def rope_t(xt):
        return xt * cos_t + pltpu.roll(xt, LANES // 2, axis=0) * sin_s

    ft = lax.dot_general(w_rows_ref[...], h, NT_DIMS, preferred_element_type=F32)
    fqt_ref[0] = (ft[0:GROUP_WIDTH] * FOX_QSCALE).astype(BF16)
    fvt_ref[0] = ft[GROUP_WIDTH:2 * GROUP_WIDTH].astype(BF16)
    mkr_ref[0] = rope_t(ft[2 * GROUP_WIDTH:2 * GROUP_WIDTH + LANES]).T.astype(BF16)

    tk = jnp.dot(h, w_cols_ref[...], preferred_element_type=F32)
    o_cq, o_ckv, o_ff = GROUP_WIDTH, GROUP_WIDTH + Q_RANK, GROUP_WIDTH + Q_RANK + KV_RANK
    fk_ref[0] = tk[:, 0:o_cq].astype(BF16)

    cq = _rms(tk[:, o_cq:o_ckv], gq_ref[...]).astype(BF16)
    qt = lax.dot_general(wuq_ref[...], cq, NT_DIMS, preferred_element_type=F32)
    mqnt_ref[0] = (qt[0:GROUP_WIDTH] * MLA_QSCALE).astype(BF16)
    for g in range(2 * N_HEADS * ROPE_HALF // LANES):
        lo = GROUP_WIDTH + g * LANES
        mqrt_ref[0, g * LANES:(g + 1) * LANES, :] = (
            rope_t(qt[lo:lo + LANES]) * MLA_QSCALE).astype(BF16)

    ckv = _rms(tk[:, o_ckv:o_ff], gkv_ref[...]).astype(BF16)
    mkn_ref[0] = jnp.dot(ckv, wuk_ref[...], preferred_element_type=F32).astype(BF16)
    mvt_ref[0] = lax.dot_general(wuvt_ref[...], ckv, NT_DIMS,
                                 preferred_element_type=F32).astype(BF16)

    lane = lax.broadcasted_iota(jnp.int32, (1, LANES), 1)
    head_lane = lane < N_HEADS
    z = tk[:, o_ff:o_ff + LANES] + bff_ref[...]
    log_f = jnp.where(head_lane, jax.nn.log_sigmoid(z), 0.0)
    tri = tri_ref[...]
    carry = carry_sc[0:1, :]
    for c in range(rows // SCAN_CHUNK):
        part = jnp.dot(tri, _pack3(log_f[c * SCAN_CHUNK:(c + 1) * SCAN_CHUNK, :]),
                       preferred_element_type=F32)
        part = (part + pltpu.roll(part, LANES - N_HEADS, axis=1)
                + pltpu.roll(part, LANES - 2 * N_HEADS, axis=1))
        cum = jnp.where(head_lane, part, 0.0) + carry
        carry = cum[SCAN_CHUNK - 1:SCAN_CHUNK, :]
        faux_ref[0, c * SCAN_CHUNK:(c + 1) * SCAN_CHUNK, :] = _pack3(cum * (-LOG2E))
    carry_sc[...] = jnp.broadcast_to(carry, carry_sc.shape)


def _const_spec(shape):
    return pl.BlockSpec(shape, lambda *_: (0,) * len(shape))


def _projection(x, cos_t, sin_t, w):
    b, s, _ = x.shape
    tm = PROJ_ROWS
    row_spec = lambda width: pl.BlockSpec((1, tm, width), lambda i, j: (i, j, 0))
    col_spec = lambda height: pl.BlockSpec((1, height, tm), lambda i, j: (i, 0, j))
    consts = [w["g_attn"], w["w_rows"], w["w_cols"], w["bff"], w["gq"], w["gkv"], w["wuq"],
              w["wuk"], w["wuvt"], w["tri"]]
    row_out = lambda width: jax.ShapeDtypeStruct((b, s, width), BF16)
    col_out = lambda height: jax.ShapeDtypeStruct((b, height, s), BF16)
    return pl.pallas_call(
        _proj_kernel,
        grid=(b, s // tm),
        in_specs=[row_spec(D_MODEL), col_spec(ROPE_HALF), col_spec(ROPE_HALF)]
                 + [_const_spec(c.shape) for c in consts],
        out_specs=[col_spec(GROUP_WIDTH), row_spec(GROUP_WIDTH), col_spec(GROUP_WIDTH),
                   row_spec(LANES), col_spec(GROUP_WIDTH), col_spec(2 * LANES),
                   row_spec(GROUP_WIDTH), row_spec(LANES), col_spec(GROUP_WIDTH)],
        out_shape=[col_out(GROUP_WIDTH), row_out(GROUP_WIDTH), col_out(GROUP_WIDTH),
                   row_out(LANES), col_out(GROUP_WIDTH), col_out(2 * LANES),
                   row_out(GROUP_WIDTH), row_out(LANES), col_out(GROUP_WIDTH)],
        scratch_shapes=[pltpu.VMEM((tm, D_MODEL), BF16), pltpu.VMEM((8, LANES), F32)],
        compiler_params=pltpu.CompilerParams(
            dimension_semantics=("arbitrary", "arbitrary"),
            vmem_limit_bytes=VMEM_LIMIT_BYTES),
        name="input_projection",
    )(x, cos_t, sin_t, *consts)


def _attn_kernel(tasks_ref, qm_ref, qa_ref, km_ref, ka_ref, vt_ref, mask_ref, o_ref,
                 bias_sc, qh_sc, s0_sc, s1_sc, acc_sc, m_sc):
    s_len = km_ref.shape[1]
    tq, tk = ATTN_TQ, ATTN_TK

    @pl.when((pl.program_id(0) == 0) & (pl.program_id(1) == 0))
    def _():
        key_idx = lax.broadcasted_iota(jnp.int32, (tk, tq), 0)
        qry_idx = lax.broadcasted_iota(jnp.int32, (tk, tq), 1)
        bias_sc[...] = jnp.where(key_idx <= qry_idx, 0.0, -jnp.inf)

    def kv_tile(k0):
        return jnp.concatenate([km_ref[0, pl.ds(k0, tk), :], ka_ref[0, pl.ds(k0, tk), :]], axis=-1)

    ones_rows = jnp.ones((ONES_ROWS, tk), BF16)

    qcat = jnp.concatenate([qm_ref[0], qa_ref[0]], axis=0)
    for a in range(2):
        qh_sc[a] = qcat * jnp.tile(mask_ref[0, a], (1, s_len // LANES))
    last_tile = s_len // tq - 1
    acc_sc[last_tile] = jnp.zeros(acc_sc.shape[1:], F32)
    m_sc[last_tile] = jnp.full(m_sc.shape[1:], -jnp.inf, F32)

    def produce(dst, t, a, diagonal):
        q0 = pl.multiple_of(tasks_ref[0, t] * tq, tq)
        k0 = pl.multiple_of(tasks_ref[1, t] * tk, tk)
        sc = jnp.dot(kv_tile(k0), qh_sc[a, :, pl.ds(q0, tq)], preferred_element_type=F32)
        if diagonal:
            sc = sc + bias_sc[...]
        dst[a] = sc
        return jnp.max(sc, axis=0, keepdims=True)

    def consume(src, t, a, maxima):
        i = tasks_ref[0, t]
        k0 = pl.multiple_of(tasks_ref[1, t] * tk, tk)
        m_run = m_sc[i, a]
        m_new = jnp.maximum(m_run, maxima)
        alpha = jnp.exp2(m_run - m_new)
        p = jnp.exp2(src[a] - m_new).astype(BF16)
        vt = jnp.concatenate(
            [vt_ref[0, a * HEAD_DIM:(a + 1) * HEAD_DIM, pl.ds(k0, tk)], ones_rows], axis=0)
        acc_sc[i, a] = alpha * acc_sc[i, a] + jnp.dot(vt, p, preferred_element_type=F32)
        m_sc[i, a] = m_new

    half = tq // 2

    def produce_diag(dst, i, a):
        q0 = i * tq
        kcat = kv_tile(q0)
        top = (jnp.dot(kcat[0:half], qh_sc[a, :, q0:q0 + tq], preferred_element_type=F32)
               + bias_sc[0:half, :])
        bot = (jnp.dot(kcat[half:tk], qh_sc[a, :, q0 + half:q0 + tq], preferred_element_type=F32)
               + bias_sc[half:tk, half:tq])
        dst[a, 0:half, :] = top
        dst[a, half:tk, half:tq] = bot
        mx_top = jnp.max(top, axis=0, keepdims=True)
        mx_bot = jnp.max(bot, axis=0, keepdims=True)
        return jnp.concatenate([mx_top[:, 0:half], jnp.maximum(mx_top[:, half:tq], mx_bot)], axis=1)

    def consume_diag(src, i, a, maxima):
        q0 = i * tq
        p_top = jnp.exp2(src[a, 0:half, :] - maxima).astype(BF16)
        p_bot = jnp.exp2(src[a, half:tk, half:tq] - maxima[:, half:tq]).astype(BF16)
        vt = jnp.concatenate(
            [vt_ref[0, a * HEAD_DIM:(a + 1) * HEAD_DIM, q0:q0 + tk], ones_rows], axis=0)
        pv_left = jnp.dot(vt[:, 0:half], p_top[:, 0:half], preferred_element_type=F32)
        pv_right = jnp.dot(vt, jnp.concatenate([p_top[:, half:tq], p_bot], axis=0),
                           preferred_element_type=F32)
        acc_sc[i, a] = jnp.concatenate([pv_left, pv_right], axis=1)
        m_sc[i, a] = maxima

    bufs = (s0_sc, s1_sc)

    def step(t, parity, diagonal, mx):
        mx_next = tuple(produce(bufs[1 - parity], t + 1, a, diagonal) for a in range(2))
        for a in range(2):
            consume(bufs[parity], t, a, mx[a])
        return mx_next

    def steps(first, count, diagonal):
        def body(it, mx):
            t = first + it * count
            for u in range(count):
                mx = step(t + u, (first + u) % 2, diagonal, mx)
            return mx
        return body

    n_diag = s_len // tq
    n_tasks = n_diag * (n_diag + 1) // 2
    n_off = n_tasks - n_diag
    assert n_diag % 2 == 0 and n_off % UNROLL_STEPS == 0 and UNROLL_STEPS % 2 == 0
    assert tq == tk
    mx = tuple(produce_diag(s0_sc, 0, a) for a in range(2))
    for t in range(n_diag - 1):
        if t + 1 < n_diag - 1:
            mx_next = tuple(produce_diag(bufs[(t + 1) % 2], t + 1, a) for a in range(2))
        else:
            mx_next = tuple(produce(bufs[(t + 1) % 2], t + 1, a, True) for a in range(2))
        for a in range(2):
            consume_diag(bufs[t % 2], t, a, mx[a])
        mx = mx_next
    mx = lax.fori_loop(0, n_off // UNROLL_STEPS, steps(n_diag - 1, UNROLL_STEPS, False), mx)
    for a in range(2):
        consume(bufs[(n_tasks - 1) % 2], n_tasks - 1, a, mx[a])

    def finalize(i, _):
        q0 = pl.multiple_of(i * tq, tq)
        out_t = jnp.concatenate(
            [acc_sc[i, a, 0:HEAD_DIM, :] / acc_sc[i, a, HEAD_DIM:HEAD_DIM + 1, :]
             for a in range(2)], axis=0)
        o_ref[0, pl.ds(q0, tq), :] = out_t.T
        return 0

    lax.fori_loop(0, n_diag, finalize, 0)


def _attention(qmt, qat, km, ka, vt, mask, qa_batched, qa_per_pairs):
    b, s, _ = km.shape
    pairs = N_HEADS // 2
    tok = lambda f: pl.BlockSpec((1, s, LANES), f)
    feat = lambda f: pl.BlockSpec((1, LANES, s), f)
    qa_map = (lambda i, p: (i, p // qa_per_pairs, 0)) if qa_batched else (lambda i, p: (0, 0, 0))
    n_q = s // ATTN_TQ
    task_list = [(i, i) for i in range(n_q)] + [(i, j) for i in range(n_q) for j in range(i)]
    tasks = jnp.asarray(np.array(task_list, np.int32).T)
    return pl.pallas_call(
        _attn_kernel,
        grid=(b, pairs),
        in_specs=[pl.BlockSpec(memory_space=pltpu.SMEM),
                  feat(lambda i, p: (i, p, 0)), feat(qa_map), tok(lambda i, p: (i, 0, p)),
                  tok(lambda i, p: (i, 0, 0)), feat(lambda i, p: (i, p, 0)),
                  pl.BlockSpec((1, 2, 2 * LANES, LANES), lambda i, p: (p, 0, 0, 0))],
        out_specs=tok(lambda i, p: (i, 0, p)),
        out_shape=jax.ShapeDtypeStruct((b, s, GROUP_WIDTH), F32),
        scratch_shapes=[pltpu.VMEM((ATTN_TK, ATTN_TQ), F32),
                        pltpu.VMEM((2, 2 * LANES, s), BF16),
                        pltpu.VMEM((2, ATTN_TK, ATTN_TQ), F32),
                        pltpu.VMEM((2, ATTN_TK, ATTN_TQ), F32),
                        pltpu.VMEM((n_q, 2, HEAD_DIM + ONES_ROWS, ATTN_TQ), F32),
                        pltpu.VMEM((n_q, 2, 1, ATTN_TQ), F32)],
        compiler_params=pltpu.CompilerParams(
            dimension_semantics=("arbitrary", "arbitrary"),
            vmem_limit_bytes=VMEM_LIMIT_BYTES),
        name="causal_attention",
    )(tasks, qmt, qat, km, ka, vt, mask)


def _mlp_kernel(x_ref, fox_ref, mla_ref, gfox_ref, gmla_ref, wo_ref, gmlp_ref, wup_ref,
                wdown_ref, gfin_ref, o_ref, *, final):
    mixed = jnp.concatenate([_rms(fox_ref[...], gfox_ref[...]).astype(BF16),
                             _rms(mla_ref[...], gmla_ref[...]).astype(BF16)], axis=-1)
    x1 = x_ref[...] + jnp.dot(mixed, wo_ref[...], preferred_element_type=F32)
    h = _rms(x1, gmlp_ref[...]).astype(BF16)
    y = x1
    for c in range(D_FF // FF_CHUNK):
        u = jnp.dot(h, wup_ref[:, c * FF_CHUNK:(c + 1) * FF_CHUNK], preferred_element_type=F32)
        act = jnp.square(jnp.maximum(u, 0.0)).astype(BF16)
        y = y + jnp.dot(act, wdown_ref[c * FF_CHUNK:(c + 1) * FF_CHUNK, :],
                        preferred_element_type=F32)
    o_ref[...] = _rms(y, gfin_ref[...]) if final else y


def _out_mlp(x2d, fox, mla, w, final):
    n = x2d.shape[0]
    tm = MLP_ROWS
    row = lambda width: pl.BlockSpec((tm, width), lambda i: (i, 0))
    resident = lambda a: pl.BlockSpec(a.shape, lambda i: (0,) * a.ndim,
                                      pipeline_mode=pl.Buffered(1))
    consts = [w["gfox"], w["gmla"], w["wo"], w["gmlp"], w["wup"], w["wdown"], w["gfin"]]
    return pl.pallas_call(
        functools.partial(_mlp_kernel, final=final),
        grid=(n // tm,),
        in_specs=[row(D_MODEL), row(GROUP_WIDTH), row(GROUP_WIDTH)] + [resident(c) for c in consts],
        out_specs=row(D_MODEL),
        out_shape=jax.ShapeDtypeStruct((n, D_MODEL), F32),
        compiler_params=pltpu.CompilerParams(
            dimension_semantics=("arbitrary",),
            vmem_limit_bytes=VMEM_LIMIT_BYTES),
        name="out_proj_mlp",
    )(x2d, fox, mla, *consts)


def _rope_split_cols(wr):
    k = wr.shape[0]
    heads = wr.shape[1]
    per_group = LANES // ROPE_DIM
    first = wr[:, :, :ROPE_HALF].reshape(k, heads // per_group, per_group * ROPE_HALF)
    second = wr[:, :, ROPE_HALF:].reshape(k, heads // per_group, per_group * ROPE_HALF)
    return jnp.concatenate([first, second], axis=-1).reshape(k, heads * ROPE_DIM)


def _prep_weights(attn_norm_g, w_in, b_forget, q_norm_g, w_uq, kv_norm_g, w_ukv, fox_out_g,
                  mla_out_g, w_o, mlp_norm_g, w_up, w_down, final_norm_g):
    row = lambda v: v.reshape(1, -1).astype(F32)
    pad_lanes = lambda a: jnp.pad(a, ((0, 0), (0, LANES - a.shape[1])))
    wkr = w_in[:, OFF_KR:IN_COLS].reshape(D_MODEL, 1, ROPE_DIM)
    wkr = _rope_split_cols(jnp.broadcast_to(wkr, (D_MODEL, LANES // ROPE_DIM, ROPE_DIM)))
    uq = w_uq.reshape(Q_RANK, N_HEADS, MLA_QK_DIM)
    ukv = w_ukv.reshape(KV_RANK, N_HEADS, 2 * HEAD_DIM)
    tri = np.tril(np.ones((SCAN_CHUNK, SCAN_CHUNK), np.float32))
    w_rows = jnp.concatenate([w_in[:, OFF_FQ:OFF_FK], w_in[:, OFF_FV:OFF_FF], wkr], axis=1)
    w_cols = jnp.concatenate([w_in[:, OFF_FK:OFF_FV], w_in[:, OFF_CQ:OFF_CKV],
                              w_in[:, OFF_CKV:OFF_KR], pad_lanes(w_in[:, OFF_FF:OFF_CQ])], axis=1)
    wuq = jnp.concatenate([uq[:, :, :HEAD_DIM].reshape(Q_RANK, GROUP_WIDTH),
                           _rope_split_cols(uq[:, :, HEAD_DIM:])], axis=1)
    return {
        "g_attn": row(attn_norm_g),
        "w_rows": w_rows.T.astype(BF16),
        "w_cols": w_cols.astype(BF16),
        "bff": pad_lanes(row(b_forget)),
        "gq": row(q_norm_g),
        "gkv": row(kv_norm_g),
        "wuq": wuq.T.astype(BF16),
        "wuk": ukv[:, :, :HEAD_DIM].reshape(KV_RANK, GROUP_WIDTH).astype(BF16),
        "wuvt": ukv[:, :, HEAD_DIM:].reshape(KV_RANK, GROUP_WIDTH).T.astype(BF16),
        "tri": jnp.asarray(tri, BF16),
        "gfox": row(fox_out_g),
        "gmla": row(mla_out_g),
        "wo": w_o.astype(BF16),
        "gmlp": row(mlp_norm_g),
        "wup": w_up.astype(BF16),
        "wdown": w_down.astype(BF16),
        "gfin": row(final_norm_g),
    }


def _head_masks():
    pairs = N_HEADS // 2
    fox = np.zeros((pairs, 2, 2 * LANES), np.float32)
    mla = np.zeros((pairs, 2, 2 * LANES), np.float32)
    per_group = LANES // ROPE_DIM
    for p in range(pairs):
        for a in range(2):
            h = 2 * p + a
            fox[p, a, a * HEAD_DIM:(a + 1) * HEAD_DIM] = 1.0
            mla[p, a, a * HEAD_DIM:(a + 1) * HEAD_DIM] = 1.0
            for piece in range(3):
                fox[p, a, LANES + piece * N_HEADS + h] = 1.0
            hh = h % per_group
            for half in range(2):
                lo = LANES + half * (LANES // 2) + hh * ROPE_HALF
                mla[p, a, lo:lo + ROPE_HALF] = 1.0
    rep = lambda m: jnp.asarray(np.repeat(m[..., None], LANES, axis=-1), BF16)
    return rep(fox), rep(mla)


def kernel(x, positions, attn_norm_g, w_in, b_forget, q_norm_g, w_uq, kv_norm_g, w_ukv,
           fox_out_g, mla_out_g, w_o, mlp_norm_g, w_up, w_down, final_norm_g):
    b, s, d = x.shape
    depth = w_in.shape[0]
    inv_freq = ROPE_THETA ** (-jnp.arange(0, ROPE_DIM, 2, dtype=F32) / ROPE_DIM)
    cos_t, sin_t = _rope_tables(positions, inv_freq)
    fox_mask, mla_mask = _head_masks()
    ones_aux = jnp.ones((1, LANES, s), BF16)

    for l in range(depth):
        w = _prep_weights(attn_norm_g[l], w_in[l], b_forget[l], q_norm_g[l], w_uq[l],
                          kv_norm_g[l], w_ukv[l], fox_out_g[l], mla_out_g[l], w_o[l],
                          mlp_norm_g[l], w_up[l], w_down[l], final_norm_g)
        fqt, fk, fvt, faux, mqnt, mqrt, mkn, mkr, mvt = _projection(x, cos_t, sin_t, w)
        fox = _attention(fqt, ones_aux, fk, faux, fvt, fox_mask, False, 1)
        mla = _attention(mqnt, mqrt, mkn, mkr, mvt, mla_mask, True, 2)
        y = _out_mlp(x.reshape(b * s, d), fox.reshape(b * s, GROUP_WIDTH),
                     mla.reshape(b * s, GROUP_WIDTH), w, l == depth - 1)
        x = y.reshape(b, s, d)
    return x
```

```python
import functools
import math

import numpy as np
import jax
import jax.numpy as jnp
from jax import lax
from jax.experimental import pallas as pl
from jax.experimental.pallas import tpu as pltpu

D_MODEL = 1024
HEAD_DIM = 64
N_HEADS = 8
GROUP_WIDTH = N_HEADS * HEAD_DIM
ROPE_DIM = 32
ROPE_HALF = ROPE_DIM // 2
MLA_QK_DIM = HEAD_DIM + ROPE_DIM
Q_RANK = 384
KV_RANK = 256
D_FF = 4096
ROPE_THETA = 10000.0
EPS = 1e-6

OFF_FQ = 0
OFF_FK = OFF_FQ + GROUP_WIDTH
OFF_FV = OFF_FK + GROUP_WIDTH
OFF_FF = OFF_FV + GROUP_WIDTH
OFF_CQ = OFF_FF + N_HEADS
OFF_CKV = OFF_CQ + Q_RANK
OFF_KR = OFF_CKV + KV_RANK
IN_COLS = OFF_KR + ROPE_DIM

LANES = 128
VMEM_LIMIT_BYTES = 56 * 1024 * 1024

LOG2E = math.log2(math.e)
FOX_QSCALE = LOG2E / math.sqrt(HEAD_DIM)
MLA_QSCALE = LOG2E / math.sqrt(MLA_QK_DIM)

PROJ_ROWS = 1024
ATTN_TQ = 512
ATTN_TK = 512
MLP_ROWS = 512
FF_CHUNK = 1024
SCAN_CHUNK = 128
ONES_ROWS = 16
UNROLL_STEPS = 14
FINALIZE_UNROLL = 4

F32 = jnp.float32
BF16 = jnp.bfloat16
NT_DIMS = (((1,), (1,)), ((), ()))


def _rms(x, g):
    return x * lax.rsqrt(jnp.mean(x * x, axis=-1, keepdims=True) + EPS) * g


def _split3(x):
    hi = x.astype(BF16).astype(F32)
    mid = (x - hi).astype(BF16).astype(F32)
    lo = (x - hi - mid).astype(BF16).astype(F32)
    return hi, mid, lo


def _pack3(x):
    hi, mid, lo = _split3(x)
    packed = hi + pltpu.roll(mid, N_HEADS, axis=1) + pltpu.roll(lo, 2 * N_HEADS, axis=1)
    return packed.astype(BF16)


def _rope_table_kernel(pos_ref, freq_ref, cos_ref, sin_ref):
    ang = pos_ref[0].astype(F32) * freq_ref[...]
    cos_ref[0] = jnp.cos(ang)
    sin_ref[0] = jnp.sin(ang)


def _rope_tables(positions, inv_freq):
    b, s = positions.shape
    table = pl.BlockSpec((1, ROPE_HALF, s), lambda i: (i, 0, 0))
    return pl.pallas_call(
        _rope_table_kernel,
        grid=(b,),
        in_specs=[pl.BlockSpec((1, 1, s), lambda i: (i, 0, 0)),
                  pl.BlockSpec((ROPE_HALF, 1), lambda i: (0, 0))],
        out_specs=[table, table],
        out_shape=[jax.ShapeDtypeStruct((b, ROPE_HALF, s), F32)] * 2,
        name="rope_tables",
    )(positions.reshape(b, 1, s), inv_freq.reshape(ROPE_HALF, 1))


def _proj_kernel(x_ref, cos_ref, sin_ref, g_attn_ref, w_rows_ref, w_cols_ref, bff_ref, gq_ref,
                 gkv_ref, wuq_ref, wuk_ref, wuvt_ref, tri_ref,
                 fqt_ref, fk_ref, fvt_ref, faux_ref, mqnt_ref, mqrt_ref, mkn_ref, mkr_ref, mvt_ref,
                 h_sc, carry_sc):
    rows = x_ref.shape[1]

    @pl.when(pl.program_id(1) == 0)
    def _():
        carry_sc[...] = jnp.zeros_like(carry_sc)

    h_sc[...] = _rms(x_ref[0], g_attn_ref[...]).astype(BF16)
    h = h_sc[...]

    groups = LANES // ROPE_HALF
    row = lax.broadcasted_iota(jnp.int32, (LANES, rows), 0)
    cos_t = jnp.tile(cos_ref[0], (groups, 1))
    sin_t = jnp.tile(sin_ref[0], (groups, 1))
    sin_s = jnp.where(row < LANES // 2, -sin_t, sin_t)

    def rope_t(xt):
        return xt * cos_t + pltpu.roll(xt, LANES // 2, axis=0) * sin_s


    tk = jnp.dot(h, w_cols_ref[...], preferred_element_type=F32)
    o_cq, o_ckv, o_ff = GROUP_WIDTH, GROUP_WIDTH + Q_RANK, GROUP_WIDTH + Q_RANK + KV_RANK
    fk_ref[0] = tk[:, 0:o_cq].astype(BF16)

    ft = lax.dot_general(w_rows_ref[...], h, NT_DIMS, preferred_element_type=F32)
    fqt_ref[0] = (ft[0:GROUP_WIDTH] * FOX_QSCALE).astype(BF16)
    fvt_ref[0] = ft[GROUP_WIDTH:2 * GROUP_WIDTH].astype(BF16)
    mkr_ref[0] = rope_t(ft[2 * GROUP_WIDTH:2 * GROUP_WIDTH + LANES]).T.astype(BF16)

    lane = lax.broadcasted_iota(jnp.int32, (1, LANES), 1)
    head_lane = lane < N_HEADS
    z = tk[:, o_ff:o_ff + LANES] + bff_ref[...]
    log_f = jnp.where(head_lane, jax.nn.log_sigmoid(z), 0.0)
    tri = tri_ref[...]
    carry = carry_sc[0:1, :]
    for c in range(rows // SCAN_CHUNK):
        part = jnp.dot(tri, _pack3(log_f[c * SCAN_CHUNK:(c + 1) * SCAN_CHUNK, :]),
                       preferred_element_type=F32)
        part = (part + pltpu.roll(part, LANES - N_HEADS, axis=1)
                + pltpu.roll(part, LANES - 2 * N_HEADS, axis=1))
        cum = jnp.where(head_lane, part, 0.0) + carry
        carry = cum[SCAN_CHUNK - 1:SCAN_CHUNK, :]
        faux_ref[0, c * SCAN_CHUNK:(c + 1) * SCAN_CHUNK, :] = _pack3(cum * (-LOG2E))
    carry_sc[...] = jnp.broadcast_to(carry, carry_sc.shape)

    cq = _rms(tk[:, o_cq:o_ckv], gq_ref[...]).astype(BF16)
    qt = lax.dot_general(wuq_ref[...], cq, NT_DIMS, preferred_element_type=F32)
    mqnt_ref[0] = (qt[0:GROUP_WIDTH] * MLA_QSCALE).astype(BF16)
    for g in range(2 * N_HEADS * ROPE_HALF // LANES):
        lo = GROUP_WIDTH + g * LANES
        mqrt_ref[0, g * LANES:(g + 1) * LANES, :] = (
            rope_t(qt[lo:lo + LANES]) * MLA_QSCALE).astype(BF16)

    ckv = _rms(tk[:, o_ckv:o_ff], gkv_ref[...]).astype(BF16)
    mkn_ref[0] = jnp.dot(ckv, wuk_ref[...], preferred_element_type=F32).astype(BF16)
    mvt_ref[0] = lax.dot_general(wuvt_ref[...], ckv, NT_DIMS,
                                 preferred_element_type=F32).astype(BF16)


def _const_spec(shape):
    return pl.BlockSpec(shape, lambda *_: (0,) * len(shape))


def _projection(x, cos_t, sin_t, w):
    b, s, _ = x.shape
    tm = PROJ_ROWS
    row_spec = lambda width: pl.BlockSpec((1, tm, width), lambda i, j: (i, j, 0))
    col_spec = lambda height: pl.BlockSpec((1, height, tm), lambda i, j: (i, 0, j))
    consts = [w["g_attn"], w["w_rows"], w["w_cols"], w["bff"], w["gq"], w["gkv"], w["wuq"],
              w["wuk"], w["wuvt"], w["tri"]]
    row_out = lambda width: jax.ShapeDtypeStruct((b, s, width), BF16)
    col_out = lambda height: jax.ShapeDtypeStruct((b, height, s), BF16)
    return pl.pallas_call(
        _proj_kernel,
        grid=(b, s // tm),
        in_specs=[row_spec(D_MODEL), col_spec(ROPE_HALF), col_spec(ROPE_HALF)]
                 + [_const_spec(c.shape) for c in consts],
        out_specs=[col_spec(GROUP_WIDTH), row_spec(GROUP_WIDTH), col_spec(GROUP_WIDTH),
                   row_spec(LANES), col_spec(GROUP_WIDTH), col_spec(2 * LANES),
                   row_spec(GROUP_WIDTH), row_spec(LANES), col_spec(GROUP_WIDTH)],
        out_shape=[col_out(GROUP_WIDTH), row_out(GROUP_WIDTH), col_out(GROUP_WIDTH),
                   row_out(LANES), col_out(GROUP_WIDTH), col_out(2 * LANES),
                   row_out(GROUP_WIDTH), row_out(LANES), col_out(GROUP_WIDTH)],
        scratch_shapes=[pltpu.VMEM((tm, D_MODEL), BF16), pltpu.VMEM((8, LANES), F32)],
        compiler_params=pltpu.CompilerParams(
            dimension_semantics=("arbitrary", "arbitrary"),
            vmem_limit_bytes=VMEM_LIMIT_BYTES),
        name="input_projection",
    )(x, cos_t, sin_t, *consts)


def _attn_kernel(tasks_ref, qm_ref, qa_ref, km_ref, ka_ref, vt_ref, mask_ref, o_ref,
                 bias_sc, qh_sc, s0_sc, s1_sc, acc_sc, m_sc):
    s_len = km_ref.shape[1]
    tq, tk = ATTN_TQ, ATTN_TK

    @pl.when((pl.program_id(0) == 0) & (pl.program_id(1) == 0))
    def _():
        key_idx = lax.broadcasted_iota(jnp.int32, (tk, tq), 0)
        qry_idx = lax.broadcasted_iota(jnp.int32, (tk, tq), 1)
        bias_sc[...] = jnp.where(key_idx <= qry_idx, 0.0, -jnp.inf)

    def kv_tile(k0):
        return jnp.concatenate([km_ref[0, pl.ds(k0, tk), :], ka_ref[0, pl.ds(k0, tk), :]], axis=-1)

    ones_rows = jnp.ones((ONES_ROWS, tk), BF16)

    qcat = jnp.concatenate([qm_ref[0], qa_ref[0]], axis=0)
    for a in range(2):
        qh_sc[a] = qcat * jnp.tile(mask_ref[0, a], (1, s_len // LANES))
    last_tile = s_len // tq - 1
    acc_sc[last_tile] = jnp.zeros(acc_sc.shape[1:], F32)
    m_sc[last_tile] = jnp.full(m_sc.shape[1:], -jnp.inf, F32)

    def produce(dst, t, a, diagonal):
        q0 = pl.multiple_of(tasks_ref[0, t] * tq, tq)
        k0 = pl.multiple_of(tasks_ref[1, t] * tk, tk)
        sc = jnp.dot(kv_tile(k0), qh_sc[a, :, pl.ds(q0, tq)], preferred_element_type=F32)
        if diagonal:
            sc = sc + bias_sc[...]
        dst[a] = sc
        return jnp.max(sc, axis=0, keepdims=True)

    def consume(src, t, a, maxima):
        i = tasks_ref[0, t]
        k0 = pl.multiple_of(tasks_ref[1, t] * tk, tk)
        m_run = m_sc[i, a]
        m_new = jnp.maximum(m_run, maxima)
        alpha = jnp.exp2(m_run - m_new)
        p = jnp.exp2(src[a] - m_new).astype(BF16)
        vt = jnp.concatenate(
            [vt_ref[0, a * HEAD_DIM:(a + 1) * HEAD_DIM, pl.ds(k0, tk)], ones_rows], axis=0)
        acc_sc[i, a] = alpha * acc_sc[i, a] + jnp.dot(vt, p, preferred_element_type=F32)
        m_sc[i, a] = m_new

    half = tq // 2

    def produce_diag(dst, i, a):
        q0 = i * tq
        kcat = kv_tile(q0)
        top = (jnp.dot(kcat[0:half], qh_sc[a, :, q0:q0 + tq], preferred_element_type=F32)
               + bias_sc[0:half, :])
        bot = (jnp.dot(kcat[half:tk], qh_sc[a, :, q0 + half:q0 + tq], preferred_element_type=F32)
               + bias_sc[half:tk, half:tq])
        dst[a, 0:half, :] = top
        dst[a, half:tk, half:tq] = bot
        mx_top = jnp.max(top, axis=0, keepdims=True)
        mx_bot = jnp.max(bot, axis=0, keepdims=True)
        return jnp.concatenate([mx_top[:, 0:half], jnp.maximum(mx_top[:, half:tq], mx_bot)], axis=1)

    def consume_diag(src, i, a, maxima):
        q0 = i * tq
        p_top = jnp.exp2(src[a, 0:half, :] - maxima).astype(BF16)
        p_bot = jnp.exp2(src[a, half:tk, half:tq] - maxima[:, half:tq]).astype(BF16)
        vt = jnp.concatenate(
            [vt_ref[0, a * HEAD_DIM:(a + 1) * HEAD_DIM, q0:q0 + tk], ones_rows], axis=0)
        pv_left = jnp.dot(vt[:, 0:half], p_top[:, 0:half], preferred_element_type=F32)
        pv_right = jnp.dot(vt, jnp.concatenate([p_top[:, half:tq], p_bot], axis=0),
                           preferred_element_type=F32)
        acc_sc[i, a] = jnp.concatenate([pv_left, pv_right], axis=1)
        m_sc[i, a] = maxima

    bufs = (s0_sc, s1_sc)

    def step(t, parity, diagonal, mx):
        mx_next = tuple(produce(bufs[1 - parity], t + 1, a, diagonal) for a in range(2))
        for a in range(2):
            consume(bufs[parity], t, a, mx[a])
        return mx_next

    def steps(first, count, diagonal):
        def body(it, mx):
            t = first + it * count
            for u in range(count):
                mx = step(t + u, (first + u) % 2, diagonal, mx)
            return mx
        return body

    n_diag = s_len // tq
    n_tasks = n_diag * (n_diag + 1) // 2
    n_off = n_tasks - n_diag
    assert n_diag % 2 == 0 and n_off % UNROLL_STEPS == 0 and UNROLL_STEPS % 2 == 0
    assert tq == tk
    mx = tuple(produce_diag(s0_sc, 0, a) for a in range(2))
    for t in range(n_diag - 1):
        if t + 1 < n_diag - 1:
            mx_next = tuple(produce_diag(bufs[(t + 1) % 2], t + 1, a) for a in range(2))
        else:
            mx_next = tuple(produce(bufs[(t + 1) % 2], t + 1, a, True) for a in range(2))
        for a in range(2):
            consume_diag(bufs[t % 2], t, a, mx[a])
        mx = mx_next
    mx = lax.fori_loop(0, n_off // UNROLL_STEPS, steps(n_diag - 1, UNROLL_STEPS, False), mx)
    for a in range(2):
        consume(bufs[(n_tasks - 1) % 2], n_tasks - 1, a, mx[a])

    def finalize(it, _):
        for u in range(FINALIZE_UNROLL):
            i = it * FINALIZE_UNROLL + u
            q0 = pl.multiple_of(i * tq, tq)
            out_t = jnp.concatenate(
                [acc_sc[i, a, 0:HEAD_DIM, :] / acc_sc[i, a, HEAD_DIM:HEAD_DIM + 1, :]
                 for a in range(2)], axis=0)
            o_ref[0, pl.ds(q0, tq), :] = out_t.T
        return 0

    assert n_diag % FINALIZE_UNROLL == 0
    lax.fori_loop(0, n_diag // FINALIZE_UNROLL, finalize, 0)


def _attention(qmt, qat, km, ka, vt, mask, qa_batched, qa_per_pairs):
    b, s, _ = km.shape
    pairs = N_HEADS // 2
    tok = lambda f: pl.BlockSpec((1, s, LANES), f)
    feat = lambda f: pl.BlockSpec((1, LANES, s), f)
    qa_map = (lambda i, p: (i, p // qa_per_pairs, 0)) if qa_batched else (lambda i, p: (0, 0, 0))
    n_q = s // ATTN_TQ
    task_list = [(i, i) for i in range(n_q)] + [(i, j) for i in range(n_q) for j in range(i)]
    tasks = jnp.asarray(np.array(task_list, np.int32).T)
    return pl.pallas_call(
        _attn_kernel,
        grid=(b, pairs),
        in_specs=[pl.BlockSpec(memory_space=pltpu.SMEM),
                  feat(lambda i, p: (i, p, 0)), feat(qa_map), tok(lambda i, p: (i, 0, p)),
                  tok(lambda i, p: (i, 0, 0)), feat(lambda i, p: (i, p, 0)),
                  pl.BlockSpec((1, 2, 2 * LANES, LANES), lambda i, p: (p, 0, 0, 0))],
        out_specs=tok(lambda i, p: (i, 0, p)),
        out_shape=jax.ShapeDtypeStruct((b, s, GROUP_WIDTH), F32),
        scratch_shapes=[pltpu.VMEM((ATTN_TK, ATTN_TQ), F32),
                        pltpu.VMEM((2, 2 * LANES, s), BF16),
                        pltpu.VMEM((2, ATTN_TK, ATTN_TQ), F32),
                        pltpu.VMEM((2, ATTN_TK, ATTN_TQ), F32),
                        pltpu.VMEM((n_q, 2, HEAD_DIM + ONES_ROWS, ATTN_TQ), F32),
                        pltpu.VMEM((n_q, 2, 1, ATTN_TQ), F32)],
        compiler_params=pltpu.CompilerParams(
            dimension_semantics=("arbitrary", "arbitrary"),
            vmem_limit_bytes=VMEM_LIMIT_BYTES),
        name="causal_attention",
    )(tasks, qmt, qat, km, ka, vt, mask)


def _mlp_kernel(x_ref, fox_ref, mla_ref, gfox_ref, gmla_ref, wo_ref, gmlp_ref, wup_ref,
                wdown_ref, gfin_ref, o_ref, *, final):
    mixed = jnp.concatenate([_rms(fox_ref[...], gfox_ref[...]).astype(BF16),
                             _rms(mla_ref[...], gmla_ref[...]).astype(BF16)], axis=-1)
    x1 = x_ref[...] + jnp.dot(mixed, wo_ref[...], preferred_element_type=F32)
    h = _rms(x1, gmlp_ref[...]).astype(BF16)
    y = x1
    for c in range(D_FF // FF_CHUNK):
        u = jnp.dot(h, wup_ref[:, c * FF_CHUNK:(c + 1) * FF_CHUNK], preferred_element_type=F32)
        act = jnp.square(jnp.maximum(u, 0.0)).astype(BF16)
        y = y + jnp.dot(act, wdown_ref[c * FF_CHUNK:(c + 1) * FF_CHUNK, :],
                        preferred_element_type=F32)
    o_ref[...] = _rms(y, gfin_ref[...]) if final else y


def _out_mlp(x2d, fox, mla, w, final):
    n = x2d.shape[0]
    tm = MLP_ROWS
    row = lambda width: pl.BlockSpec((tm, width), lambda i: (i, 0))
    resident = lambda a: pl.BlockSpec(a.shape, lambda i: (0,) * a.ndim,
                                      pipeline_mode=pl.Buffered(1))
    consts = [w["gfox"], w["gmla"], w["wo"], w["gmlp"], w["wup"], w["wdown"], w["gfin"]]
    return pl.pallas_call(
        functools.partial(_mlp_kernel, final=final),
        grid=(n // tm,),
        in_specs=[row(D_MODEL), row(GROUP_WIDTH), row(GROUP_WIDTH)] + [resident(c) for c in consts],
        out_specs=row(D_MODEL),
        out_shape=jax.ShapeDtypeStruct((n, D_MODEL), F32),
        compiler_params=pltpu.CompilerParams(
            dimension_semantics=("arbitrary",),
            vmem_limit_bytes=VMEM_LIMIT_BYTES),
        name="out_proj_mlp",
    )(x2d, fox, mla, *consts)


def _rope_split_cols(wr):
    k = wr.shape[0]
    heads = wr.shape[1]
    per_group = LANES // ROPE_DIM
    first = wr[:, :, :ROPE_HALF].reshape(k, heads // per_group, per_group * ROPE_HALF)
    second = wr[:, :, ROPE_HALF:].reshape(k, heads // per_group, per_group * ROPE_HALF)
    return jnp.concatenate([first, second], axis=-1).reshape(k, heads * ROPE_DIM)


def _prep_weights(attn_norm_g, w_in, b_forget, q_norm_g, w_uq, kv_norm_g, w_ukv, fox_out_g,
                  mla_out_g, w_o, mlp_norm_g, w_up, w_down, final_norm_g):
    row = lambda v: v.reshape(1, -1).astype(F32)
    pad_lanes = lambda a: jnp.pad(a, ((0, 0), (0, LANES - a.shape[1])))
    wkr = w_in[:, OFF_KR:IN_COLS].reshape(D_MODEL, 1, ROPE_DIM)
    wkr = _rope_split_cols(jnp.broadcast_to(wkr, (D_MODEL, LANES // ROPE_DIM, ROPE_DIM)))
    uq = w_uq.reshape(Q_RANK, N_HEADS, MLA_QK_DIM)
    ukv = w_ukv.reshape(KV_RANK, N_HEADS, 2 * HEAD_DIM)
    tri = np.tril(np.ones((SCAN_CHUNK, SCAN_CHUNK), np.float32))
    w_rows = jnp.concatenate([w_in[:, OFF_FQ:OFF_FK], w_in[:, OFF_FV:OFF_FF], wkr], axis=1)
    w_cols = jnp.concatenate([w_in[:, OFF_FK:OFF_FV], w_in[:, OFF_CQ:OFF_CKV],
                              w_in[:, OFF_CKV:OFF_KR], pad_lanes(w_in[:, OFF_FF:OFF_CQ])], axis=1)
    wuq = jnp.concatenate([uq[:, :, :HEAD_DIM].reshape(Q_RANK, GROUP_WIDTH),
                           _rope_split_cols(uq[:, :, HEAD_DIM:])], axis=1)
    return {
        "g_attn": row(attn_norm_g),
        "w_rows": w_rows.T.astype(BF16),
        "w_cols": w_cols.astype(BF16),
        "bff": pad_lanes(row(b_forget)),
        "gq": row(q_norm_g),
        "gkv": row(kv_norm_g),
        "wuq": wuq.T.astype(BF16),
        "wuk": ukv[:, :, :HEAD_DIM].reshape(KV_RANK, GROUP_WIDTH).astype(BF16),
        "wuvt": ukv[:, :, HEAD_DIM:].reshape(KV_RANK, GROUP_WIDTH).T.astype(BF16),
        "tri": jnp.asarray(tri, BF16),
        "gfox": row(fox_out_g),
        "gmla": row(mla_out_g),
        "wo": w_o.astype(BF16),
        "gmlp": row(mlp_norm_g),
        "wup": w_up.astype(BF16),
        "wdown": w_down.astype(BF16),
        "gfin": row(final_norm_g),
    }


def _head_masks():
    pairs = N_HEADS // 2
    fox = np.zeros((pairs, 2, 2 * LANES), np.float32)
    mla = np.zeros((pairs, 2, 2 * LANES), np.float32)
    per_group = LANES // ROPE_DIM
    for p in range(pairs):
        for a in range(2):
            h = 2 * p + a
            fox[p, a, a * HEAD_DIM:(a + 1) * HEAD_DIM] = 1.0
            mla[p, a, a * HEAD_DIM:(a + 1) * HEAD_DIM] = 1.0
            for piece in range(3):
                fox[p, a, LANES + piece * N_HEADS + h] = 1.0
            hh = h % per_group
            for half in range(2):
                lo = LANES + half * (LANES // 2) + hh * ROPE_HALF
                mla[p, a, lo:lo + ROPE_HALF] = 1.0
    rep = lambda m: jnp.asarray(np.repeat(m[..., None], LANES, axis=-1), BF16)
    return rep(fox), rep(mla)


def kernel(x, positions, attn_norm_g, w_in, b_forget, q_norm_g, w_uq, kv_norm_g, w_ukv,
           fox_out_g, mla_out_g, w_o, mlp_norm_g, w_up, w_down, final_norm_g):
    b, s, d = x.shape
    depth = w_in.shape[0]
    inv_freq = ROPE_THETA ** (-jnp.arange(0, ROPE_DIM, 2, dtype=F32) / ROPE_DIM)
    cos_t, sin_t = _rope_tables(positions, inv_freq)
    fox_mask, mla_mask = _head_masks()
    ones_aux = jnp.ones((1, LANES, s), BF16)

    for l in range(depth):
        w = _prep_weights(attn_norm_g[l], w_in[l], b_forget[l], q_norm_g[l], w_uq[l],
                          kv_norm_g[l], w_ukv[l], fox_out_g[l], mla_out_g[l], w_o[l],
                          mlp_norm_g[l], w_up[l], w_down[l], final_norm_g)
        fqt, fk, fvt, faux, mqnt, mqrt, mkn, mkr, mvt = _projection(x, cos_t, sin_t, w)
        fox = _attention(fqt, ones_aux, fk, faux, fvt, fox_mask, False, 1)
        mla = _attention(mqnt, mqrt, mkn, mkr, mvt, mla_mask, True, 2)
        y = _out_mlp(x.reshape(b * s, d), fox.reshape(b * s, GROUP_WIDTH),
                     mla.reshape(b * s, GROUP_WIDTH), w, l == depth - 1)
        x = y.reshape(b, s, d)
    return x
```

```python
import functools
import math

import numpy as np
import jax
import jax.numpy as jnp
from jax import lax
from jax.experimental import pallas as pl
from jax.experimental.pallas import tpu as pltpu

D_MODEL = 1024
HEAD_DIM = 64
N_HEADS = 8
GROUP_WIDTH = N_HEADS * HEAD_DIM
ROPE_DIM = 32
ROPE_HALF = ROPE_DIM // 2
MLA_QK_DIM = HEAD_DIM + ROPE_DIM
Q_RANK = 384
KV_RANK = 256
D_FF = 4096
ROPE_THETA = 10000.0
EPS = 1e-6

OFF_FQ = 0
OFF_FK = OFF_FQ + GROUP_WIDTH
OFF_FV = OFF_FK + GROUP_WIDTH
OFF_FF = OFF_FV + GROUP_WIDTH
OFF_CQ = OFF_FF + N_HEADS
OFF_CKV = OFF_CQ + Q_RANK
OFF_KR = OFF_CKV + KV_RANK
IN_COLS = OFF_KR + ROPE_DIM

LANES = 128
VMEM_LIMIT_BYTES = 56 * 1024 * 1024

LOG2E = math.log2(math.e)
FOX_QSCALE = LOG2E / math.sqrt(HEAD_DIM)
MLA_QSCALE = LOG2E / math.sqrt(MLA_QK_DIM)

PROJ_ROWS = 1024
ATTN_TQ = 512
ATTN_TK = 512
MLP_ROWS = 512
FF_CHUNK = 1024
SCAN_CHUNK = 128
ONES_ROWS = 16
UNROLL_STEPS = 14
FINALIZE_UNROLL = 4

F32 = jnp.float32
BF16 = jnp.bfloat16
NT_DIMS = (((1,), (1,)), ((), ()))


def _rms(x, g):
    return x * lax.rsqrt(jnp.mean(x * x, axis=-1, keepdims=True) + EPS) * g


def _split3(x):
    hi = x.astype(BF16).astype(F32)
    mid = (x - hi).astype(BF16).astype(F32)
    lo = (x - hi - mid).astype(BF16).astype(F32)
    return hi, mid, lo


def _pack3(x):
    hi, mid, lo = _split3(x)
    packed = hi + pltpu.roll(mid, N_HEADS, axis=1) + pltpu.roll(lo, 2 * N_HEADS, axis=1)
    return packed.astype(BF16)


def _rope_table_kernel(pos_ref, freq_ref, cos_ref, sin_ref):
    ang = pos_ref[0].astype(F32) * freq_ref[...]
    cos_ref[0] = jnp.cos(ang)
    sin_ref[0] = jnp.sin(ang)


def _rope_tables(positions, inv_freq):
    b, s = positions.shape
    table = pl.BlockSpec((1, ROPE_HALF, s), lambda i: (i, 0, 0))
    return pl.pallas_call(
        _rope_table_kernel,
        grid=(b,),
        in_specs=[pl.BlockSpec((1, 1, s), lambda i: (i, 0, 0)),
                  pl.BlockSpec((ROPE_HALF, 1), lambda i: (0, 0))],
        out_specs=[table, table],
        out_shape=[jax.ShapeDtypeStruct((b, ROPE_HALF, s), F32)] * 2,
        name="rope_tables",
    )(positions.reshape(b, 1, s), inv_freq.reshape(ROPE_HALF, 1))


def _proj_kernel(x_ref, cos_ref, sin_ref, g_attn_ref, w_rows_ref, w_cols_ref, bff_ref, gq_ref,
                 gkv_ref, wuq_ref, wuk_ref, wuvt_ref, tri_ref,
                 fqt_ref, fk_ref, fvt_ref, faux_ref, mqnt_ref, mqrt_ref, mkn_ref, mkr_ref, mvt_ref,
                 h_sc, carry_sc):
    rows = x_ref.shape[1]

    @pl.when(pl.program_id(1) == 0)
    def _():
        carry_sc[...] = jnp.zeros_like(carry_sc)

    h_sc[...] = _rms(x_ref[0], g_attn_ref[...]).astype(BF16)
    h = h_sc[...]

    groups = LANES // ROPE_HALF
    row = lax.broadcasted_iota(jnp.int32, (LANES, rows), 0)
    cos_t = jnp.tile(cos_ref[0], (groups, 1))
    sin_t = jnp.tile(sin_ref[0], (groups, 1))
    sin_s = jnp.where(row < LANES // 2, -sin_t, sin_t)

    def rope_t(xt):
        return xt * cos_t + pltpu.roll(xt, LANES // 2, axis=0) * sin_s


    tk = jnp.dot(h, w_cols_ref[...], preferred_element_type=F32)
    o_cq, o_ckv, o_ff = GROUP_WIDTH, GROUP_WIDTH + Q_RANK, GROUP_WIDTH + Q_RANK + KV_RANK
    fk_ref[0] = tk[:, 0:o_cq].astype(BF16)

    ft = lax.dot_general(w_rows_ref[...], h, NT_DIMS, preferred_element_type=F32)
    fqt_ref[0] = (ft[0:GROUP_WIDTH] * FOX_QSCALE).astype(BF16)
    fvt_ref[0] = ft[GROUP_WIDTH:2 * GROUP_WIDTH].astype(BF16)
    mkr_ref[0] = rope_t(ft[2 * GROUP_WIDTH:2 * GROUP_WIDTH + LANES]).T.astype(BF16)

    lane = lax.broadcasted_iota(jnp.int32, (1, LANES), 1)
    head_lane = lane < N_HEADS
    z = tk[:, o_ff:o_ff + LANES] + bff_ref[...]
    log_f = jnp.where(head_lane, jax.nn.log_sigmoid(z), 0.0)
    tri = tri_ref[...]
    carry = carry_sc[0:1, :]
    for c in range(rows // SCAN_CHUNK):
        part = jnp.dot(tri, _pack3(log_f[c * SCAN_CHUNK:(c + 1) * SCAN_CHUNK, :]),
                       preferred_element_type=F32)
        part = (part + pltpu.roll(part, LANES - N_HEADS, axis=1)
                + pltpu.roll(part, LANES - 2 * N_HEADS, axis=1))
        cum = jnp.where(head_lane, part, 0.0) + carry
        carry = cum[SCAN_CHUNK - 1:SCAN_CHUNK, :]
        faux_ref[0, c * SCAN_CHUNK:(c + 1) * SCAN_CHUNK, :] = _pack3(cum * (-LOG2E))
    carry_sc[...] = jnp.broadcast_to(carry, carry_sc.shape)

    cq = _rms(tk[:, o_cq:o_ckv], gq_ref[...]).astype(BF16)
    qt = lax.dot_general(wuq_ref[...], cq, NT_DIMS, preferred_element_type=F32)
    mqnt_ref[0] = (qt[0:GROUP_WIDTH] * MLA_QSCALE).astype(BF16)
    for g in range(2 * N_HEADS * ROPE_HALF // LANES):
        lo = GROUP_WIDTH + g * LANES
        mqrt_ref[0, g * LANES:(g + 1) * LANES, :] = (
            rope_t(qt[lo:lo + LANES]) * MLA_QSCALE).astype(BF16)

    ckv = _rms(tk[:, o_ckv:o_ff], gkv_ref[...]).astype(BF16)
    mkn_ref[0] = jnp.dot(ckv, wuk_ref[...], preferred_element_type=F32).astype(BF16)
    mvt_ref[0] = lax.dot_general(wuvt_ref[...], ckv, NT_DIMS,
                                 preferred_element_type=F32).astype(BF16)


def _const_spec(shape):
    return pl.BlockSpec(shape, lambda *_: (0,) * len(shape))


def _projection(x, cos_t, sin_t, w):
    b, s, _ = x.shape
    tm = PROJ_ROWS
    row_spec = lambda width: pl.BlockSpec((1, tm, width), lambda i, j: (i, j, 0))
    col_spec = lambda height: pl.BlockSpec((1, height, tm), lambda i, j: (i, 0, j))
    consts = [w["g_attn"], w["w_rows"], w["w_cols"], w["bff"], w["gq"], w["gkv"], w["wuq"],
              w["wuk"], w["wuvt"], w["tri"]]
    row_out = lambda width: jax.ShapeDtypeStruct((b, s, width), BF16)
    col_out = lambda height: jax.ShapeDtypeStruct((b, height, s), BF16)
    return pl.pallas_call(
        _proj_kernel,
        grid=(b, s // tm),
        in_specs=[row_spec(D_MODEL), col_spec(ROPE_HALF), col_spec(ROPE_HALF)]
                 + [_const_spec(c.shape) for c in consts],
        out_specs=[col_spec(GROUP_WIDTH), row_spec(GROUP_WIDTH), col_spec(GROUP_WIDTH),
                   row_spec(LANES), col_spec(GROUP_WIDTH), col_spec(2 * LANES),
                   row_spec(GROUP_WIDTH), row_spec(LANES), col_spec(GROUP_WIDTH)],
        out_shape=[col_out(GROUP_WIDTH), row_out(GROUP_WIDTH), col_out(GROUP_WIDTH),
                   row_out(LANES), col_out(GROUP_WIDTH), col_out(2 * LANES),
                   row_out(GROUP_WIDTH), row_out(LANES), col_out(GROUP_WIDTH)],
        scratch_shapes=[pltpu.VMEM((tm, D_MODEL), BF16), pltpu.VMEM((8, LANES), F32)],
        compiler_params=pltpu.CompilerParams(
            dimension_semantics=("arbitrary", "arbitrary"),
            vmem_limit_bytes=VMEM_LIMIT_BYTES),
        name="input_projection",
    )(x, cos_t, sin_t, *consts)


def _attn_kernel(tasks_ref, qm_ref, qa_ref, km_ref, ka_ref, vt_ref, mask_ref, o_ref,
                 bias_sc, qh_sc, s0_sc, s1_sc, acc_sc, m_sc):
    s_len = km_ref.shape[1]
    tq, tk = ATTN_TQ, ATTN_TK

    @pl.when((pl.program_id(0) == 0) & (pl.program_id(1) == 0))
    def _():
        key_idx = lax.broadcasted_iota(jnp.int32, (tk, tq), 0)
        qry_idx = lax.broadcasted_iota(jnp.int32, (tk, tq), 1)
        bias_sc[...] = jnp.where(key_idx <= qry_idx, 0.0, -jnp.inf)

    def kv_tile(k0):
        return jnp.concatenate([km_ref[0, pl.ds(k0, tk), :], ka_ref[0, pl.ds(k0, tk), :]], axis=-1)

    ones_rows = jnp.ones((ONES_ROWS, tk), BF16)

    qcat = jnp.concatenate([qm_ref[0], qa_ref[0]], axis=0)
    for a in range(2):
        qh_sc[a] = qcat * jnp.tile(mask_ref[0, a], (1, s_len // LANES))
    last_tile = s_len // tq - 1
    acc_sc[last_tile] = jnp.zeros(acc_sc.shape[1:], F32)
    m_sc[last_tile] = jnp.full(m_sc.shape[1:], -jnp.inf, F32)

    def produce(dst, t, a, diagonal):
        q0 = pl.multiple_of(tasks_ref[0, t] * tq, tq)
        k0 = pl.multiple_of(tasks_ref[1, t] * tk, tk)
        sc = jnp.dot(kv_tile(k0), qh_sc[a, :, pl.ds(q0, tq)], preferred_element_type=F32)
        if diagonal:
            sc = sc + bias_sc[...]
        dst[a] = sc
        return jnp.max(sc, axis=0, keepdims=True)

    def consume(src, t, a, maxima):
        i = tasks_ref[0, t]
        k0 = pl.multiple_of(tasks_ref[1, t] * tk, tk)
        m_run = m_sc[i, a]
        m_new = jnp.maximum(m_run, maxima)
        alpha = jnp.exp2(m_run - m_new)
        p = jnp.exp2(src[a] - m_new).astype(BF16)
        vt = jnp.concatenate(
            [vt_ref[0, a * HEAD_DIM:(a + 1) * HEAD_DIM, pl.ds(k0, tk)], ones_rows], axis=0)
        acc_sc[i, a] = alpha * acc_sc[i, a] + jnp.dot(vt, p, preferred_element_type=F32)
        m_sc[i, a] = m_new

    half = tq // 2

    def produce_diag(dst, i, a):
        q0 = i * tq
        kcat = kv_tile(q0)
        top = (jnp.dot(kcat[0:half], qh_sc[a, :, q0:q0 + tq], preferred_element_type=F32)
               + bias_sc[0:half, :])
        bot = (jnp.dot(kcat[half:tk], qh_sc[a, :, q0 + half:q0 + tq], preferred_element_type=F32)
               + bias_sc[half:tk, half:tq])
        dst[a, 0:half, :] = top
        dst[a, half:tk, half:tq] = bot
        mx_top = jnp.max(top, axis=0, keepdims=True)
        mx_bot = jnp.max(bot, axis=0, keepdims=True)
        return jnp.concatenate([mx_top[:, 0:half], jnp.maximum(mx_top[:, half:tq], mx_bot)], axis=1)

    def consume_diag(src, i, a, maxima):
        q0 = i * tq
        p_top = jnp.exp2(src[a, 0:half, :] - maxima).astype(BF16)
        p_bot = jnp.exp2(src[a, half:tk, half:tq] - maxima[:, half:tq]).astype(BF16)
        vt = jnp.concatenate(
            [vt_ref[0, a * HEAD_DIM:(a + 1) * HEAD_DIM, q0:q0 + tk], ones_rows], axis=0)
        pv_left = jnp.dot(vt[:, 0:half], p_top[:, 0:half], preferred_element_type=F32)
        pv_right = jnp.dot(vt, jnp.concatenate([p_top[:, half:tq], p_bot], axis=0),
                           preferred_element_type=F32)
        acc_sc[i, a] = jnp.concatenate([pv_left, pv_right], axis=1)
        m_sc[i, a] = maxima

    bufs = (s0_sc, s1_sc)

    def step(t, parity, diagonal, mx):
        mx_next = []
        for a in range(2):
            mx_next.append(produce(bufs[1 - parity], t + 1, a, diagonal))
            consume(bufs[parity], t, a, mx[a])
        return tuple(mx_next)

    def steps(first, count, diagonal):
        def body(it, mx):
            t = first + it * count
            for u in range(count):
                mx = step(t + u, (first + u) % 2, diagonal, mx)
            return mx
        return body

    n_diag = s_len // tq
    n_tasks = n_diag * (n_diag + 1) // 2
    n_off = n_tasks - n_diag
    assert n_diag % 2 == 0 and n_off % UNROLL_STEPS == 0 and UNROLL_STEPS % 2 == 0
    assert tq == tk
    mx = tuple(produce_diag(s0_sc, 0, a) for a in range(2))
    for t in range(n_diag - 1):
        if t + 1 < n_diag - 1:
            mx_next = tuple(produce_diag(bufs[(t + 1) % 2], t + 1, a) for a in range(2))
        else:
            mx_next = tuple(produce(bufs[(t + 1) % 2], t + 1, a, True) for a in range(2))
        for a in range(2):
            consume_diag(bufs[t % 2], t, a, mx[a])
        mx = mx_next
    mx = lax.fori_loop(0, n_off // UNROLL_STEPS, steps(n_diag - 1, UNROLL_STEPS, False), mx)
    for a in range(2):
        consume(bufs[(n_tasks - 1) % 2], n_tasks - 1, a, mx[a])

    def finalize(it, _):
        for u in range(FINALIZE_UNROLL):
            i = it * FINALIZE_UNROLL + u
            q0 = pl.multiple_of(i * tq, tq)
            out_t = jnp.concatenate(
                [acc_sc[i, a, 0:HEAD_DIM, :] / acc_sc[i, a, HEAD_DIM:HEAD_DIM + 1, :]
                 for a in range(2)], axis=0)
            o_ref[0, pl.ds(q0, tq), :] = out_t.T
        return 0

    assert n_diag % FINALIZE_UNROLL == 0
    lax.fori_loop(0, n_diag // FINALIZE_UNROLL, finalize, 0)


def _attention(qmt, qat, km, ka, vt, mask, qa_batched, qa_per_pairs):
    b, s, _ = km.shape
    pairs = N_HEADS // 2
    tok = lambda f: pl.BlockSpec((1, s, LANES), f)
    feat = lambda f: pl.BlockSpec((1, LANES, s), f)
    qa_map = (lambda i, p: (i, p // qa_per_pairs, 0)) if qa_batched else (lambda i, p: (0, 0, 0))
    n_q = s // ATTN_TQ
    task_list = [(i, i) for i in range(n_q)] + [(i, j) for i in range(n_q) for j in range(i)]
    tasks = jnp.asarray(np.array(task_list, np.int32).T)
    return pl.pallas_call(
        _attn_kernel,
        grid=(b, pairs),
        in_specs=[pl.BlockSpec(memory_space=pltpu.SMEM),
                  feat(lambda i, p: (i, p, 0)), feat(qa_map), tok(lambda i, p: (i, 0, p)),
                  tok(lambda i, p: (i, 0, 0)), feat(lambda i, p: (i, p, 0)),
                  pl.BlockSpec((1, 2, 2 * LANES, LANES), lambda i, p: (p, 0, 0, 0))],
        out_specs=tok(lambda i, p: (i, 0, p)),
        out_shape=jax.ShapeDtypeStruct((b, s, GROUP_WIDTH), F32),
        scratch_shapes=[pltpu.VMEM((ATTN_TK, ATTN_TQ), F32),
                        pltpu.VMEM((2, 2 * LANES, s), BF16),
                        pltpu.VMEM((2, ATTN_TK, ATTN_TQ), F32),
                        pltpu.VMEM((2, ATTN_TK, ATTN_TQ), F32),
                        pltpu.VMEM((n_q, 2, HEAD_DIM + ONES_ROWS, ATTN_TQ), F32),
                        pltpu.VMEM((n_q, 2, 1, ATTN_TQ), F32)],
        compiler_params=pltpu.CompilerParams(
            dimension_semantics=("arbitrary", "arbitrary"),
            vmem_limit_bytes=VMEM_LIMIT_BYTES),
        name="causal_attention",
    )(tasks, qmt, qat, km, ka, vt, mask)


def _mlp_kernel(x_ref, fox_ref, mla_ref, gfox_ref, gmla_ref, wo_ref, gmlp_ref, wup_ref,
                wdown_ref, gfin_ref, o_ref, *, final):
    mixed = jnp.concatenate([_rms(fox_ref[...], gfox_ref[...]).astype(BF16),
                             _rms(mla_ref[...], gmla_ref[...]).astype(BF16)], axis=-1)
    x1 = x_ref[...] + jnp.dot(mixed, wo_ref[...], preferred_element_type=F32)
    h = _rms(x1, gmlp_ref[...]).astype(BF16)
    y = x1
    for c in range(D_FF // FF_CHUNK):
        u = jnp.dot(h, wup_ref[:, c * FF_CHUNK:(c + 1) * FF_CHUNK], preferred_element_type=F32)
        act = jnp.square(jnp.maximum(u, 0.0)).astype(BF16)
        y = y + jnp.dot(act, wdown_ref[c * FF_CHUNK:(c + 1) * FF_CHUNK, :],
                        preferred_element_type=F32)
    o_ref[...] = _rms(y, gfin_ref[...]) if final else y


def _out_mlp(x2d, fox, mla, w, final):
    n = x2d.shape[0]
    tm = MLP_ROWS
    row = lambda width: pl.BlockSpec((tm, width), lambda i: (i, 0))
    resident = lambda a: pl.BlockSpec(a.shape, lambda i: (0,) * a.ndim,
                                      pipeline_mode=pl.Buffered(1))
    consts = [w["gfox"], w["gmla"], w["wo"], w["gmlp"], w["wup"], w["wdown"], w["gfin"]]
    return pl.pallas_call(
        functools.partial(_mlp_kernel, final=final),
        grid=(n // tm,),
        in_specs=[row(D_MODEL), row(GROUP_WIDTH), row(GROUP_WIDTH)] + [resident(c) for c in consts],
        out_specs=row(D_MODEL),
        out_shape=jax.ShapeDtypeStruct((n, D_MODEL), F32),
        compiler_params=pltpu.CompilerParams(
            dimension_semantics=("arbitrary",),
            vmem_limit_bytes=VMEM_LIMIT_BYTES),
        name="out_proj_mlp",
    )(x2d, fox, mla, *consts)


def _rope_split_cols(wr):
    k = wr.shape[0]
    heads = wr.shape[1]
    per_group = LANES // ROPE_DIM
    first = wr[:, :, :ROPE_HALF].reshape(k, heads // per_group, per_group * ROPE_HALF)
    second = wr[:, :, ROPE_HALF:].reshape(k, heads // per_group, per_group * ROPE_HALF)
    return jnp.concatenate([first, second], axis=-1).reshape(k, heads * ROPE_DIM)


def _prep_weights(attn_norm_g, w_in, b_forget, q_norm_g, w_uq, kv_norm_g, w_ukv, fox_out_g,
                  mla_out_g, w_o, mlp_norm_g, w_up, w_down, final_norm_g):
    row = lambda v: v.reshape(1, -1).astype(F32)
    pad_lanes = lambda a: jnp.pad(a, ((0, 0), (0, LANES - a.shape[1])))
    wkr = w_in[:, OFF_KR:IN_COLS].reshape(D_MODEL, 1, ROPE_DIM)
    wkr = _rope_split_cols(jnp.broadcast_to(wkr, (D_MODEL, LANES // ROPE_DIM, ROPE_DIM)))
    uq = w_uq.reshape(Q_RANK, N_HEADS, MLA_QK_DIM)
    ukv = w_ukv.reshape(KV_RANK, N_HEADS, 2 * HEAD_DIM)
    tri = np.tril(np.ones((SCAN_CHUNK, SCAN_CHUNK), np.float32))
    w_rows = jnp.concatenate([w_in[:, OFF_FQ:OFF_FK], w_in[:, OFF_FV:OFF_FF], wkr], axis=1)
    w_cols = jnp.concatenate([w_in[:, OFF_FK:OFF_FV], w_in[:, OFF_CQ:OFF_CKV],
                              w_in[:, OFF_CKV:OFF_KR], pad_lanes(w_in[:, OFF_FF:OFF_CQ])], axis=1)
    wuq = jnp.concatenate([uq[:, :, :HEAD_DIM].reshape(Q_RANK, GROUP_WIDTH),
                           _rope_split_cols(uq[:, :, HEAD_DIM:])], axis=1)
    return {
        "g_attn": row(attn_norm_g),
        "w_rows": w_rows.T.astype(BF16),
        "w_cols": w_cols.astype(BF16),
        "bff": pad_lanes(row(b_forget)),
        "gq": row(q_norm_g),
        "gkv": row(kv_norm_g),
        "wuq": wuq.T.astype(BF16),
        "wuk": ukv[:, :, :HEAD_DIM].reshape(KV_RANK, GROUP_WIDTH).astype(BF16),
        "wuvt": ukv[:, :, HEAD_DIM:].reshape(KV_RANK, GROUP_WIDTH).T.astype(BF16),
        "tri": jnp.asarray(tri, BF16),
        "gfox": row(fox_out_g),
        "gmla": row(mla_out_g),
        "wo": w_o.astype(BF16),
        "gmlp": row(mlp_norm_g),
        "wup": w_up.astype(BF16),
        "wdown": w_down.astype(BF16),
        "gfin": row(final_norm_g),
    }


def _head_masks():
    pairs = N_HEADS // 2
    fox = np.zeros((pairs, 2, 2 * LANES), np.float32)
    mla = np.zeros((pairs, 2, 2 * LANES), np.float32)
    per_group = LANES // ROPE_DIM
    for p in range(pairs):
        for a in range(2):
            h = 2 * p + a
            fox[p, a, a * HEAD_DIM:(a + 1) * HEAD_DIM] = 1.0
            mla[p, a, a * HEAD_DIM:(a + 1) * HEAD_DIM] = 1.0
            for piece in range(3):
                fox[p, a, LANES + piece * N_HEADS + h] = 1.0
            hh = h % per_group
            for half in range(2):
                lo = LANES + half * (LANES // 2) + hh * ROPE_HALF
                mla[p, a, lo:lo + ROPE_HALF] = 1.0
    rep = lambda m: jnp.asarray(np.repeat(m[..., None], LANES, axis=-1), BF16)
    return rep(fox), rep(mla)


def kernel(x, positions, attn_norm_g, w_in, b_forget, q_norm_g, w_uq, kv_norm_g, w_ukv,
           fox_out_g, mla_out_g, w_o, mlp_norm_g, w_up, w_down, final_norm_g):
    b, s, d = x.shape
    depth = w_in.shape[0]
    inv_freq = ROPE_THETA ** (-jnp.arange(0, ROPE_DIM, 2, dtype=F32) / ROPE_DIM)
    cos_t, sin_t = _rope_tables(positions, inv_freq)
    fox_mask, mla_mask = _head_masks()
    ones_aux = jnp.ones((1, LANES, s), BF16)

    for l in range(depth):
        w = _prep_weights(attn_norm_g[l], w_in[l], b_forget[l], q_norm_g[l], w_uq[l],
                          kv_norm_g[l], w_ukv[l], fox_out_g[l], mla_out_g[l], w_o[l],
                          mlp_norm_g[l], w_up[l], w_down[l], final_norm_g)
        fqt, fk, fvt, faux, mqnt, mqrt, mkn, mkr, mvt = _projection(x, cos_t, sin_t, w)
        fox = _attention(fqt, ones_aux, fk, faux, fvt, fox_mask, False, 1)
        mla = _attention(mqnt, mqrt, mkn, mkr, mvt, mla_mask, True, 2)
        y = _out_mlp(x.reshape(b * s, d), fox.reshape(b * s, GROUP_WIDTH),
                     mla.reshape(b * s, GROUP_WIDTH), w, l == depth - 1)
        x = y.reshape(b, s, d)
    return x
```

```python
import functools
import math

import numpy as np
import jax
import jax.numpy as jnp
from jax import lax
from jax.experimental import pallas as pl
from jax.experimental.pallas import tpu as pltpu

D_MODEL = 1024
HEAD_DIM = 64
N_HEADS = 8
GROUP_WIDTH = N_HEADS * HEAD_DIM
ROPE_DIM = 32
ROPE_HALF = ROPE_DIM // 2
MLA_QK_DIM = HEAD_DIM + ROPE_DIM
Q_RANK = 384
KV_RANK = 256
D_FF = 4096
ROPE_THETA = 10000.0
EPS = 1e-6

OFF_FQ = 0
OFF_FK = OFF_FQ + GROUP_WIDTH
OFF_FV = OFF_FK + GROUP_WIDTH
OFF_FF = OFF_FV + GROUP_WIDTH
OFF_CQ = OFF_FF + N_HEADS
OFF_CKV = OFF_CQ + Q_RANK
OFF_KR = OFF_CKV + KV_RANK
IN_COLS = OFF_KR + ROPE_DIM

LANES = 128
VMEM_LIMIT_BYTES = 56 * 1024 * 1024

LOG2E = math.log2(math.e)
FOX_QSCALE = LOG2E / math.sqrt(HEAD_DIM)
MLA_QSCALE = LOG2E / math.sqrt(MLA_QK_DIM)

PROJ_ROWS = 1024
ATTN_TQ = 512
ATTN_TK = 512
MLP_ROWS = 512
FF_CHUNK = 1024
SCAN_CHUNK = 128
ONES_ROWS = 16
UNROLL_STEPS = 14
FINALIZE_UNROLL = 4

F32 = jnp.float32
BF16 = jnp.bfloat16
NT_DIMS = (((1,), (1,)), ((), ()))


def _rms(x, g):
    return x * lax.rsqrt(jnp.mean(x * x, axis=-1, keepdims=True) + EPS) * g


def _split3(x):
    hi = x.astype(BF16).astype(F32)
    mid = (x - hi).astype(BF16).astype(F32)
    lo = (x - hi - mid).astype(BF16).astype(F32)
    return hi, mid, lo


def _pack3(x):
    hi, mid, lo = _split3(x)
    packed = hi + pltpu.roll(mid, N_HEADS, axis=1) + pltpu.roll(lo, 2 * N_HEADS, axis=1)
    return packed.astype(BF16)


def _rope_table_kernel(pos_ref, freq_ref, cos_ref, sin_ref):
    ang = pos_ref[0].astype(F32) * freq_ref[...]
    cos_ref[0] = jnp.cos(ang)
    sin_ref[0] = jnp.sin(ang)


def _rope_tables(positions, inv_freq):
    b, s = positions.shape
    table = pl.BlockSpec((1, ROPE_HALF, s), lambda i: (i, 0, 0))
    return pl.pallas_call(
        _rope_table_kernel,
        grid=(b,),
        in_specs=[pl.BlockSpec((1, 1, s), lambda i: (i, 0, 0)),
                  pl.BlockSpec((ROPE_HALF, 1), lambda i: (0, 0))],
        out_specs=[table, table],
        out_shape=[jax.ShapeDtypeStruct((b, ROPE_HALF, s), F32)] * 2,
        name="rope_tables",
    )(positions.reshape(b, 1, s), inv_freq.reshape(ROPE_HALF, 1))


def _proj_kernel(x_ref, cos_ref, sin_ref, g_attn_ref, w_rows_ref, w_cols_ref, bff_ref, gq_ref,
                 gkv_ref, wuq_ref, wuk_ref, wuvt_ref, tri_ref,
                 fqt_ref, fk_ref, fvt_ref, faux_ref, mqnt_ref, mqrt_ref, mkn_ref, mkr_ref, mvt_ref,
                 h_sc, carry_sc):
    rows = x_ref.shape[1]

    @pl.when(pl.program_id(1) == 0)
    def _():
        carry_sc[...] = jnp.zeros_like(carry_sc)

    h_sc[...] = _rms(x_ref[0], g_attn_ref[...]).astype(BF16)
    h = h_sc[...]

    groups = LANES // ROPE_HALF
    row = lax.broadcasted_iota(jnp.int32, (LANES, rows), 0)
    cos_t = jnp.tile(cos_ref[0], (groups, 1))
    sin_t = jnp.tile(sin_ref[0], (groups, 1))
    sin_s = jnp.where(row < LANES // 2, -sin_t, sin_t)

    def rope_t(xt):
        return xt * cos_t + pltpu.roll(xt, LANES // 2, axis=0) * sin_s


    tk = jnp.dot(h, w_cols_ref[...], preferred_element_type=F32)
    o_cq, o_ckv, o_ff = GROUP_WIDTH, GROUP_WIDTH + Q_RANK, GROUP_WIDTH + Q_RANK + KV_RANK
    fk_ref[0] = tk[:, 0:o_cq].astype(BF16)

    ft = lax.dot_general(w_rows_ref[...], h, NT_DIMS, preferred_element_type=F32)
    fqt_ref[0] = (ft[0:GROUP_WIDTH] * FOX_QSCALE).astype(BF16)
    fvt_ref[0] = ft[GROUP_WIDTH:2 * GROUP_WIDTH].astype(BF16)
    mkr_ref[0] = rope_t(ft[2 * GROUP_WIDTH:2 * GROUP_WIDTH + LANES]).T.astype(BF16)

    lane = lax.broadcasted_iota(jnp.int32, (1, LANES), 1)
    head_lane = lane < N_HEADS
    z = tk[:, o_ff:o_ff + LANES] + bff_ref[...]
    log_f = jnp.where(head_lane, jax.nn.log_sigmoid(z), 0.0)
    tri = tri_ref[...]
    carry = carry_sc[0:1, :]
    for c in range(rows // SCAN_CHUNK):
        part = jnp.dot(tri, _pack3(log_f[c * SCAN_CHUNK:(c + 1) * SCAN_CHUNK, :]),
                       preferred_element_type=F32)
        part = (part + pltpu.roll(part, LANES - N_HEADS, axis=1)
                + pltpu.roll(part, LANES - 2 * N_HEADS, axis=1))
        cum = jnp.where(head_lane, part, 0.0) + carry
        carry = cum[SCAN_CHUNK - 1:SCAN_CHUNK, :]
        faux_ref[0, c * SCAN_CHUNK:(c + 1) * SCAN_CHUNK, :] = _pack3(cum * (-LOG2E))
    carry_sc[...] = jnp.broadcast_to(carry, carry_sc.shape)

    cq = _rms(tk[:, o_cq:o_ckv], gq_ref[...]).astype(BF16)
    qt = lax.dot_general(wuq_ref[...], cq, NT_DIMS, preferred_element_type=F32)
    mqnt_ref[0] = (qt[0:GROUP_WIDTH] * MLA_QSCALE).astype(BF16)
    for g in range(2 * N_HEADS * ROPE_HALF // LANES):
        lo = GROUP_WIDTH + g * LANES
        mqrt_ref[0, g * LANES:(g + 1) * LANES, :] = (
            rope_t(qt[lo:lo + LANES]) * MLA_QSCALE).astype(BF16)

    ckv = _rms(tk[:, o_ckv:o_ff], gkv_ref[...]).astype(BF16)
    mkn_ref[0] = jnp.dot(ckv, wuk_ref[...], preferred_element_type=F32).astype(BF16)
    mvt_ref[0] = lax.dot_general(wuvt_ref[...], ckv, NT_DIMS,
                                 preferred_element_type=F32).astype(BF16)


def _const_spec(shape):
    return pl.BlockSpec(shape, lambda *_: (0,) * len(shape))


def _projection(x, cos_t, sin_t, w):
    b, s, _ = x.shape
    tm = PROJ_ROWS
    row_spec = lambda width: pl.BlockSpec((1, tm, width), lambda i, j: (i, j, 0))
    col_spec = lambda height: pl.BlockSpec((1, height, tm), lambda i, j: (i, 0, j))
    consts = [w["g_attn"], w["w_rows"], w["w_cols"], w["bff"], w["gq"], w["gkv"], w["wuq"],
              w["wuk"], w["wuvt"], w["tri"]]
    row_out = lambda width: jax.ShapeDtypeStruct((b, s, width), BF16)
    col_out = lambda height: jax.ShapeDtypeStruct((b, height, s), BF16)
    return pl.pallas_call(
        _proj_kernel,
        grid=(b, s // tm),
        in_specs=[row_spec(D_MODEL), col_spec(ROPE_HALF), col_spec(ROPE_HALF)]
                 + [_const_spec(c.shape) for c in consts],
        out_specs=[col_spec(GROUP_WIDTH), row_spec(GROUP_WIDTH), col_spec(GROUP_WIDTH),
                   row_spec(LANES), col_spec(GROUP_WIDTH), col_spec(2 * LANES),
                   row_spec(GROUP_WIDTH), row_spec(LANES), col_spec(GROUP_WIDTH)],
        out_shape=[col_out(GROUP_WIDTH), row_out(GROUP_WIDTH), col_out(GROUP_WIDTH),
                   row_out(LANES), col_out(GROUP_WIDTH), col_out(2 * LANES),
                   row_out(GROUP_WIDTH), row_out(LANES), col_out(GROUP_WIDTH)],
        scratch_shapes=[pltpu.VMEM((tm, D_MODEL), BF16), pltpu.VMEM((8, LANES), F32)],
        compiler_params=pltpu.CompilerParams(
            dimension_semantics=("arbitrary", "arbitrary"),
            vmem_limit_bytes=VMEM_LIMIT_BYTES),
        name="input_projection",
    )(x, cos_t, sin_t, *consts)


def _attn_kernel(tasks_ref, qm_ref, qa_ref, km_ref, ka_ref, vt_ref, mask_ref, o_ref,
                 bias_sc, qh_sc, s0_sc, s1_sc, acc_sc, m_sc):
    s_len = km_ref.shape[1]
    tq, tk = ATTN_TQ, ATTN_TK

    @pl.when((pl.program_id(0) == 0) & (pl.program_id(1) == 0))
    def _():
        key_idx = lax.broadcasted_iota(jnp.int32, (tk, tq), 0)
        qry_idx = lax.broadcasted_iota(jnp.int32, (tk, tq), 1)
        bias_sc[...] = jnp.where(key_idx <= qry_idx, 0.0, -jnp.inf)

    def kv_tile(k0):
        return jnp.concatenate([km_ref[0, pl.ds(k0, tk), :], ka_ref[0, pl.ds(k0, tk), :]], axis=-1)

    ones_rows = jnp.ones((ONES_ROWS, tk), BF16)

    qcat = jnp.concatenate([qm_ref[0], qa_ref[0]], axis=0)
    for a in range(2):
        qh_sc[a] = qcat * jnp.tile(mask_ref[0, a], (1, s_len // LANES))
    last_tile = s_len // tq - 1
    acc_sc[last_tile] = jnp.zeros(acc_sc.shape[1:], F32)
    m_sc[last_tile] = jnp.full(m_sc.shape[1:], -jnp.inf, F32)

    def produce(dst, t, a, diagonal):
        q0 = pl.multiple_of(tasks_ref[0, t] * tq, tq)
        k0 = pl.multiple_of(tasks_ref[1, t] * tk, tk)
        sc = jnp.dot(kv_tile(k0), qh_sc[a, :, pl.ds(q0, tq)], preferred_element_type=F32)
        if diagonal:
            sc = sc + bias_sc[...]
        dst[a] = sc
        return jnp.max(sc, axis=0, keepdims=True)

    def consume(src, t, a, maxima):
        i = tasks_ref[0, t]
        k0 = pl.multiple_of(tasks_ref[1, t] * tk, tk)
        m_run = m_sc[i, a]
        m_new = jnp.maximum(m_run, maxima)
        alpha = jnp.exp2(m_run - m_new)
        p = jnp.exp2(src[a] - m_new).astype(BF16)
        vt = jnp.concatenate([vt_ref[0, :, pl.ds(k0, tk)], ones_rows], axis=0)
        pv = jnp.dot(vt, p, preferred_element_type=F32)
        pv = jnp.concatenate([pv[a * HEAD_DIM:(a + 1) * HEAD_DIM], pv[2 * HEAD_DIM:]], axis=0)
        acc_sc[i, a] = alpha * acc_sc[i, a] + pv
        m_sc[i, a] = m_new

    half = tq // 2

    def produce_diag(dst, i, a):
        q0 = i * tq
        kcat = kv_tile(q0)
        top = (jnp.dot(kcat[0:half], qh_sc[a, :, q0:q0 + tq], preferred_element_type=F32)
               + bias_sc[0:half, :])
        bot = (jnp.dot(kcat[half:tk], qh_sc[a, :, q0 + half:q0 + tq], preferred_element_type=F32)
               + bias_sc[half:tk, half:tq])
        dst[a, 0:half, :] = top
        dst[a, half:tk, half:tq] = bot
        mx_top = jnp.max(top, axis=0, keepdims=True)
        mx_bot = jnp.max(bot, axis=0, keepdims=True)
        return jnp.concatenate([mx_top[:, 0:half], jnp.maximum(mx_top[:, half:tq], mx_bot)], axis=1)

    def consume_diag(src, i, a, maxima):
        q0 = i * tq
        p_top = jnp.exp2(src[a, 0:half, :] - maxima).astype(BF16)
        p_bot = jnp.exp2(src[a, half:tk, half:tq] - maxima[:, half:tq]).astype(BF16)
        vt = jnp.concatenate(
            [vt_ref[0, a * HEAD_DIM:(a + 1) * HEAD_DIM, q0:q0 + tk], ones_rows], axis=0)
        pv_left = jnp.dot(vt[:, 0:half], p_top[:, 0:half], preferred_element_type=F32)
        pv_right = jnp.dot(vt, jnp.concatenate([p_top[:, half:tq], p_bot], axis=0),
                           preferred_element_type=F32)
        acc_sc[i, a] = jnp.concatenate([pv_left, pv_right], axis=1)
        m_sc[i, a] = maxima

    bufs = (s0_sc, s1_sc)

    def step(t, parity, diagonal, mx):
        mx_next = tuple(produce(bufs[1 - parity], t + 1, a, diagonal) for a in range(2))
        for a in range(2):
            consume(bufs[parity], t, a, mx[a])
        return mx_next

    def steps(first, count, diagonal):
        def body(it, mx):
            t = first + it * count
            for u in range(count):
                mx = step(t + u, (first + u) % 2, diagonal, mx)
            return mx
        return body

    n_diag = s_len // tq
    n_tasks = n_diag * (n_diag + 1) // 2
    n_off = n_tasks - n_diag
    assert n_diag % 2 == 0 and n_off % UNROLL_STEPS == 0 and UNROLL_STEPS % 2 == 0
    assert tq == tk
    mx = tuple(produce_diag(s0_sc, 0, a) for a in range(2))
    for t in range(n_diag - 1):
        if t + 1 < n_diag - 1:
            mx_next = tuple(produce_diag(bufs[(t + 1) % 2], t + 1, a) for a in range(2))
        else:
            mx_next = tuple(produce(bufs[(t + 1) % 2], t + 1, a, True) for a in range(2))
        for a in range(2):
            consume_diag(bufs[t % 2], t, a, mx[a])
        mx = mx_next
    mx = lax.fori_loop(0, n_off // UNROLL_STEPS, steps(n_diag - 1, UNROLL_STEPS, False), mx)
    for a in range(2):
        consume(bufs[(n_tasks - 1) % 2], n_tasks - 1, a, mx[a])

    def finalize(it, _):
        for u in range(FINALIZE_UNROLL):
            i = it * FINALIZE_UNROLL + u
            q0 = pl.multiple_of(i * tq, tq)
            out_t = jnp.concatenate(
                [acc_sc[i, a, 0:HEAD_DIM, :] / acc_sc[i, a, HEAD_DIM:HEAD_DIM + 1, :]
                 for a in range(2)], axis=0)
            o_ref[0, pl.ds(q0, tq), :] = out_t.T
        return 0

    assert n_diag % FINALIZE_UNROLL == 0
    lax.fori_loop(0, n_diag // FINALIZE_UNROLL, finalize, 0)


def _attention(qmt, qat, km, ka, vt, mask, qa_batched, qa_per_pairs):
    b, s, _ = km.shape
    pairs = N_HEADS // 2
    tok = lambda f: pl.BlockSpec((1, s, LANES), f)
    feat = lambda f: pl.BlockSpec((1, LANES, s), f)
    qa_map = (lambda i, p: (i, p // qa_per_pairs, 0)) if qa_batched else (lambda i, p: (0, 0, 0))
    n_q = s // ATTN_TQ
    task_list = [(i, i) for i in range(n_q)] + [(i, j) for i in range(n_q) for j in range(i)]
    tasks = jnp.asarray(np.array(task_list, np.int32).T)
    return pl.pallas_call(
        _attn_kernel,
        grid=(b, pairs),
        in_specs=[pl.BlockSpec(memory_space=pltpu.SMEM),
                  feat(lambda i, p: (i, p, 0)), feat(qa_map), tok(lambda i, p: (i, 0, p)),
                  tok(lambda i, p: (i, 0, 0)), feat(lambda i, p: (i, p, 0)),
                  pl.BlockSpec((1, 2, 2 * LANES, LANES), lambda i, p: (p, 0, 0, 0))],
        out_specs=tok(lambda i, p: (i, 0, p)),
        out_shape=jax.ShapeDtypeStruct((b, s, GROUP_WIDTH), F32),
        scratch_shapes=[pltpu.VMEM((ATTN_TK, ATTN_TQ), F32),
                        pltpu.VMEM((2, 2 * LANES, s), BF16),
                        pltpu.VMEM((2, ATTN_TK, ATTN_TQ), F32),
                        pltpu.VMEM((2, ATTN_TK, ATTN_TQ), F32),
                        pltpu.VMEM((n_q, 2, HEAD_DIM + ONES_ROWS, ATTN_TQ), F32),
                        pltpu.VMEM((n_q, 2, 1, ATTN_TQ), F32)],
        compiler_params=pltpu.CompilerParams(
            dimension_semantics=("arbitrary", "arbitrary"),
            vmem_limit_bytes=VMEM_LIMIT_BYTES),
        name="causal_attention",
    )(tasks, qmt, qat, km, ka, vt, mask)


def _mlp_kernel(x_ref, fox_ref, mla_ref, gfox_ref, gmla_ref, wo_ref, gmlp_ref, wup_ref,
                wdown_ref, gfin_ref, o_ref, *, final):
    mixed = jnp.concatenate([_rms(fox_ref[...], gfox_ref[...]).astype(BF16),
                             _rms(mla_ref[...], gmla_ref[...]).astype(BF16)], axis=-1)
    x1 = x_ref[...] + jnp.dot(mixed, wo_ref[...], preferred_element_type=F32)
    h = _rms(x1, gmlp_ref[...]).astype(BF16)
    y = x1
    for c in range(D_FF // FF_CHUNK):
        u = jnp.dot(h, wup_ref[:, c * FF_CHUNK:(c + 1) * FF_CHUNK], preferred_element_type=F32)
        act = jnp.square(jnp.maximum(u, 0.0)).astype(BF16)
        y = y + jnp.dot(act, wdown_ref[c * FF_CHUNK:(c + 1) * FF_CHUNK, :],
                        preferred_element_type=F32)
    o_ref[...] = _rms(y, gfin_ref[...]) if final else y


def _out_mlp(x2d, fox, mla, w, final):
    n = x2d.shape[0]
    tm = MLP_ROWS
    row = lambda width: pl.BlockSpec((tm, width), lambda i: (i, 0))
    resident = lambda a: pl.BlockSpec(a.shape, lambda i: (0,) * a.ndim,
                                      pipeline_mode=pl.Buffered(1))
    consts = [w["gfox"], w["gmla"], w["wo"], w["gmlp"], w["wup"], w["wdown"], w["gfin"]]
    return pl.pallas_call(
        functools.partial(_mlp_kernel, final=final),
        grid=(n // tm,),
        in_specs=[row(D_MODEL), row(GROUP_WIDTH), row(GROUP_WIDTH)] + [resident(c) for c in consts],
        out_specs=row(D_MODEL),
        out_shape=jax.ShapeDtypeStruct((n, D_MODEL), F32),
        compiler_params=pltpu.CompilerParams(
            dimension_semantics=("arbitrary",),
            vmem_limit_bytes=VMEM_LIMIT_BYTES),
        name="out_proj_mlp",
    )(x2d, fox, mla, *consts)


def _rope_split_cols(wr):
    k = wr.shape[0]
    heads = wr.shape[1]
    per_group = LANES // ROPE_DIM
    first = wr[:, :, :ROPE_HALF].reshape(k, heads // per_group, per_group * ROPE_HALF)
    second = wr[:, :, ROPE_HALF:].reshape(k, heads // per_group, per_group * ROPE_HALF)
    return jnp.concatenate([first, second], axis=-1).reshape(k, heads * ROPE_DIM)


def _prep_weights(attn_norm_g, w_in, b_forget, q_norm_g, w_uq, kv_norm_g, w_ukv, fox_out_g,
                  mla_out_g, w_o, mlp_norm_g, w_up, w_down, final_norm_g):
    row = lambda v: v.reshape(1, -1).astype(F32)
    pad_lanes = lambda a: jnp.pad(a, ((0, 0), (0, LANES - a.shape[1])))
    wkr = w_in[:, OFF_KR:IN_COLS].reshape(D_MODEL, 1, ROPE_DIM)
    wkr = _rope_split_cols(jnp.broadcast_to(wkr, (D_MODEL, LANES // ROPE_DIM, ROPE_DIM)))
    uq = w_uq.reshape(Q_RANK, N_HEADS, MLA_QK_DIM)
    ukv = w_ukv.reshape(KV_RANK, N_HEADS, 2 * HEAD_DIM)
    tri = np.tril(np.ones((SCAN_CHUNK, SCAN_CHUNK), np.float32))
    w_rows = jnp.concatenate([w_in[:, OFF_FQ:OFF_FK], w_in[:, OFF_FV:OFF_FF], wkr], axis=1)
    w_cols = jnp.concatenate([w_in[:, OFF_FK:OFF_FV], w_in[:, OFF_CQ:OFF_CKV],
                              w_in[:, OFF_CKV:OFF_KR], pad_lanes(w_in[:, OFF_FF:OFF_CQ])], axis=1)
    wuq = jnp.concatenate([uq[:, :, :HEAD_DIM].reshape(Q_RANK, GROUP_WIDTH),
                           _rope_split_cols(uq[:, :, HEAD_DIM:])], axis=1)
    return {
        "g_attn": row(attn_norm_g),
        "w_rows": w_rows.T.astype(BF16),
        "w_cols": w_cols.astype(BF16),
        "bff": pad_lanes(row(b_forget)),
        "gq": row(q_norm_g),
        "gkv": row(kv_norm_g),
        "wuq": wuq.T.astype(BF16),
        "wuk": ukv[:, :, :HEAD_DIM].reshape(KV_RANK, GROUP_WIDTH).astype(BF16),
        "wuvt": ukv[:, :, HEAD_DIM:].reshape(KV_RANK, GROUP_WIDTH).T.astype(BF16),
        "tri": jnp.asarray(tri, BF16),
        "gfox": row(fox_out_g),
        "gmla": row(mla_out_g),
        "wo": w_o.astype(BF16),
        "gmlp": row(mlp_norm_g),
        "wup": w_up.astype(BF16),
        "wdown": w_down.astype(BF16),
        "gfin": row(final_norm_g),
    }


def _head_masks():
    pairs = N_HEADS // 2
    fox = np.zeros((pairs, 2, 2 * LANES), np.float32)
    mla = np.zeros((pairs, 2, 2 * LANES), np.float32)
    per_group = LANES // ROPE_DIM
    for p in range(pairs):
        for a in range(2):
            h = 2 * p + a
            fox[p, a, a * HEAD_DIM:(a + 1) * HEAD_DIM] = 1.0
            mla[p, a, a * HEAD_DIM:(a + 1) * HEAD_DIM] = 1.0
            for piece in range(3):
                fox[p, a, LANES + piece * N_HEADS + h] = 1.0
            hh = h % per_group
            for half in range(2):
                lo = LANES + half * (LANES // 2) + hh * ROPE_HALF
                mla[p, a, lo:lo + ROPE_HALF] = 1.0
    rep = lambda m: jnp.asarray(np.repeat(m[..., None], LANES, axis=-1), BF16)
    return rep(fox), rep(mla)


def kernel(x, positions, attn_norm_g, w_in, b_forget, q_norm_g, w_uq, kv_norm_g, w_ukv,
           fox_out_g, mla_out_g, w_o, mlp_norm_g, w_up, w_down, final_norm_g):
    b, s, d = x.shape
    depth = w_in.shape[0]
    inv_freq = ROPE_THETA ** (-jnp.arange(0, ROPE_DIM, 2, dtype=F32) / ROPE_DIM)
    cos_t, sin_t = _rope_tables(positions, inv_freq)
    fox_mask, mla_mask = _head_masks()
    ones_aux = jnp.ones((1, LANES, s), BF16)

    for l in range(depth):
        w = _prep_weights(attn_norm_g[l], w_in[l], b_forget[l], q_norm_g[l], w_uq[l],
                          kv_norm_g[l], w_ukv[l], fox_out_g[l], mla_out_g[l], w_o[l],
                          mlp_norm_g[l], w_up[l], w_down[l], final_norm_g)
        fqt, fk, fvt, faux, mqnt, mqrt, mkn, mkr, mvt = _projection(x, cos_t, sin_t, w)
        fox = _attention(fqt, ones_aux, fk, faux, fvt, fox_mask, False, 1)
        mla = _attention(mqnt, mqrt, mkn, mkr, mvt, mla_mask, True, 2)
        y = _out_mlp(x.reshape(b * s, d), fox.reshape(b * s, GROUP_WIDTH),
                     mla.reshape(b * s, GROUP_WIDTH), w, l == depth - 1)
        x = y.reshape(b, s, d)
    return x
```

```python
import functools
import math

import numpy as np
import jax
import jax.numpy as jnp
from jax import lax
from jax.experimental import pallas as pl
from jax.experimental.pallas import tpu as pltpu

D_MODEL = 1024
HEAD_DIM = 64
N_HEADS = 8
GROUP_WIDTH = N_HEADS * HEAD_DIM
ROPE_DIM = 32
ROPE_HALF = ROPE_DIM // 2
MLA_QK_DIM = HEAD_DIM + ROPE_DIM
Q_RANK = 384
KV_RANK = 256
D_FF = 4096
ROPE_THETA = 10000.0
EPS = 1e-6

OFF_FQ = 0
OFF_FK = OFF_FQ + GROUP_WIDTH
OFF_FV = OFF_FK + GROUP_WIDTH
OFF_FF = OFF_FV + GROUP_WIDTH
OFF_CQ = OFF_FF + N_HEADS
OFF_CKV = OFF_CQ + Q_RANK
OFF_KR = OFF_CKV + KV_RANK
IN_COLS = OFF_KR + ROPE_DIM

LANES = 128
VMEM_LIMIT_BYTES = 56 * 1024 * 1024

LOG2E = math.log2(math.e)
FOX_QSCALE = LOG2E / math.sqrt(HEAD_DIM)
MLA_QSCALE = LOG2E / math.sqrt(MLA_QK_DIM)

PROJ_ROWS = 1024
ATTN_TQ = 512
ATTN_TK = 512
MLP_ROWS = 512
FF_CHUNK = 1024
SCAN_CHUNK = 128
ONES_ROWS = 16
UNROLL_STEPS = 14
FINALIZE_UNROLL = 4

F32 = jnp.float32
BF16 = jnp.bfloat16
NT_DIMS = (((1,), (1,)), ((), ()))


def _rms(x, g):
    return x * lax.rsqrt(jnp.mean(x * x, axis=-1, keepdims=True) + EPS) * g


def _split3(x):
    hi = x.astype(BF16).astype(F32)
    mid = (x - hi).astype(BF16).astype(F32)
    lo = (x - hi - mid).astype(BF16).astype(F32)
    return hi, mid, lo


def _pack3(x):
    hi, mid, lo = _split3(x)
    packed = hi + pltpu.roll(mid, N_HEADS, axis=1) + pltpu.roll(lo, 2 * N_HEADS, axis=1)
    return packed.astype(BF16)


def _rope_table_kernel(pos_ref, freq_ref, cos_ref, sin_ref):
    ang = pos_ref[0].astype(F32) * freq_ref[...]
    cos_ref[0] = jnp.cos(ang)
    sin_ref[0] = jnp.sin(ang)


def _rope_tables(positions, inv_freq):
    b, s = positions.shape
    table = pl.BlockSpec((1, ROPE_HALF, s), lambda i: (i, 0, 0))
    return pl.pallas_call(
        _rope_table_kernel,
        grid=(b,),
        in_specs=[pl.BlockSpec((1, 1, s), lambda i: (i, 0, 0)),
                  pl.BlockSpec((ROPE_HALF, 1), lambda i: (0, 0))],
        out_specs=[table, table],
        out_shape=[jax.ShapeDtypeStruct((b, ROPE_HALF, s), F32)] * 2,
        name="rope_tables",
    )(positions.reshape(b, 1, s), inv_freq.reshape(ROPE_HALF, 1))


def _proj_kernel(x_ref, cos_ref, sin_ref, g_attn_ref, w_rows_ref, w_cols_ref, bff_ref, gq_ref,
                 gkv_ref, wuq_ref, wuk_ref, wuvt_ref, tri_ref,
                 fqt_ref, fk_ref, fvt_ref, faux_ref, mqnt_ref, mqrt_ref, mkn_ref, mkr_ref, mvt_ref,
                 h_sc, carry_sc):
    rows = x_ref.shape[1]

    @pl.when(pl.program_id(1) == 0)
    def _():
        carry_sc[...] = jnp.zeros_like(carry_sc)

    h_sc[...] = _rms(x_ref[0], g_attn_ref[...]).astype(BF16)
    h = h_sc[...]

    groups = LANES // ROPE_HALF
    row = lax.broadcasted_iota(jnp.int32, (LANES, rows), 0)
    cos_t = jnp.tile(cos_ref[0], (groups, 1))
    sin_t = jnp.tile(sin_ref[0], (groups, 1))
    sin_s = jnp.where(row < LANES // 2, -sin_t, sin_t)

    def rope_t(xt):
        return xt * cos_t + pltpu.roll(xt, LANES // 2, axis=0) * sin_s


    tk = jnp.dot(h, w_cols_ref[...], preferred_element_type=F32)
    o_cq, o_ckv, o_ff = GROUP_WIDTH, GROUP_WIDTH + Q_RANK, GROUP_WIDTH + Q_RANK + KV_RANK
    fk_ref[0] = tk[:, 0:o_cq].astype(BF16)

    ft = lax.dot_general(w_rows_ref[...], h, NT_DIMS, preferred_element_type=F32)
    fqt_ref[0] = (ft[0:GROUP_WIDTH] * FOX_QSCALE).astype(BF16)
    fvt_ref[0] = ft[GROUP_WIDTH:2 * GROUP_WIDTH].astype(BF16)
    mkr_ref[0] = rope_t(ft[2 * GROUP_WIDTH:2 * GROUP_WIDTH + LANES]).T.astype(BF16)

    lane = lax.broadcasted_iota(jnp.int32, (1, LANES), 1)
    head_lane = lane < N_HEADS
    z = tk[:, o_ff:o_ff + LANES] + bff_ref[...]
    log_f = jnp.where(head_lane, jax.nn.log_sigmoid(z), 0.0)
    tri = tri_ref[...]
    carry = carry_sc[0:1, :]
    for c in range(rows // SCAN_CHUNK):
        part = jnp.dot(tri, _pack3(log_f[c * SCAN_CHUNK:(c + 1) * SCAN_CHUNK, :]),
                       preferred_element_type=F32)
        part = (part + pltpu.roll(part, LANES - N_HEADS, axis=1)
                + pltpu.roll(part, LANES - 2 * N_HEADS, axis=1))
        cum = jnp.where(head_lane, part, 0.0) + carry
        carry = cum[SCAN_CHUNK - 1:SCAN_CHUNK, :]
        faux_ref[0, c * SCAN_CHUNK:(c + 1) * SCAN_CHUNK, :] = _pack3(cum * (-LOG2E))
    carry_sc[...] = jnp.broadcast_to(carry, carry_sc.shape)

    cq = _rms(tk[:, o_cq:o_ckv], gq_ref[...]).astype(BF16)
    qt = lax.dot_general(wuq_ref[...], cq, NT_DIMS, preferred_element_type=F32)
    mqnt_ref[0] = (qt[0:GROUP_WIDTH] * MLA_QSCALE).astype(BF16)
    for g in range(2 * N_HEADS * ROPE_HALF // LANES):
        lo = GROUP_WIDTH + g * LANES
        mqrt_ref[0, g * LANES:(g + 1) * LANES, :] = (
            rope_t(qt[lo:lo + LANES]) * MLA_QSCALE).astype(BF16)

    ckv = _rms(tk[:, o_ckv:o_ff], gkv_ref[...]).astype(BF16)
    mkn_ref[0] = jnp.dot(ckv, wuk_ref[...], preferred_element_type=F32).astype(BF16)
    mvt_ref[0] = lax.dot_general(wuvt_ref[...], ckv, NT_DIMS,
                                 preferred_element_type=F32).astype(BF16)


def _const_spec(shape):
    return pl.BlockSpec(shape, lambda *_: (0,) * len(shape))


def _projection(x, cos_t, sin_t, w):
    b, s, _ = x.shape
    tm = PROJ_ROWS
    row_spec = lambda width: pl.BlockSpec((1, tm, width), lambda i, j: (i, j, 0))
    col_spec = lambda height: pl.BlockSpec((1, height, tm), lambda i, j: (i, 0, j))
    consts = [w["g_attn"], w["w_rows"], w["w_cols"], w["bff"], w["gq"], w["gkv"], w["wuq"],
              w["wuk"], w["wuvt"], w["tri"]]
    row_out = lambda width: jax.ShapeDtypeStruct((b, s, width), BF16)
    col_out = lambda height: jax.ShapeDtypeStruct((b, height, s), BF16)
    return pl.pallas_call(
        _proj_kernel,
        grid=(b, s // tm),
        in_specs=[row_spec(D_MODEL), col_spec(ROPE_HALF), col_spec(ROPE_HALF)]
                 + [_const_spec(c.shape) for c in consts],
        out_specs=[col_spec(GROUP_WIDTH), row_spec(GROUP_WIDTH), col_spec(GROUP_WIDTH),
                   row_spec(LANES), col_spec(GROUP_WIDTH), col_spec(2 * LANES),
                   row_spec(GROUP_WIDTH), row_spec(LANES), col_spec(GROUP_WIDTH)],
        out_shape=[col_out(GROUP_WIDTH), row_out(GROUP_WIDTH), col_out(GROUP_WIDTH),
                   row_out(LANES), col_out(GROUP_WIDTH), col_out(2 * LANES),
                   row_out(GROUP_WIDTH), row_out(LANES), col_out(GROUP_WIDTH)],
        scratch_shapes=[pltpu.VMEM((tm, D_MODEL), BF16), pltpu.VMEM((8, LANES), F32)],
        compiler_params=pltpu.CompilerParams(
            dimension_semantics=("arbitrary", "arbitrary"),
            vmem_limit_bytes=VMEM_LIMIT_BYTES),
        name="input_projection",
    )(x, cos_t, sin_t, *consts)


def _attn_kernel(tasks_ref, qm_ref, qa_ref, km_ref, ka_ref, vt_ref, mask_ref, o_ref,
                 bias_sc, qh_sc, s0_sc, s1_sc, acc_sc, m_sc):
    s_len = km_ref.shape[1]
    tq, tk = ATTN_TQ, ATTN_TK

    @pl.when((pl.program_id(0) == 0) & (pl.program_id(1) == 0))
    def _():
        key_idx = lax.broadcasted_iota(jnp.int32, (tk, tq), 0)
        qry_idx = lax.broadcasted_iota(jnp.int32, (tk, tq), 1)
        bias_sc[...] = jnp.where(key_idx <= qry_idx, 0.0, -jnp.inf)

    def kv_tile(k0):
        return jnp.concatenate([km_ref[0, pl.ds(k0, tk), :], ka_ref[0, pl.ds(k0, tk), :]], axis=-1)

    ones_rows = jnp.ones((ONES_ROWS, tk), BF16)

    qcat = jnp.concatenate([qm_ref[0], qa_ref[0]], axis=0)
    for a in range(2):
        qh_sc[a] = qcat * jnp.tile(mask_ref[0, a], (1, s_len // LANES))
    last_tile = s_len // tq - 1
    acc_sc[last_tile] = jnp.zeros(acc_sc.shape[1:], F32)
    m_sc[last_tile] = jnp.full(m_sc.shape[1:], -jnp.inf, F32)

    def produce(dst, t, diagonal):
        q0 = pl.multiple_of(tasks_ref[0, t] * tq, tq)
        k0 = pl.multiple_of(tasks_ref[1, t] * tk, tk)
        q_pair = jnp.concatenate([qh_sc[a, :, pl.ds(q0, tq)] for a in range(2)], axis=1)
        sc = jnp.dot(kv_tile(k0), q_pair, preferred_element_type=F32)
        maxima = []
        for a in range(2):
            sc_a = sc[:, a * tq:(a + 1) * tq]
            if diagonal:
                sc_a = sc_a + bias_sc[...]
            dst[a] = sc_a
            maxima.append(jnp.max(sc_a, axis=0, keepdims=True))
        return tuple(maxima)

    def consume(src, t, maxima):
        i = tasks_ref[0, t]
        k0 = pl.multiple_of(tasks_ref[1, t] * tk, tk)
        alphas, ps = [], []
        for a in range(2):
            m_run = m_sc[i, a]
            m_new = jnp.maximum(m_run, maxima[a])
            alphas.append(jnp.exp2(m_run - m_new))
            ps.append(jnp.exp2(src[a] - m_new).astype(BF16))
            m_sc[i, a] = m_new
        vt = jnp.concatenate([vt_ref[0, :, pl.ds(k0, tk)], ones_rows], axis=0)
        pv = jnp.dot(vt, jnp.concatenate(ps, axis=1), preferred_element_type=F32)
        for a in range(2):
            pv_a = jnp.concatenate([pv[a * HEAD_DIM:(a + 1) * HEAD_DIM, a * tq:(a + 1) * tq],
                                    pv[2 * HEAD_DIM:, a * tq:(a + 1) * tq]], axis=0)
            acc_sc[i, a] = alphas[a] * acc_sc[i, a] + pv_a

    half = tq // 2

    def produce_diag(dst, i, a):
        q0 = i * tq
        kcat = kv_tile(q0)
        top = (jnp.dot(kcat[0:half], qh_sc[a, :, q0:q0 + tq], preferred_element_type=F32)
               + bias_sc[0:half, :])
        bot = (jnp.dot(kcat[half:tk], qh_sc[a, :, q0 + half:q0 + tq], preferred_element_type=F32)
               + bias_sc[half:tk, half:tq])
        dst[a, 0:half, :] = top
        dst[a, half:tk, half:tq] = bot
        mx_top = jnp.max(top, axis=0, keepdims=True)
        mx_bot = jnp.max(bot, axis=0, keepdims=True)
        return jnp.concatenate([mx_top[:, 0:half], jnp.maximum(mx_top[:, half:tq], mx_bot)], axis=1)

    def consume_diag(src, i, a, maxima):
        q0 = i * tq
        p_top = jnp.exp2(src[a, 0:half, :] - maxima).astype(BF16)
        p_bot = jnp.exp2(src[a, half:tk, half:tq] - maxima[:, half:tq]).astype(BF16)
        vt = jnp.concatenate(
            [vt_ref[0, a * HEAD_DIM:(a + 1) * HEAD_DIM, q0:q0 + tk], ones_rows], axis=0)
        pv_left = jnp.dot(vt[:, 0:half], p_top[:, 0:half], preferred_element_type=F32)
        pv_right = jnp.dot(vt, jnp.concatenate([p_top[:, half:tq], p_bot], axis=0),
                           preferred_element_type=F32)
        acc_sc[i, a] = jnp.concatenate([pv_left, pv_right], axis=1)
        m_sc[i, a] = maxima

    bufs = (s0_sc, s1_sc)

    def step(t, parity, diagonal, mx):
        mx_next = produce(bufs[1 - parity], t + 1, diagonal)
        consume(bufs[parity], t, mx)
        return mx_next

    def steps(first, count, diagonal):
        def body(it, mx):
            t = first + it * count
            for u in range(count):
                mx = step(t + u, (first + u) % 2, diagonal, mx)
            return mx
        return body

    n_diag = s_len // tq
    n_tasks = n_diag * (n_diag + 1) // 2
    n_off = n_tasks - n_diag
    assert n_diag % 2 == 0 and n_off % UNROLL_STEPS == 0 and UNROLL_STEPS % 2 == 0
    assert tq == tk
    mx = tuple(produce_diag(s0_sc, 0, a) for a in range(2))
    for t in range(n_diag - 1):
        if t + 1 < n_diag - 1:
            mx_next = tuple(produce_diag(bufs[(t + 1) % 2], t + 1, a) for a in range(2))
        else:
            mx_next = produce(bufs[(t + 1) % 2], t + 1, True)
        for a in range(2):
            consume_diag(bufs[t % 2], t, a, mx[a])
        mx = mx_next
    mx = lax.fori_loop(0, n_off // UNROLL_STEPS, steps(n_diag - 1, UNROLL_STEPS, False), mx)
    consume(bufs[(n_tasks - 1) % 2], n_tasks - 1, mx)

    def finalize(it, _):
        for u in range(FINALIZE_UNROLL):
            i = it * FINALIZE_UNROLL + u
            q0 = pl.multiple_of(i * tq, tq)
            out_t = jnp.concatenate(
                [acc_sc[i, a, 0:HEAD_DIM, :] / acc_sc[i, a, HEAD_DIM:HEAD_DIM + 1, :]
                 for a in range(2)], axis=0)
            o_ref[0, pl.ds(q0, tq), :] = out_t.T
        return 0

    assert n_diag % FINALIZE_UNROLL == 0
    lax.fori_loop(0, n_diag // FINALIZE_UNROLL, finalize, 0)


def _attention(qmt, qat, km, ka, vt, mask, qa_batched, qa_per_pairs):
    b, s, _ = km.shape
    pairs = N_HEADS // 2
    tok = lambda f: pl.BlockSpec((1, s, LANES), f)
    feat = lambda f: pl.BlockSpec((1, LANES, s), f)
    qa_map = (lambda i, p: (i, p // qa_per_pairs, 0)) if qa_batched else (lambda i, p: (0, 0, 0))
    n_q = s // ATTN_TQ
    task_list = [(i, i) for i in range(n_q)] + [(i, j) for i in range(n_q) for j in range(i)]
    tasks = jnp.asarray(np.array(task_list, np.int32).T)
    return pl.pallas_call(
        _attn_kernel,
        grid=(b, pairs),
        in_specs=[pl.BlockSpec(memory_space=pltpu.SMEM),
                  feat(lambda i, p: (i, p, 0)), feat(qa_map), tok(lambda i, p: (i, 0, p)),
                  tok(lambda i, p: (i, 0, 0)), feat(lambda i, p: (i, p, 0)),
                  pl.BlockSpec((1, 2, 2 * LANES, LANES), lambda i, p: (p, 0, 0, 0))],
        out_specs=tok(lambda i, p: (i, 0, p)),
        out_shape=jax.ShapeDtypeStruct((b, s, GROUP_WIDTH), F32),
        scratch_shapes=[pltpu.VMEM((ATTN_TK, ATTN_TQ), F32),
                        pltpu.VMEM((2, 2 * LANES, s), BF16),
                        pltpu.VMEM((2, ATTN_TK, ATTN_TQ), F32),
                        pltpu.VMEM((2, ATTN_TK, ATTN_TQ), F32),
                        pltpu.VMEM((n_q, 2, HEAD_DIM + ONES_ROWS, ATTN_TQ), F32),
                        pltpu.VMEM((n_q, 2, 1, ATTN_TQ), F32)],
        compiler_params=pltpu.CompilerParams(
            dimension_semantics=("arbitrary", "arbitrary"),
            vmem_limit_bytes=VMEM_LIMIT_BYTES),
        name="causal_attention",
    )(tasks, qmt, qat, km, ka, vt, mask)


def _mlp_kernel(x_ref, fox_ref, mla_ref, gfox_ref, gmla_ref, wo_ref, gmlp_ref, wup_ref,
                wdown_ref, gfin_ref, o_ref, *, final):
    mixed = jnp.concatenate([_rms(fox_ref[...], gfox_ref[...]).astype(BF16),
                             _rms(mla_ref[...], gmla_ref[...]).astype(BF16)], axis=-1)
    x1 = x_ref[...] + jnp.dot(mixed, wo_ref[...], preferred_element_type=F32)
    h = _rms(x1, gmlp_ref[...]).astype(BF16)
    y = x1
    for c in range(D_FF // FF_CHUNK):
        u = jnp.dot(h, wup_ref[:, c * FF_CHUNK:(c + 1) * FF_CHUNK], preferred_element_type=F32)
        act = jnp.square(jnp.maximum(u, 0.0)).astype(BF16)
        y = y + jnp.dot(act, wdown_ref[c * FF_CHUNK:(c + 1) * FF_CHUNK, :],
                        preferred_element_type=F32)
    o_ref[...] = _rms(y, gfin_ref[...]) if final else y


def _out_mlp(x2d, fox, mla, w, final):
    n = x2d.shape[0]
    tm = MLP_ROWS
    row = lambda width: pl.BlockSpec((tm, width), lambda i: (i, 0))
    resident = lambda a: pl.BlockSpec(a.shape, lambda i: (0,) * a.ndim,
                                      pipeline_mode=pl.Buffered(1))
    consts = [w["gfox"], w["gmla"], w["wo"], w["gmlp"], w["wup"], w["wdown"], w["gfin"]]
    return pl.pallas_call(
        functools.partial(_mlp_kernel, final=final),
        grid=(n // tm,),
        in_specs=[row(D_MODEL), row(GROUP_WIDTH), row(GROUP_WIDTH)] + [resident(c) for c in consts],
        out_specs=row(D_MODEL),
        out_shape=jax.ShapeDtypeStruct((n, D_MODEL), F32),
        compiler_params=pltpu.CompilerParams(
            dimension_semantics=("arbitrary",),
            vmem_limit_bytes=VMEM_LIMIT_BYTES),
        name="out_proj_mlp",
    )(x2d, fox, mla, *consts)


def _rope_split_cols(wr):
    k = wr.shape[0]
    heads = wr.shape[1]
    per_group = LANES // ROPE_DIM
    first = wr[:, :, :ROPE_HALF].reshape(k, heads // per_group, per_group * ROPE_HALF)
    second = wr[:, :, ROPE_HALF:].reshape(k, heads // per_group, per_group * ROPE_HALF)
    return jnp.concatenate([first, second], axis=-1).reshape(k, heads * ROPE_DIM)


def _prep_weights(attn_norm_g, w_in, b_forget, q_norm_g, w_uq, kv_norm_g, w_ukv, fox_out_g,
                  mla_out_g, w_o, mlp_norm_g, w_up, w_down, final_norm_g):
    row = lambda v: v.reshape(1, -1).astype(F32)
    pad_lanes = lambda a: jnp.pad(a, ((0, 0), (0, LANES - a.shape[1])))
    wkr = w_in[:, OFF_KR:IN_COLS].reshape(D_MODEL, 1, ROPE_DIM)
    wkr = _rope_split_cols(jnp.broadcast_to(wkr, (D_MODEL, LANES // ROPE_DIM, ROPE_DIM)))
    uq = w_uq.reshape(Q_RANK, N_HEADS, MLA_QK_DIM)
    ukv = w_ukv.reshape(KV_RANK, N_HEADS, 2 * HEAD_DIM)
    tri = np.tril(np.ones((SCAN_CHUNK, SCAN_CHUNK), np.float32))
    w_rows = jnp.concatenate([w_in[:, OFF_FQ:OFF_FK], w_in[:, OFF_FV:OFF_FF], wkr], axis=1)
    w_cols = jnp.concatenate([w_in[:, OFF_FK:OFF_FV], w_in[:, OFF_CQ:OFF_CKV],
                              w_in[:, OFF_CKV:OFF_KR], pad_lanes(w_in[:, OFF_FF:OFF_CQ])], axis=1)
    wuq = jnp.concatenate([uq[:, :, :HEAD_DIM].reshape(Q_RANK, GROUP_WIDTH),
                           _rope_split_cols(uq[:, :, HEAD_DIM:])], axis=1)
    return {
        "g_attn": row(attn_norm_g),
        "w_rows": w_rows.T.astype(BF16),
        "w_cols": w_cols.astype(BF16),
        "bff": pad_lanes(row(b_forget)),
        "gq": row(q_norm_g),
        "gkv": row(kv_norm_g),
        "wuq": wuq.T.astype(BF16),
        "wuk": ukv[:, :, :HEAD_DIM].reshape(KV_RANK, GROUP_WIDTH).astype(BF16),
        "wuvt": ukv[:, :, HEAD_DIM:].reshape(KV_RANK, GROUP_WIDTH).T.astype(BF16),
        "tri": jnp.asarray(tri, BF16),
        "gfox": row(fox_out_g),
        "gmla": row(mla_out_g),
        "wo": w_o.astype(BF16),
        "gmlp": row(mlp_norm_g),
        "wup": w_up.astype(BF16),
        "wdown": w_down.astype(BF16),
        "gfin": row(final_norm_g),
    }


def _head_masks():
    pairs = N_HEADS // 2
    fox = np.zeros((pairs, 2, 2 * LANES), np.float32)
    mla = np.zeros((pairs, 2, 2 * LANES), np.float32)
    per_group = LANES // ROPE_DIM
    for p in range(pairs):
        for a in range(2):
            h = 2 * p + a
            fox[p, a, a * HEAD_DIM:(a + 1) * HEAD_DIM] = 1.0
            mla[p, a, a * HEAD_DIM:(a + 1) * HEAD_DIM] = 1.0
            for piece in range(3):
                fox[p, a, LANES + piece * N_HEADS + h] = 1.0
            hh = h % per_group
            for half in range(2):
                lo = LANES + half * (LANES // 2) + hh * ROPE_HALF
                mla[p, a, lo:lo + ROPE_HALF] = 1.0
    rep = lambda m: jnp.asarray(np.repeat(m[..., None], LANES, axis=-1), BF16)
    return rep(fox), rep(mla)


def kernel(x, positions, attn_norm_g, w_in, b_forget, q_norm_g, w_uq, kv_norm_g, w_ukv,
           fox_out_g, mla_out_g, w_o, mlp_norm_g, w_up, w_down, final_norm_g):
    b, s, d = x.shape
    depth = w_in.shape[0]
    inv_freq = ROPE_THETA ** (-jnp.arange(0, ROPE_DIM, 2, dtype=F32) / ROPE_DIM)
    cos_t, sin_t = _rope_tables(positions, inv_freq)
    fox_mask, mla_mask = _head_masks()
    ones_aux = jnp.ones((1, LANES, s), BF16)

    for l in range(depth):
        w = _prep_weights(attn_norm_g[l], w_in[l], b_forget[l], q_norm_g[l], w_uq[l],
                          kv_norm_g[l], w_ukv[l], fox_out_g[l], mla_out_g[l], w_o[l],
                          mlp_norm_g[l], w_up[l], w_down[l], final_norm_g)
        fqt, fk, fvt, faux, mqnt, mqrt, mkn, mkr, mvt = _projection(x, cos_t, sin_t, w)
        fox = _attention(fqt, ones_aux, fk, faux, fvt, fox_mask, False, 1)
        mla = _attention(mqnt, mqrt, mkn, mkr, mvt, mla_mask, True, 2)
        y = _out_mlp(x.reshape(b * s, d), fox.reshape(b * s, GROUP_WIDTH),
                     mla.reshape(b * s, GROUP_WIDTH), w, l == depth - 1)
        x = y.reshape(b, s, d)
    return x
```

```python
import functools
import math

import numpy as np
import jax
import jax.numpy as jnp
from jax import lax
from jax.experimental import pallas as pl
from jax.experimental.pallas import tpu as pltpu

D_MODEL = 1024
HEAD_DIM = 64
N_HEADS = 8
GROUP_WIDTH = N_HEADS * HEAD_DIM
ROPE_DIM = 32
ROPE_HALF = ROPE_DIM // 2
MLA_QK_DIM = HEAD_DIM + ROPE_DIM
Q_RANK = 384
KV_RANK = 256
D_FF = 4096
ROPE_THETA = 10000.0
EPS = 1e-6

OFF_FQ = 0
OFF_FK = OFF_FQ + GROUP_WIDTH
OFF_FV = OFF_FK + GROUP_WIDTH
OFF_FF = OFF_FV + GROUP_WIDTH
OFF_CQ = OFF_FF + N_HEADS
OFF_CKV = OFF_CQ + Q_RANK
OFF_KR = OFF_CKV + KV_RANK
IN_COLS = OFF_KR + ROPE_DIM

LANES = 128
VMEM_LIMIT_BYTES = 56 * 1024 * 1024

LOG2E = math.log2(math.e)
FOX_QSCALE = LOG2E / math.sqrt(HEAD_DIM)
MLA_QSCALE = LOG2E / math.sqrt(MLA_QK_DIM)

PROJ_ROWS = 1024
ATTN_TQ = 512
ATTN_TK = 512
MLP_ROWS = 512
FF_CHUNK = 1024
SCAN_CHUNK = 128
ONES_ROWS = 16
UNROLL_STEPS = 14
FINALIZE_UNROLL = 4

F32 = jnp.float32
BF16 = jnp.bfloat16
NT_DIMS = (((1,), (1,)), ((), ()))


def _rms(x, g):
    return x * lax.rsqrt(jnp.mean(x * x, axis=-1, keepdims=True) + EPS) * g


def _split3(x):
    hi = x.astype(BF16).astype(F32)
    mid = (x - hi).astype(BF16).astype(F32)
    lo = (x - hi - mid).astype(BF16).astype(F32)
    return hi, mid, lo


def _pack3(x):
    hi, mid, lo = _split3(x)
    packed = hi + pltpu.roll(mid, N_HEADS, axis=1) + pltpu.roll(lo, 2 * N_HEADS, axis=1)
    return packed.astype(BF16)


def _rope_table_kernel(pos_ref, freq_ref, cos_ref, sin_ref):
    ang = pos_ref[0].astype(F32) * freq_ref[...]
    cos_ref[0] = jnp.cos(ang)
    sin_ref[0] = jnp.sin(ang)


def _rope_tables(positions, inv_freq):
    b, s = positions.shape
    table = pl.BlockSpec((1, ROPE_HALF, s), lambda i: (i, 0, 0))
    return pl.pallas_call(
        _rope_table_kernel,
        grid=(b,),
        in_specs=[pl.BlockSpec((1, 1, s), lambda i: (i, 0, 0)),
                  pl.BlockSpec((ROPE_HALF, 1), lambda i: (0, 0))],
        out_specs=[table, table],
        out_shape=[jax.ShapeDtypeStruct((b, ROPE_HALF, s), F32)] * 2,
        name="rope_tables",
    )(positions.reshape(b, 1, s), inv_freq.reshape(ROPE_HALF, 1))


def _proj_kernel(x_ref, cos_ref, sin_ref, g_attn_ref, w_rows_ref, w_cols_ref, bff_ref, gq_ref,
                 gkv_ref, wuq_ref, wuk_ref, wuvt_ref, tri_ref,
                 fqt_ref, fk_ref, fvt_ref, faux_ref, mqnt_ref, mqrt_ref, mkn_ref, mkr_ref, mvt_ref,
                 h_sc, carry_sc):
    rows = x_ref.shape[1]

    @pl.when(pl.program_id(1) == 0)
    def _():
        carry_sc[...] = jnp.zeros_like(carry_sc)

    h_sc[...] = _rms(x_ref[0], g_attn_ref[...]).astype(BF16)
    h = h_sc[...]

    groups = LANES // ROPE_HALF
    row = lax.broadcasted_iota(jnp.int32, (LANES, rows), 0)
    cos_t = jnp.tile(cos_ref[0], (groups, 1))
    sin_t = jnp.tile(sin_ref[0], (groups, 1))
    sin_s = jnp.where(row < LANES // 2, -sin_t, sin_t)

    def rope_t(xt):
        return xt * cos_t + pltpu.roll(xt, LANES // 2, axis=0) * sin_s


    tk = jnp.dot(h, w_cols_ref[...], preferred_element_type=F32)
    o_cq, o_ckv, o_ff = GROUP_WIDTH, GROUP_WIDTH + Q_RANK, GROUP_WIDTH + Q_RANK + KV_RANK
    fk_ref[0] = tk[:, 0:o_cq].astype(BF16)

    ft = lax.dot_general(w_rows_ref[...], h, NT_DIMS, preferred_element_type=F32)
    fqt_ref[0] = (ft[0:GROUP_WIDTH] * FOX_QSCALE).astype(BF16)
    fvt_ref[0] = ft[GROUP_WIDTH:2 * GROUP_WIDTH].astype(BF16)
    mkr_ref[0] = rope_t(ft[2 * GROUP_WIDTH:2 * GROUP_WIDTH + LANES]).T.astype(BF16)

    lane = lax.broadcasted_iota(jnp.int32, (1, LANES), 1)
    head_lane = lane < N_HEADS
    z = tk[:, o_ff:o_ff + LANES] + bff_ref[...]
    log_f = jnp.where(head_lane, jax.nn.log_sigmoid(z), 0.0)
    tri = tri_ref[...]
    carry = carry_sc[0:1, :]
    for c in range(rows // SCAN_CHUNK):
        part = jnp.dot(tri, _pack3(log_f[c * SCAN_CHUNK:(c + 1) * SCAN_CHUNK, :]),
                       preferred_element_type=F32)
        part = (part + pltpu.roll(part, LANES - N_HEADS, axis=1)
                + pltpu.roll(part, LANES - 2 * N_HEADS, axis=1))
        cum = jnp.where(head_lane, part, 0.0) + carry
        carry = cum[SCAN_CHUNK - 1:SCAN_CHUNK, :]
        faux_ref[0, c * SCAN_CHUNK:(c + 1) * SCAN_CHUNK, :] = _pack3(cum * (-LOG2E))
    carry_sc[...] = jnp.broadcast_to(carry, carry_sc.shape)

    cq = _rms(tk[:, o_cq:o_ckv], gq_ref[...]).astype(BF16)
    qt = lax.dot_general(wuq_ref[...], cq, NT_DIMS, preferred_element_type=F32)
    mqnt_ref[0] = (qt[0:GROUP_WIDTH] * MLA_QSCALE).astype(BF16)
    for g in range(2 * N_HEADS * ROPE_HALF // LANES):
        lo = GROUP_WIDTH + g * LANES
        mqrt_ref[0, g * LANES:(g + 1) * LANES, :] = (
            rope_t(qt[lo:lo + LANES]) * MLA_QSCALE).astype(BF16)

    ckv = _rms(tk[:, o_ckv:o_ff], gkv_ref[...]).astype(BF16)
    mkn_ref[0] = jnp.dot(ckv, wuk_ref[...], preferred_element_type=F32).astype(BF16)
    mvt_ref[0] = lax.dot_general(wuvt_ref[...], ckv, NT_DIMS,
                                 preferred_element_type=F32).astype(BF16)


def _const_spec(shape):
    return pl.BlockSpec(shape, lambda *_: (0,) * len(shape))


def _projection(x, cos_t, sin_t, w):
    b, s, _ = x.shape
    tm = PROJ_ROWS
    row_spec = lambda width: pl.BlockSpec((1, tm, width), lambda i, j: (i, j, 0))
    col_spec = lambda height: pl.BlockSpec((1, height, tm), lambda i, j: (i, 0, j))
    consts = [w["g_attn"], w["w_rows"], w["w_cols"], w["bff"], w["gq"], w["gkv"], w["wuq"],
              w["wuk"], w["wuvt"], w["tri"]]
    row_out = lambda width: jax.ShapeDtypeStruct((b, s, width), BF16)
    col_out = lambda height: jax.ShapeDtypeStruct((b, height, s), BF16)
    return pl.pallas_call(
        _proj_kernel,
        grid=(b, s // tm),
        in_specs=[row_spec(D_MODEL), col_spec(ROPE_HALF), col_spec(ROPE_HALF)]
                 + [_const_spec(c.shape) for c in consts],
        out_specs=[col_spec(GROUP_WIDTH), row_spec(GROUP_WIDTH), col_spec(GROUP_WIDTH),
                   row_spec(LANES), col_spec(GROUP_WIDTH), col_spec(2 * LANES),
                   row_spec(GROUP_WIDTH), row_spec(LANES), col_spec(GROUP_WIDTH)],
        out_shape=[col_out(GROUP_WIDTH), row_out(GROUP_WIDTH), col_out(GROUP_WIDTH),
                   row_out(LANES), col_out(GROUP_WIDTH), col_out(2 * LANES),
                   row_out(GROUP_WIDTH), row_out(LANES), col_out(GROUP_WIDTH)],
        scratch_shapes=[pltpu.VMEM((tm, D_MODEL), BF16), pltpu.VMEM((8, LANES), F32)],
        compiler_params=pltpu.CompilerParams(
            dimension_semantics=("arbitrary", "arbitrary"),
            vmem_limit_bytes=VMEM_LIMIT_BYTES),
        name="input_projection",
    )(x, cos_t, sin_t, *consts)


def _attn_kernel(tasks_ref, qm_ref, qa_ref, km_ref, ka_ref, vt_ref, mask_ref, o_ref,
                 bias_sc, qh_sc, s0_sc, s1_sc, acc_sc, m_sc):
    s_len = km_ref.shape[1]
    tq, tk = ATTN_TQ, ATTN_TK

    @pl.when((pl.program_id(0) == 0) & (pl.program_id(1) == 0))
    def _():
        key_idx = lax.broadcasted_iota(jnp.int32, (tk, tq), 0)
        qry_idx = lax.broadcasted_iota(jnp.int32, (tk, tq), 1)
        bias_sc[...] = jnp.where(key_idx <= qry_idx, 0.0, -jnp.inf)

    def kv_tile(k0):
        return jnp.concatenate([km_ref[0, pl.ds(k0, tk), :], ka_ref[0, pl.ds(k0, tk), :]], axis=-1)

    ones_rows = jnp.ones((ONES_ROWS, tk), BF16)

    qcat = jnp.concatenate([qm_ref[0], qa_ref[0]], axis=0)
    for a in range(2):
        qh_sc[a] = qcat * jnp.tile(mask_ref[0, a], (1, s_len // LANES))
    last_tile = s_len // tq - 1
    acc_sc[last_tile] = jnp.zeros(acc_sc.shape[1:], F32)
    m_sc[last_tile] = jnp.full(m_sc.shape[1:], -jnp.inf, F32)

    def produce(dst, t, diagonal):
        q0 = pl.multiple_of(tasks_ref[0, t] * tq, tq)
        k0 = pl.multiple_of(tasks_ref[1, t] * tk, tk)
        q_pair = jnp.concatenate([qh_sc[a, :, pl.ds(q0, tq)] for a in range(2)], axis=1)
        sc = jnp.dot(kv_tile(k0), q_pair, preferred_element_type=F32)
        maxima = []
        for a in range(2):
            sc_a = sc[:, a * tq:(a + 1) * tq]
            if diagonal:
                sc_a = sc_a + bias_sc[...]
            dst[a] = sc_a
            maxima.append(jnp.max(sc_a, axis=0, keepdims=True))
        return tuple(maxima)

    def consume(src, t, maxima):
        i = tasks_ref[0, t]
        k0 = pl.multiple_of(tasks_ref[1, t] * tk, tk)
        alphas, ps = [], []
        for a in range(2):
            m_run = m_sc[i, a]
            m_new = jnp.maximum(m_run, maxima[a])
            alphas.append(jnp.exp2(m_run - m_new))
            ps.append(jnp.exp2(src[a] - m_new).astype(BF16))
            m_sc[i, a] = m_new
        vt = jnp.concatenate([vt_ref[0, :, pl.ds(k0, tk)], ones_rows], axis=0)
        pv = jnp.dot(vt, jnp.concatenate(ps, axis=1), preferred_element_type=F32)
        for a in range(2):
            pv_a = jnp.concatenate([pv[a * HEAD_DIM:(a + 1) * HEAD_DIM, a * tq:(a + 1) * tq],
                                    pv[2 * HEAD_DIM:, a * tq:(a + 1) * tq]], axis=0)
            acc_sc[i, a] = alphas[a] * acc_sc[i, a] + pv_a

    half = tq // 2

    def produce_diag(dst, i):
        q0 = i * tq
        kcat = kv_tile(q0)
        q_all = jnp.concatenate([qh_sc[a, :, q0:q0 + tq] for a in range(2)], axis=1)
        q_late = jnp.concatenate([qh_sc[a, :, q0 + half:q0 + tq] for a in range(2)], axis=1)
        top = jnp.dot(kcat[0:half], q_all, preferred_element_type=F32)
        bot = jnp.dot(kcat[half:tk], q_late, preferred_element_type=F32)
        maxima = []
        for a in range(2):
            top_a = top[:, a * tq:(a + 1) * tq] + bias_sc[0:half, :]
            bot_a = bot[:, a * half:(a + 1) * half] + bias_sc[half:tk, half:tq]
            dst[a, 0:half, :] = top_a
            dst[a, half:tk, half:tq] = bot_a
            mx_top = jnp.max(top_a, axis=0, keepdims=True)
            mx_bot = jnp.max(bot_a, axis=0, keepdims=True)
            maxima.append(jnp.concatenate(
                [mx_top[:, 0:half], jnp.maximum(mx_top[:, half:tq], mx_bot)], axis=1))
        return tuple(maxima)

    def consume_diag(src, i, maxima):
        q0 = i * tq
        early, late = [], []
        for a in range(2):
            p_top = jnp.exp2(src[a, 0:half, :] - maxima[a]).astype(BF16)
            p_bot = jnp.exp2(src[a, half:tk, half:tq] - maxima[a][:, half:tq]).astype(BF16)
            early.append(p_top[:, 0:half])
            late.append(jnp.concatenate([p_top[:, half:tq], p_bot], axis=0))
        vt = jnp.concatenate([vt_ref[0, :, q0:q0 + tk], ones_rows], axis=0)
        pv_early = jnp.dot(vt[:, 0:half], jnp.concatenate(early, axis=1),
                           preferred_element_type=F32)
        pv_late = jnp.dot(vt, jnp.concatenate(late, axis=1), preferred_element_type=F32)
        for a in range(2):
            pv_a = jnp.concatenate([pv_early[:, a * half:(a + 1) * half],
                                    pv_late[:, a * half:(a + 1) * half]], axis=1)
            acc_sc[i, a] = jnp.concatenate(
                [pv_a[a * HEAD_DIM:(a + 1) * HEAD_DIM], pv_a[2 * HEAD_DIM:]], axis=0)
            m_sc[i, a] = maxima[a]

    bufs = (s0_sc, s1_sc)

    def step(t, parity, diagonal, mx):
        mx_next = produce(bufs[1 - parity], t + 1, diagonal)
        consume(bufs[parity], t, mx)
        return mx_next

    def steps(first, count, diagonal):
        def body(it, mx):
            t = first + it * count
            for u in range(count):
                mx = step(t + u, (first + u) % 2, diagonal, mx)
            return mx
        return body

    n_diag = s_len // tq
    n_tasks = n_diag * (n_diag + 1) // 2
    n_off = n_tasks - n_diag
    assert n_diag % 2 == 0 and n_off % UNROLL_STEPS == 0 and UNROLL_STEPS % 2 == 0
    assert tq == tk
    mx = produce_diag(s0_sc, 0)
    for t in range(n_diag - 1):
        if t + 1 < n_diag - 1:
            mx_next = produce_diag(bufs[(t + 1) % 2], t + 1)
        else:
            mx_next = produce(bufs[(t + 1) % 2], t + 1, True)
        consume_diag(bufs[t % 2], t, mx)
        mx = mx_next
    mx = lax.fori_loop(0, n_off // UNROLL_STEPS, steps(n_diag - 1, UNROLL_STEPS, False), mx)
    consume(bufs[(n_tasks - 1) % 2], n_tasks - 1, mx)

    def finalize(it, _):
        for u in range(FINALIZE_UNROLL):
            i = it * FINALIZE_UNROLL + u
            q0 = pl.multiple_of(i * tq, tq)
            out_t = jnp.concatenate(
                [acc_sc[i, a, 0:HEAD_DIM, :] / acc_sc[i, a, HEAD_DIM:HEAD_DIM + 1, :]
                 for a in range(2)], axis=0)
            o_ref[0, pl.ds(q0, tq), :] = out_t.T
        return 0

    assert n_diag % FINALIZE_UNROLL == 0
    lax.fori_loop(0, n_diag // FINALIZE_UNROLL, finalize, 0)


def _attention(qmt, qat, km, ka, vt, mask, qa_batched, qa_per_pairs):
    b, s, _ = km.shape
    pairs = N_HEADS // 2
    tok = lambda f: pl.BlockSpec((1, s, LANES), f)
    feat = lambda f: pl.BlockSpec((1, LANES, s), f)
    qa_map = (lambda i, p: (i, p // qa_per_pairs, 0)) if qa_batched else (lambda i, p: (0, 0, 0))
    n_q = s // ATTN_TQ
    task_list = [(i, i) for i in range(n_q)] + [(i, j) for i in range(n_q) for j in range(i)]
    tasks = jnp.asarray(np.array(task_list, np.int32).T)
    return pl.pallas_call(
        _attn_kernel,
        grid=(b, pairs),
        in_specs=[pl.BlockSpec(memory_space=pltpu.SMEM),
                  feat(lambda i, p: (i, p, 0)), feat(qa_map), tok(lambda i, p: (i, 0, p)),
                  tok(lambda i, p: (i, 0, 0)), feat(lambda i, p: (i, p, 0)),
                  pl.BlockSpec((1, 2, 2 * LANES, LANES), lambda i, p: (p, 0, 0, 0))],
        out_specs=tok(lambda i, p: (i, 0, p)),
        out_shape=jax.ShapeDtypeStruct((b, s, GROUP_WIDTH), F32),
        scratch_shapes=[pltpu.VMEM((ATTN_TK, ATTN_TQ), F32),
                        pltpu.VMEM((2, 2 * LANES, s), BF16),
                        pltpu.VMEM((2, ATTN_TK, ATTN_TQ), F32),
                        pltpu.VMEM((2, ATTN_TK, ATTN_TQ), F32),
                        pltpu.VMEM((n_q, 2, HEAD_DIM + ONES_ROWS, ATTN_TQ), F32),
                        pltpu.VMEM((n_q, 2, 1, ATTN_TQ), F32)],
        compiler_params=pltpu.CompilerParams(
            dimension_semantics=("arbitrary", "arbitrary"),
            vmem_limit_bytes=VMEM_LIMIT_BYTES),
        name="causal_attention",
    )(tasks, qmt, qat, km, ka, vt, mask)


def _mlp_kernel(x_ref, fox_ref, mla_ref, gfox_ref, gmla_ref, wo_ref, gmlp_ref, wup_ref,
                wdown_ref, gfin_ref, o_ref, *, final):
    mixed = jnp.concatenate([_rms(fox_ref[...], gfox_ref[...]).astype(BF16),
                             _rms(mla_ref[...], gmla_ref[...]).astype(BF16)], axis=-1)
    x1 = x_ref[...] + jnp.dot(mixed, wo_ref[...], preferred_element_type=F32)
    h = _rms(x1, gmlp_ref[...]).astype(BF16)
    y = x1
    for c in range(D_FF // FF_CHUNK):
        u = jnp.dot(h, wup_ref[:, c * FF_CHUNK:(c + 1) * FF_CHUNK], preferred_element_type=F32)
        act = jnp.square(jnp.maximum(u, 0.0)).astype(BF16)
        y = y + jnp.dot(act, wdown_ref[c * FF_CHUNK:(c + 1) * FF_CHUNK, :],
                        preferred_element_type=F32)
    o_ref[...] = _rms(y, gfin_ref[...]) if final else y


def _out_mlp(x2d, fox, mla, w, final):
    n = x2d.shape[0]
    tm = MLP_ROWS
    row = lambda width: pl.BlockSpec((tm, width), lambda i: (i, 0))
    resident = lambda a: pl.BlockSpec(a.shape, lambda i: (0,) * a.ndim,
                                      pipeline_mode=pl.Buffered(1))
    consts = [w["gfox"], w["gmla"], w["wo"], w["gmlp"], w["wup"], w["wdown"], w["gfin"]]
    return pl.pallas_call(
        functools.partial(_mlp_kernel, final=final),
        grid=(n // tm,),
        in_specs=[row(D_MODEL), row(GROUP_WIDTH), row(GROUP_WIDTH)] + [resident(c) for c in consts],
        out_specs=row(D_MODEL),
        out_shape=jax.ShapeDtypeStruct((n, D_MODEL), F32),
        compiler_params=pltpu.CompilerParams(
            dimension_semantics=("arbitrary",),
            vmem_limit_bytes=VMEM_LIMIT_BYTES),
        name="out_proj_mlp",
    )(x2d, fox, mla, *consts)


def _rope_split_cols(wr):
    k = wr.shape[0]
    heads = wr.shape[1]
    per_group = LANES // ROPE_DIM
    first = wr[:, :, :ROPE_HALF].reshape(k, heads // per_group, per_group * ROPE_HALF)
    second = wr[:, :, ROPE_HALF:].reshape(k, heads // per_group, per_group * ROPE_HALF)
    return jnp.concatenate([first, second], axis=-1).reshape(k, heads * ROPE_DIM)


def _prep_weights(attn_norm_g, w_in, b_forget, q_norm_g, w_uq, kv_norm_g, w_ukv, fox_out_g,
                  mla_out_g, w_o, mlp_norm_g, w_up, w_down, final_norm_g):
    row = lambda v: v.reshape(1, -1).astype(F32)
    pad_lanes = lambda a: jnp.pad(a, ((0, 0), (0, LANES - a.shape[1])))
    wkr = w_in[:, OFF_KR:IN_COLS].reshape(D_MODEL, 1, ROPE_DIM)
    wkr = _rope_split_cols(jnp.broadcast_to(wkr, (D_MODEL, LANES // ROPE_DIM, ROPE_DIM)))
    uq = w_uq.reshape(Q_RANK, N_HEADS, MLA_QK_DIM)
    ukv = w_ukv.reshape(KV_RANK, N_HEADS, 2 * HEAD_DIM)
    tri = np.tril(np.ones((SCAN_CHUNK, SCAN_CHUNK), np.float32))
    w_rows = jnp.concatenate([w_in[:, OFF_FQ:OFF_FK], w_in[:, OFF_FV:OFF_FF], wkr], axis=1)
    w_cols = jnp.concatenate([w_in[:, OFF_FK:OFF_FV], w_in[:, OFF_CQ:OFF_CKV],
                              w_in[:, OFF_CKV:OFF_KR], pad_lanes(w_in[:, OFF_FF:OFF_CQ])], axis=1)
    wuq = jnp.concatenate([uq[:, :, :HEAD_DIM].reshape(Q_RANK, GROUP_WIDTH),
                           _rope_split_cols(uq[:, :, HEAD_DIM:])], axis=1)
    return {
        "g_attn": row(attn_norm_g),
        "w_rows": w_rows.T.astype(BF16),
        "w_cols": w_cols.astype(BF16),
        "bff": pad_lanes(row(b_forget)),
        "gq": row(q_norm_g),
        "gkv": row(kv_norm_g),
        "wuq": wuq.T.astype(BF16),
        "wuk": ukv[:, :, :HEAD_DIM].reshape(KV_RANK, GROUP_WIDTH).astype(BF16),
        "wuvt": ukv[:, :, HEAD_DIM:].reshape(KV_RANK, GROUP_WIDTH).T.astype(BF16),
        "tri": jnp.asarray(tri, BF16),
        "gfox": row(fox_out_g),
        "gmla": row(mla_out_g),
        "wo": w_o.astype(BF16),
        "gmlp": row(mlp_norm_g),
        "wup": w_up.astype(BF16),
        "wdown": w_down.astype(BF16),
        "gfin": row(final_norm_g),
    }


def _head_masks():
    pairs = N_HEADS // 2
    fox = np.zeros((pairs, 2, 2 * LANES), np.float32)
    mla = np.zeros((pairs, 2, 2 * LANES), np.float32)
    per_group = LANES // ROPE_DIM
    for p in range(pairs):
        for a in range(2):
            h = 2 * p + a
            fox[p, a, a * HEAD_DIM:(a + 1) * HEAD_DIM] = 1.0
            mla[p, a, a * HEAD_DIM:(a + 1) * HEAD_DIM] = 1.0
            for piece in range(3):
                fox[p, a, LANES + piece * N_HEADS + h] = 1.0
            hh = h % per_group
            for half in range(2):
                lo = LANES + half * (LANES // 2) + hh * ROPE_HALF
                mla[p, a, lo:lo + ROPE_HALF] = 1.0
    rep = lambda m: jnp.asarray(np.repeat(m[..., None], LANES, axis=-1), BF16)
    return rep(fox), rep(mla)


def kernel(x, positions, attn_norm_g, w_in, b_forget, q_norm_g, w_uq, kv_norm_g, w_ukv,
           fox_out_g, mla_out_g, w_o, mlp_norm_g, w_up, w_down, final_norm_g):
    b, s, d = x.shape
    depth = w_in.shape[0]
    inv_freq = ROPE_THETA ** (-jnp.arange(0, ROPE_DIM, 2, dtype=F32) / ROPE_DIM)
    cos_t, sin_t = _rope_tables(positions, inv_freq)
    fox_mask, mla_mask = _head_masks()
    ones_aux = jnp.ones((1, LANES, s), BF16)

    for l in range(depth):
        w = _prep_weights(attn_norm_g[l], w_in[l], b_forget[l], q_norm_g[l], w_uq[l],
                          kv_norm_g[l], w_ukv[l], fox_out_g[l], mla_out_g[l], w_o[l],
                          mlp_norm_g[l], w_up[l], w_down[l], final_norm_g)
        fqt, fk, fvt, faux, mqnt, mqrt, mkn, mkr, mvt = _projection(x, cos_t, sin_t, w)
        fox = _attention(fqt, ones_aux, fk, faux, fvt, fox_mask, False, 1)
        mla = _attention(mqnt, mqrt, mkn, mkr, mvt, mla_mask, True, 2)
        y = _out_mlp(x.reshape(b * s, d), fox.reshape(b * s, GROUP_WIDTH),
                     mla.reshape(b * s, GROUP_WIDTH), w, l == depth - 1)
        x = y.reshape(b, s, d)
    return x
```

```python
import functools
import math

import numpy as np
import jax
import jax.numpy as jnp
from jax import lax
from jax.experimental import pallas as pl
from jax.experimental.pallas import tpu as pltpu

D_MODEL = 1024
HEAD_DIM = 64
N_HEADS = 8
GROUP_WIDTH = N_HEADS * HEAD_DIM
ROPE_DIM = 32
ROPE_HALF = ROPE_DIM // 2
MLA_QK_DIM = HEAD_DIM + ROPE_DIM
Q_RANK = 384
KV_RANK = 256
D_FF = 4096
ROPE_THETA = 10000.0
EPS = 1e-6

OFF_FQ = 0
OFF_FK = OFF_FQ + GROUP_WIDTH
OFF_FV = OFF_FK + GROUP_WIDTH
OFF_FF = OFF_FV + GROUP_WIDTH
OFF_CQ = OFF_FF + N_HEADS
OFF_CKV = OFF_CQ + Q_RANK
OFF_KR = OFF_CKV + KV_RANK
IN_COLS = OFF_KR + ROPE_DIM

LANES = 128
VMEM_LIMIT_BYTES = 56 * 1024 * 1024

LOG2E = math.log2(math.e)
FOX_QSCALE = LOG2E / math.sqrt(HEAD_DIM)
MLA_QSCALE = LOG2E / math.sqrt(MLA_QK_DIM)

PROJ_ROWS = 1024
ATTN_TQ = 512
ATTN_TK = 512
MLP_ROWS = 512
FF_CHUNK = 4096
SCAN_CHUNK = 128
ONES_ROWS = 16
UNROLL_STEPS = 14
FINALIZE_UNROLL = 4

F32 = jnp.float32
BF16 = jnp.bfloat16
NT_DIMS = (((1,), (1,)), ((), ()))


def _rms(x, g):
    return x * lax.rsqrt(jnp.mean(x * x, axis=-1, keepdims=True) + EPS) * g


def _split3(x):
    hi = x.astype(BF16).astype(F32)
    mid = (x - hi).astype(BF16).astype(F32)
    lo = (x - hi - mid).astype(BF16).astype(F32)
    return hi, mid, lo


def _pack3(x):
    hi, mid, lo = _split3(x)
    packed = hi + pltpu.roll(mid, N_HEADS, axis=1) + pltpu.roll(lo, 2 * N_HEADS, axis=1)
    return packed.astype(BF16)


def _rope_table_kernel(pos_ref, freq_ref, cos_ref, sin_ref):
    ang = pos_ref[0].astype(F32) * freq_ref[...]
    cos_ref[0] = jnp.cos(ang)
    sin_ref[0] = jnp.sin(ang)


def _rope_tables(positions, inv_freq):
    b, s = positions.shape
    table = pl.BlockSpec((1, ROPE_HALF, s), lambda i: (i, 0, 0))
    return pl.pallas_call(
        _rope_table_kernel,
        grid=(b,),
        in_specs=[pl.BlockSpec((1, 1, s), lambda i: (i, 0, 0)),
                  pl.BlockSpec((ROPE_HALF, 1), lambda i: (0, 0))],
        out_specs=[table, table],
        out_shape=[jax.ShapeDtypeStruct((b, ROPE_HALF, s), F32)] * 2,
        name="rope_tables",
    )(positions.reshape(b, 1, s), inv_freq.reshape(ROPE_HALF, 1))


def _proj_kernel(x_ref, cos_ref, sin_ref, g_attn_ref, w_rows_ref, w_cols_ref, bff_ref, gq_ref,
                 gkv_ref, wuq_ref, wuk_ref, wuvt_ref, tri_ref,
                 fqt_ref, fk_ref, fvt_ref, faux_ref, mqnt_ref, mqrt_ref, mkn_ref, mkr_ref, mvt_ref,
                 h_sc, carry_sc):
    rows = x_ref.shape[1]

    @pl.when(pl.program_id(1) == 0)
    def _():
        carry_sc[...] = jnp.zeros_like(carry_sc)

    h_sc[...] = _rms(x_ref[0], g_attn_ref[...]).astype(BF16)
    h = h_sc[...]

    groups = LANES // ROPE_HALF
    row = lax.broadcasted_iota(jnp.int32, (LANES, rows), 0)
    cos_t = jnp.tile(cos_ref[0], (groups, 1))
    sin_t = jnp.tile(sin_ref[0], (groups, 1))
    sin_s = jnp.where(row < LANES // 2, -sin_t, sin_t)

    def rope_t(xt):
        return xt * cos_t + pltpu.roll(xt, LANES // 2, axis=0) * sin_s


    tk = jnp.dot(h, w_cols_ref[...], preferred_element_type=F32)
    o_cq, o_ckv, o_ff = GROUP_WIDTH, GROUP_WIDTH + Q_RANK, GROUP_WIDTH + Q_RANK + KV_RANK
    fk_ref[0] = tk[:, 0:o_cq].astype(BF16)

    ft = lax.dot_general(w_rows_ref[...], h, NT_DIMS, preferred_element_type=F32)
    fqt_ref[0] = (ft[0:GROUP_WIDTH] * FOX_QSCALE).astype(BF16)
    fvt_ref[0] = ft[GROUP_WIDTH:2 * GROUP_WIDTH].astype(BF16)
    mkr_ref[0] = rope_t(ft[2 * GROUP_WIDTH:2 * GROUP_WIDTH + LANES]).T.astype(BF16)

    lane = lax.broadcasted_iota(jnp.int32, (1, LANES), 1)
    head_lane = lane < N_HEADS
    z = tk[:, o_ff:o_ff + LANES] + bff_ref[...]
    log_f = jnp.where(head_lane, jax.nn.log_sigmoid(z), 0.0)
    tri = tri_ref[...]
    carry = carry_sc[0:1, :]
    for c in range(rows // SCAN_CHUNK):
        part = jnp.dot(tri, _pack3(log_f[c * SCAN_CHUNK:(c + 1) * SCAN_CHUNK, :]),
                       preferred_element_type=F32)
        part = (part + pltpu.roll(part, LANES - N_HEADS, axis=1)
                + pltpu.roll(part, LANES - 2 * N_HEADS, axis=1))
        cum = jnp.where(head_lane, part, 0.0) + carry
        carry = cum[SCAN_CHUNK - 1:SCAN_CHUNK, :]
        faux_ref[0, c * SCAN_CHUNK:(c + 1) * SCAN_CHUNK, :] = _pack3(cum * (-LOG2E))
    carry_sc[...] = jnp.broadcast_to(carry, carry_sc.shape)

    cq = _rms(tk[:, o_cq:o_ckv], gq_ref[...]).astype(BF16)
    qt = lax.dot_general(wuq_ref[...], cq, NT_DIMS, preferred_element_type=F32)
    mqnt_ref[0] = (qt[0:GROUP_WIDTH] * MLA_QSCALE).astype(BF16)
    for g in range(2 * N_HEADS * ROPE_HALF // LANES):
        lo = GROUP_WIDTH + g * LANES
        mqrt_ref[0, g * LANES:(g + 1) * LANES, :] = (
            rope_t(qt[lo:lo + LANES]) * MLA_QSCALE).astype(BF16)

    ckv = _rms(tk[:, o_ckv:o_ff], gkv_ref[...]).astype(BF16)
    mkn_ref[0] = jnp.dot(ckv, wuk_ref[...], preferred_element_type=F32).astype(BF16)
    mvt_ref[0] = lax.dot_general(wuvt_ref[...], ckv, NT_DIMS,
                                 preferred_element_type=F32).astype(BF16)


def _const_spec(shape):
    return pl.BlockSpec(shape, lambda *_: (0,) * len(shape))


def _projection(x, cos_t, sin_t, w):
    b, s, _ = x.shape
    tm = PROJ_ROWS
    row_spec = lambda width: pl.BlockSpec((1, tm, width), lambda i, j: (i, j, 0))
    col_spec = lambda height: pl.BlockSpec((1, height, tm), lambda i, j: (i, 0, j))
    consts = [w["g_attn"], w["w_rows"], w["w_cols"], w["bff"], w["gq"], w["gkv"], w["wuq"],
              w["wuk"], w["wuvt"], w["tri"]]
    row_out = lambda width: jax.ShapeDtypeStruct((b, s, width), BF16)
    col_out = lambda height: jax.ShapeDtypeStruct((b, height, s), BF16)
    return pl.pallas_call(
        _proj_kernel,
        grid=(b, s // tm),
        in_specs=[row_spec(D_MODEL), col_spec(ROPE_HALF), col_spec(ROPE_HALF)]
                 + [_const_spec(c.shape) for c in consts],
        out_specs=[col_spec(GROUP_WIDTH), row_spec(GROUP_WIDTH), col_spec(GROUP_WIDTH),
                   row_spec(LANES), col_spec(GROUP_WIDTH), col_spec(2 * LANES),
                   row_spec(GROUP_WIDTH), row_spec(LANES), col_spec(GROUP_WIDTH)],
        out_shape=[col_out(GROUP_WIDTH), row_out(GROUP_WIDTH), col_out(GROUP_WIDTH),
                   row_out(LANES), col_out(GROUP_WIDTH), col_out(2 * LANES),
                   row_out(GROUP_WIDTH), row_out(LANES), col_out(GROUP_WIDTH)],
        scratch_shapes=[pltpu.VMEM((tm, D_MODEL), BF16), pltpu.VMEM((8, LANES), F32)],
        compiler_params=pltpu.CompilerParams(
            dimension_semantics=("arbitrary", "arbitrary"),
            vmem_limit_bytes=VMEM_LIMIT_BYTES),
        name="input_projection",
    )(x, cos_t, sin_t, *consts)


def _attn_kernel(tasks_ref, qm_ref, qa_ref, km_ref, ka_ref, vt_ref, mask_ref, o_ref,
                 bias_sc, qh_sc, s0_sc, s1_sc, acc_sc, m_sc):
    s_len = km_ref.shape[1]
    tq, tk = ATTN_TQ, ATTN_TK

    @pl.when((pl.program_id(0) == 0) & (pl.program_id(1) == 0))
    def _():
        key_idx = lax.broadcasted_iota(jnp.int32, (tk, tq), 0)
        qry_idx = lax.broadcasted_iota(jnp.int32, (tk, tq), 1)
        bias_sc[...] = jnp.where(key_idx <= qry_idx, 0.0, -jnp.inf)

    def kv_tile(k0):
        return jnp.concatenate([km_ref[0, pl.ds(k0, tk), :], ka_ref[0, pl.ds(k0, tk), :]], axis=-1)

    ones_rows = jnp.ones((ONES_ROWS, tk), BF16)

    qcat = jnp.concatenate([qm_ref[0], qa_ref[0]], axis=0)
    for a in range(2):
        qh_sc[a] = qcat * jnp.tile(mask_ref[0, a], (1, s_len // LANES))
    last_tile = s_len // tq - 1
    acc_sc[last_tile] = jnp.zeros(acc_sc.shape[1:], F32)
    m_sc[last_tile] = jnp.full(m_sc.shape[1:], -jnp.inf, F32)

    def produce(dst, t, diagonal):
        q0 = pl.multiple_of(tasks_ref[0, t] * tq, tq)
        k0 = pl.multiple_of(tasks_ref[1, t] * tk, tk)
        q_pair = jnp.concatenate([qh_sc[a, :, pl.ds(q0, tq)] for a in range(2)], axis=1)
        sc = jnp.dot(kv_tile(k0), q_pair, preferred_element_type=F32)
        maxima = []
        for a in range(2):
            sc_a = sc[:, a * tq:(a + 1) * tq]
            if diagonal:
                sc_a = sc_a + bias_sc[...]
            dst[a] = sc_a
            maxima.append(jnp.max(sc_a, axis=0, keepdims=True))
        return tuple(maxima)

    def consume(src, t, maxima):
        i = tasks_ref[0, t]
        k0 = pl.multiple_of(tasks_ref[1, t] * tk, tk)
        alphas, ps = [], []
        for a in range(2):
            m_run = m_sc[i, a]
            m_new = jnp.maximum(m_run, maxima[a])
            alphas.append(jnp.exp2(m_run - m_new))
            ps.append(jnp.exp2(src[a] - m_new).astype(BF16))
            m_sc[i, a] = m_new
        vt = jnp.concatenate([vt_ref[0, :, pl.ds(k0, tk)], ones_rows], axis=0)
        pv = jnp.dot(vt, jnp.concatenate(ps, axis=1), preferred_element_type=F32)
        for a in range(2):
            pv_a = jnp.concatenate([pv[a * HEAD_DIM:(a + 1) * HEAD_DIM, a * tq:(a + 1) * tq],
                                    pv[2 * HEAD_DIM:, a * tq:(a + 1) * tq]], axis=0)
            acc_sc[i, a] = alphas[a] * acc_sc[i, a] + pv_a

    half = tq // 2

    def produce_diag(dst, i):
        q0 = i * tq
        kcat = kv_tile(q0)
        q_all = jnp.concatenate([qh_sc[a, :, q0:q0 + tq] for a in range(2)], axis=1)
        q_late = jnp.concatenate([qh_sc[a, :, q0 + half:q0 + tq] for a in range(2)], axis=1)
        top = jnp.dot(kcat[0:half], q_all, preferred_element_type=F32)
        bot = jnp.dot(kcat[half:tk], q_late, preferred_element_type=F32)
        maxima = []
        for a in range(2):
            top_a = top[:, a * tq:(a + 1) * tq] + bias_sc[0:half, :]
            bot_a = bot[:, a * half:(a + 1) * half] + bias_sc[half:tk, half:tq]
            dst[a, 0:half, :] = top_a
            dst[a, half:tk, half:tq] = bot_a
            mx_top = jnp.max(top_a, axis=0, keepdims=True)
            mx_bot = jnp.max(bot_a, axis=0, keepdims=True)
            maxima.append(jnp.concatenate(
                [mx_top[:, 0:half], jnp.maximum(mx_top[:, half:tq], mx_bot)], axis=1))
        return tuple(maxima)

    def consume_diag(src, i, maxima):
        q0 = i * tq
        early, late = [], []
        for a in range(2):
            p_top = jnp.exp2(src[a, 0:half, :] - maxima[a]).astype(BF16)
            p_bot = jnp.exp2(src[a, half:tk, half:tq] - maxima[a][:, half:tq]).astype(BF16)
            early.append(p_top[:, 0:half])
            late.append(jnp.concatenate([p_top[:, half:tq], p_bot], axis=0))
        vt = jnp.concatenate([vt_ref[0, :, q0:q0 + tk], ones_rows], axis=0)
        pv_early = jnp.dot(vt[:, 0:half], jnp.concatenate(early, axis=1),
                           preferred_element_type=F32)
        pv_late = jnp.dot(vt, jnp.concatenate(late, axis=1), preferred_element_type=F32)
        for a in range(2):
            pv_a = jnp.concatenate([pv_early[:, a * half:(a + 1) * half],
                                    pv_late[:, a * half:(a + 1) * half]], axis=1)
            acc_sc[i, a] = jnp.concatenate(
                [pv_a[a * HEAD_DIM:(a + 1) * HEAD_DIM], pv_a[2 * HEAD_DIM:]], axis=0)
            m_sc[i, a] = maxima[a]

    bufs = (s0_sc, s1_sc)

    def step(t, parity, diagonal, mx):
        mx_next = produce(bufs[1 - parity], t + 1, diagonal)
        consume(bufs[parity], t, mx)
        return mx_next

    def steps(first, count, diagonal):
        def body(it, mx):
            t = first + it * count
            for u in range(count):
                mx = step(t + u, (first + u) % 2, diagonal, mx)
            return mx
        return body

    n_diag = s_len // tq
    n_tasks = n_diag * (n_diag + 1) // 2
    n_off = n_tasks - n_diag
    assert n_diag % 2 == 0 and n_off % UNROLL_STEPS == 0 and UNROLL_STEPS % 2 == 0
    assert tq == tk
    mx = produce_diag(s0_sc, 0)
    for t in range(n_diag - 1):
        if t + 1 < n_diag - 1:
            mx_next = produce_diag(bufs[(t + 1) % 2], t + 1)
        else:
            mx_next = produce(bufs[(t + 1) % 2], t + 1, True)
        consume_diag(bufs[t % 2], t, mx)
        mx = mx_next
    mx = lax.fori_loop(0, n_off // UNROLL_STEPS, steps(n_diag - 1, UNROLL_STEPS, False), mx)
    consume(bufs[(n_tasks - 1) % 2], n_tasks - 1, mx)

    def finalize(it, _):
        for u in range(FINALIZE_UNROLL):
            i = it * FINALIZE_UNROLL + u
            q0 = pl.multiple_of(i * tq, tq)
            out_t = jnp.concatenate(
                [acc_sc[i, a, 0:HEAD_DIM, :] / acc_sc[i, a, HEAD_DIM:HEAD_DIM + 1, :]
                 for a in range(2)], axis=0)
            o_ref[0, pl.ds(q0, tq), :] = out_t.T
        return 0

    assert n_diag % FINALIZE_UNROLL == 0
    lax.fori_loop(0, n_diag // FINALIZE_UNROLL, finalize, 0)


def _attention(qmt, qat, km, ka, vt, mask, qa_batched, qa_per_pairs):
    b, s, _ = km.shape
    pairs = N_HEADS // 2
    tok = lambda f: pl.BlockSpec((1, s, LANES), f)
    feat = lambda f: pl.BlockSpec((1, LANES, s), f)
    qa_map = (lambda i, p: (i, p // qa_per_pairs, 0)) if qa_batched else (lambda i, p: (0, 0, 0))
    n_q = s // ATTN_TQ
    task_list = [(i, i) for i in range(n_q)] + [(i, j) for i in range(n_q) for j in range(i)]
    tasks = jnp.asarray(np.array(task_list, np.int32).T)
    return pl.pallas_call(
        _attn_kernel,
        grid=(b, pairs),
        in_specs=[pl.BlockSpec(memory_space=pltpu.SMEM),
                  feat(lambda i, p: (i, p, 0)), feat(qa_map), tok(lambda i, p: (i, 0, p)),
                  tok(lambda i, p: (i, 0, 0)), feat(lambda i, p: (i, p, 0)),
                  pl.BlockSpec((1, 2, 2 * LANES, LANES), lambda i, p: (p, 0, 0, 0))],
        out_specs=tok(lambda i, p: (i, 0, p)),
        out_shape=jax.ShapeDtypeStruct((b, s, GROUP_WIDTH), F32),
        scratch_shapes=[pltpu.VMEM((ATTN_TK, ATTN_TQ), F32),
                        pltpu.VMEM((2, 2 * LANES, s), BF16),
                        pltpu.VMEM((2, ATTN_TK, ATTN_TQ), F32),
                        pltpu.VMEM((2, ATTN_TK, ATTN_TQ), F32),
                        pltpu.VMEM((n_q, 2, HEAD_DIM + ONES_ROWS, ATTN_TQ), F32),
                        pltpu.VMEM((n_q, 2, 1, ATTN_TQ), F32)],
        compiler_params=pltpu.CompilerParams(
            dimension_semantics=("arbitrary", "arbitrary"),
            vmem_limit_bytes=VMEM_LIMIT_BYTES),
        name="causal_attention",
    )(tasks, qmt, qat, km, ka, vt, mask)


def _mlp_kernel(x_ref, fox_ref, mla_ref, gfox_ref, gmla_ref, wo_ref, gmlp_ref, wup_ref,
                wdown_ref, gfin_ref, o_ref, *, final):
    mixed = jnp.concatenate([_rms(fox_ref[...], gfox_ref[...]).astype(BF16),
                             _rms(mla_ref[...], gmla_ref[...]).astype(BF16)], axis=-1)
    x1 = x_ref[...] + jnp.dot(mixed, wo_ref[...], preferred_element_type=F32)
    h = _rms(x1, gmlp_ref[...]).astype(BF16)
    y = x1
    for c in range(D_FF // FF_CHUNK):
        u = jnp.dot(h, wup_ref[:, c * FF_CHUNK:(c + 1) * FF_CHUNK], preferred_element_type=F32)
        act = jnp.square(jnp.maximum(u, 0.0)).astype(BF16)
        y = y + jnp.dot(act, wdown_ref[c * FF_CHUNK:(c + 1) * FF_CHUNK, :],
                        preferred_element_type=F32)
    o_ref[...] = _rms(y, gfin_ref[...]) if final else y


def _out_mlp(x2d, fox, mla, w, final):
    n = x2d.shape[0]
    tm = MLP_ROWS
    row = lambda width: pl.BlockSpec((tm, width), lambda i: (i, 0))
    resident = lambda a: pl.BlockSpec(a.shape, lambda i: (0,) * a.ndim,
                                      pipeline_mode=pl.Buffered(1))
    consts = [w["gfox"], w["gmla"], w["wo"], w["gmlp"], w["wup"], w["wdown"], w["gfin"]]
    return pl.pallas_call(
        functools.partial(_mlp_kernel, final=final),
        grid=(n // tm,),
        in_specs=[row(D_MODEL), row(GROUP_WIDTH), row(GROUP_WIDTH)] + [resident(c) for c in consts],
        out_specs=row(D_MODEL),
        out_shape=jax.ShapeDtypeStruct((n, D_MODEL), F32),
        compiler_params=pltpu.CompilerParams(
            dimension_semantics=("arbitrary",),
            vmem_limit_bytes=VMEM_LIMIT_BYTES),
        name="out_proj_mlp",
    )(x2d, fox, mla, *consts)


def _rope_split_cols(wr):
    k = wr.shape[0]
    heads = wr.shape[1]
    per_group = LANES // ROPE_DIM
    first = wr[:, :, :ROPE_HALF].reshape(k, heads // per_group, per_group * ROPE_HALF)
    second = wr[:, :, ROPE_HALF:].reshape(k, heads // per_group, per_group * ROPE_HALF)
    return jnp.concatenate([first, second], axis=-1).reshape(k, heads * ROPE_DIM)


def _prep_weights(attn_norm_g, w_in, b_forget, q_norm_g, w_uq, kv_norm_g, w_ukv, fox_out_g,
                  mla_out_g, w_o, mlp_norm_g, w_up, w_down, final_norm_g):
    row = lambda v: v.reshape(1, -1).astype(F32)
    pad_lanes = lambda a: jnp.pad(a, ((0, 0), (0, LANES - a.shape[1])))
    wkr = w_in[:, OFF_KR:IN_COLS].reshape(D_MODEL, 1, ROPE_DIM)
    wkr = _rope_split_cols(jnp.broadcast_to(wkr, (D_MODEL, LANES // ROPE_DIM, ROPE_DIM)))
    uq = w_uq.reshape(Q_RANK, N_HEADS, MLA_QK_DIM)
    ukv = w_ukv.reshape(KV_RANK, N_HEADS, 2 * HEAD_DIM)
    tri = np.tril(np.ones((SCAN_CHUNK, SCAN_CHUNK), np.float32))
    w_rows = jnp.concatenate([w_in[:, OFF_FQ:OFF_FK], w_in[:, OFF_FV:OFF_FF], wkr], axis=1)
    w_cols = jnp.concatenate([w_in[:, OFF_FK:OFF_FV], w_in[:, OFF_CQ:OFF_CKV],
                              w_in[:, OFF_CKV:OFF_KR], pad_lanes(w_in[:, OFF_FF:OFF_CQ])], axis=1)
    wuq = jnp.concatenate([uq[:, :, :HEAD_DIM].reshape(Q_RANK, GROUP_WIDTH),
                           _rope_split_cols(uq[:, :, HEAD_DIM:])], axis=1)
    return {
        "g_attn": row(attn_norm_g),
        "w_rows": w_rows.T.astype(BF16),
        "w_cols": w_cols.astype(BF16),
        "bff": pad_lanes(row(b_forget)),
        "gq": row(q_norm_g),
        "gkv": row(kv_norm_g),
        "wuq": wuq.T.astype(BF16),
        "wuk": ukv[:, :, :HEAD_DIM].reshape(KV_RANK, GROUP_WIDTH).astype(BF16),
        "wuvt": ukv[:, :, HEAD_DIM:].reshape(KV_RANK, GROUP_WIDTH).T.astype(BF16),
        "tri": jnp.asarray(tri, BF16),
        "gfox": row(fox_out_g),
        "gmla": row(mla_out_g),
        "wo": w_o.astype(BF16),
        "gmlp": row(mlp_norm_g),
        "wup": w_up.astype(BF16),
        "wdown": w_down.astype(BF16),
        "gfin": row(final_norm_g),
    }


def _head_masks():
    pairs = N_HEADS // 2
    fox = np.zeros((pairs, 2, 2 * LANES), np.float32)
    mla = np.zeros((pairs, 2, 2 * LANES), np.float32)
    per_group = LANES // ROPE_DIM
    for p in range(pairs):
        for a in range(2):
            h = 2 * p + a
            fox[p, a, a * HEAD_DIM:(a + 1) * HEAD_DIM] = 1.0
            mla[p, a, a * HEAD_DIM:(a + 1) * HEAD_DIM] = 1.0
            for piece in range(3):
                fox[p, a, LANES + piece * N_HEADS + h] = 1.0
            hh = h % per_group
            for half in range(2):
                lo = LANES + half * (LANES // 2) + hh * ROPE_HALF
                mla[p, a, lo:lo + ROPE_HALF] = 1.0
    rep = lambda m: jnp.asarray(np.repeat(m[..., None], LANES, axis=-1), BF16)
    return rep(fox), rep(mla)


def kernel(x, positions, attn_norm_g, w_in, b_forget, q_norm_g, w_uq, kv_norm_g, w_ukv,
           fox_out_g, mla_out_g, w_o, mlp_norm_g, w_up, w_down, final_norm_g):
    b, s, d = x.shape
    depth = w_in.shape[0]
    inv_freq = ROPE_THETA ** (-jnp.arange(0, ROPE_DIM, 2, dtype=F32) / ROPE_DIM)
    cos_t, sin_t = _rope_tables(positions, inv_freq)
    fox_mask, mla_mask = _head_masks()
    ones_aux = jnp.ones((1, LANES, s), BF16)

    for l in range(depth):
        w = _prep_weights(attn_norm_g[l], w_in[l], b_forget[l], q_norm_g[l], w_uq[l],
                          kv_norm_g[l], w_ukv[l], fox_out_g[l], mla_out_g[l], w_o[l],
                          mlp_norm_g[l], w_up[l], w_down[l], final_norm_g)
        fqt, fk, fvt, faux, mqnt, mqrt, mkn, mkr, mvt = _projection(x, cos_t, sin_t, w)
        fox = _attention(fqt, ones_aux, fk, faux, fvt, fox_mask, False, 1)
        mla = _attention(mqnt, mqrt, mkn, mkr, mvt, mla_mask, True, 2)
        y = _out_mlp(x.reshape(b * s, d), fox.reshape(b * s, GROUP_WIDTH),
                     mla.reshape(b * s, GROUP_WIDTH), w, l == depth - 1)
        x = y.reshape(b, s, d)
    return x
```

```python
import functools
import math

import numpy as np
import jax
import jax.numpy as jnp
from jax import lax
from jax.experimental import pallas as pl
from jax.experimental.pallas import tpu as pltpu

D_MODEL = 1024
HEAD_DIM = 64
N_HEADS = 8
GROUP_WIDTH = N_HEADS * HEAD_DIM
ROPE_DIM = 32
ROPE_HALF = ROPE_DIM // 2
MLA_QK_DIM = HEAD_DIM + ROPE_DIM
Q_RANK = 384
KV_RANK = 256
ROPE_THETA = 10000.0
EPS = 1e-6

OFF_FQ = 0
OFF_FK = OFF_FQ + GROUP_WIDTH
OFF_FV = OFF_FK + GROUP_WIDTH
OFF_FF = OFF_FV + GROUP_WIDTH
OFF_CQ = OFF_FF + N_HEADS
OFF_CKV = OFF_CQ + Q_RANK
OFF_KR = OFF_CKV + KV_RANK
IN_COLS = OFF_KR + ROPE_DIM

LANES = 128
VMEM_LIMIT_BYTES = 56 * 1024 * 1024

LOG2E = math.log2(math.e)
FOX_QSCALE = LOG2E / math.sqrt(HEAD_DIM)
MLA_QSCALE = LOG2E / math.sqrt(MLA_QK_DIM)

PROJ_ROWS = 1024
ATTN_TQ = 512
ATTN_TK = 512
MLP_ROWS = 512
SCAN_CHUNK = 128
ONES_ROWS = 16
UNROLL_STEPS = 14
FINALIZE_UNROLL = 4

F32 = jnp.float32
BF16 = jnp.bfloat16
NT_DIMS = (((1,), (1,)), ((), ()))


def _rms(x, g):
    return x * lax.rsqrt(jnp.mean(x * x, axis=-1, keepdims=True) + EPS) * g


def _split3(x):
    hi = x.astype(BF16).astype(F32)
    mid = (x - hi).astype(BF16).astype(F32)
    lo = (x - hi - mid).astype(BF16).astype(F32)
    return hi, mid, lo


def _pack3(x):
    hi, mid, lo = _split3(x)
    packed = hi + pltpu.roll(mid, N_HEADS, axis=1) + pltpu.roll(lo, 2 * N_HEADS, axis=1)
    return packed.astype(BF16)


def _proj_kernel(x_ref, pos_ref, freq_ref, g_attn_ref, w_rows_ref, w_cols_ref, bff_ref, gq_ref,
                 gkv_ref, wuq_ref, wuk_ref, wuvt_ref, tri_ref,
                 fqt_ref, fk_ref, fvt_ref, faux_ref, mqnt_ref, mqrt_ref, mkn_ref, mkr_ref, mvt_ref,
                 h_sc, carry_sc):
    rows = x_ref.shape[1]

    @pl.when(pl.program_id(1) == 0)
    def _():
        carry_sc[...] = jnp.zeros_like(carry_sc)

    h_sc[...] = _rms(x_ref[0], g_attn_ref[...]).astype(BF16)
    h = h_sc[...]

    groups = LANES // ROPE_HALF
    row = lax.broadcasted_iota(jnp.int32, (LANES, rows), 0)
    angle = pos_ref[0].astype(F32) * freq_ref[...]
    cos_t = jnp.tile(jnp.cos(angle), (groups, 1))
    sin_t = jnp.tile(jnp.sin(angle), (groups, 1))
    sin_s = jnp.where(row < LANES // 2, -sin_t, sin_t)

    def rope_t(xt):
        return xt * cos_t + pltpu.roll(xt, LANES // 2, axis=0) * sin_s


    tk = jnp.dot(h, w_cols_ref[...], preferred_element_type=F32)
    o_cq, o_ckv, o_ff = GROUP_WIDTH, GROUP_WIDTH + Q_RANK, GROUP_WIDTH + Q_RANK + KV_RANK
    fk_ref[0] = tk[:, 0:o_cq].astype(BF16)

    ft = lax.dot_general(w_rows_ref[...], h, NT_DIMS, preferred_element_type=F32)
    fqt_ref[0] = (ft[0:GROUP_WIDTH] * FOX_QSCALE).astype(BF16)
    fvt_ref[0] = ft[GROUP_WIDTH:2 * GROUP_WIDTH].astype(BF16)
    mkr_ref[0] = rope_t(ft[2 * GROUP_WIDTH:2 * GROUP_WIDTH + LANES]).T.astype(BF16)

    lane = lax.broadcasted_iota(jnp.int32, (1, LANES), 1)
    head_lane = lane < N_HEADS
    z = tk[:, o_ff:o_ff + LANES] + bff_ref[...]
    log_f = jnp.where(head_lane, jax.nn.log_sigmoid(z), 0.0)
    tri = tri_ref[...]
    carry = carry_sc[0:1, :]
    for c in range(rows // SCAN_CHUNK):
        part = jnp.dot(tri, _pack3(log_f[c * SCAN_CHUNK:(c + 1) * SCAN_CHUNK, :]),
                       preferred_element_type=F32)
        part = (part + pltpu.roll(part, LANES - N_HEADS, axis=1)
                + pltpu.roll(part, LANES - 2 * N_HEADS, axis=1))
        cum = jnp.where(head_lane, part, 0.0) + carry
        carry = cum[SCAN_CHUNK - 1:SCAN_CHUNK, :]
        faux_ref[0, c * SCAN_CHUNK:(c + 1) * SCAN_CHUNK, :] = _pack3(cum * (-LOG2E))
    carry_sc[...] = jnp.broadcast_to(carry, carry_sc.shape)

    cq = _rms(tk[:, o_cq:o_ckv], gq_ref[...]).astype(BF16)
    qt = lax.dot_general(wuq_ref[...], cq, NT_DIMS, preferred_element_type=F32)
    mqnt_ref[0] = (qt[0:GROUP_WIDTH] * MLA_QSCALE).astype(BF16)
    for g in range(2 * N_HEADS * ROPE_HALF // LANES):
        lo = GROUP_WIDTH + g * LANES
        mqrt_ref[0, g * LANES:(g + 1) * LANES, :] = (
            rope_t(qt[lo:lo + LANES]) * MLA_QSCALE).astype(BF16)

    ckv = _rms(tk[:, o_ckv:o_ff], gkv_ref[...]).astype(BF16)
    mkn_ref[0] = jnp.dot(ckv, wuk_ref[...], preferred_element_type=F32).astype(BF16)
    mvt_ref[0] = lax.dot_general(wuvt_ref[...], ckv, NT_DIMS,
                                 preferred_element_type=F32).astype(BF16)


def _const_spec(shape):
    return pl.BlockSpec(shape, lambda *_: (0,) * len(shape))


def _projection(x, positions, inv_freq, w):
    b, s, _ = x.shape
    tm = PROJ_ROWS
    row_spec = lambda width: pl.BlockSpec((1, tm, width), lambda i, j: (i, j, 0))
    col_spec = lambda height: pl.BlockSpec((1, height, tm), lambda i, j: (i, 0, j))
    consts = [w["g_attn"], w["w_rows"], w["w_cols"], w["bff"], w["gq"], w["gkv"], w["wuq"],
              w["wuk"], w["wuvt"], w["tri"]]
    row_out = lambda width: jax.ShapeDtypeStruct((b, s, width), BF16)
    col_out = lambda height: jax.ShapeDtypeStruct((b, height, s), BF16)
    return pl.pallas_call(
        _proj_kernel,
        grid=(b, s // tm),
        in_specs=[row_spec(D_MODEL), col_spec(1), _const_spec((ROPE_HALF, 1))]
                 + [_const_spec(c.shape) for c in consts],
        out_specs=[col_spec(GROUP_WIDTH), row_spec(GROUP_WIDTH), col_spec(GROUP_WIDTH),
                   row_spec(LANES), col_spec(GROUP_WIDTH), col_spec(2 * LANES),
                   row_spec(GROUP_WIDTH), row_spec(LANES), col_spec(GROUP_WIDTH)],
        out_shape=[col_out(GROUP_WIDTH), row_out(GROUP_WIDTH), col_out(GROUP_WIDTH),
                   row_out(LANES), col_out(GROUP_WIDTH), col_out(2 * LANES),
                   row_out(GROUP_WIDTH), row_out(LANES), col_out(GROUP_WIDTH)],
        scratch_shapes=[pltpu.VMEM((tm, D_MODEL), BF16), pltpu.VMEM((8, LANES), F32)],
        compiler_params=pltpu.CompilerParams(
            dimension_semantics=("arbitrary", "arbitrary"),
            vmem_limit_bytes=VMEM_LIMIT_BYTES),
        name="input_projection",
    )(x, positions.reshape(b, 1, s), inv_freq.reshape(ROPE_HALF, 1), *consts)


def _attn_kernel(tasks_ref, qm_ref, qa_ref, km_ref, ka_ref, vt_ref, mask_ref, o_ref,
                 bias_sc, qh_sc, s0_sc, s1_sc, acc_sc, m_sc):
    s_len = km_ref.shape[1]
    tq, tk = ATTN_TQ, ATTN_TK

    @pl.when((pl.program_id(0) == 0) & (pl.program_id(1) == 0))
    def _():
        key_idx = lax.broadcasted_iota(jnp.int32, (tk, tq), 0)
        qry_idx = lax.broadcasted_iota(jnp.int32, (tk, tq), 1)
        bias_sc[...] = jnp.where(key_idx <= qry_idx, 0.0, -jnp.inf)

    def kv_tile(k0):
        return jnp.concatenate([km_ref[0, pl.ds(k0, tk), :], ka_ref[0, pl.ds(k0, tk), :]], axis=-1)

    ones_rows = jnp.ones((ONES_ROWS, tk), BF16)

    qcat = jnp.concatenate([qm_ref[0], qa_ref[0]], axis=0)
    for a in range(2):
        qh_sc[a] = qcat * jnp.tile(mask_ref[0, a], (1, s_len // LANES))
    last_tile = s_len // tq - 1
    acc_sc[last_tile] = jnp.zeros(acc_sc.shape[1:], F32)
    m_sc[last_tile] = jnp.full(m_sc.shape[1:], -jnp.inf, F32)

    def produce(dst, t, diagonal):
        q0 = pl.multiple_of(tasks_ref[0, t] * tq, tq)
        k0 = pl.multiple_of(tasks_ref[1, t] * tk, tk)
        q_pair = jnp.concatenate([qh_sc[a, :, pl.ds(q0, tq)] for a in range(2)], axis=1)
        sc = jnp.dot(kv_tile(k0), q_pair, preferred_element_type=F32)
        maxima = []
        for a in range(2):
            sc_a = sc[:, a * tq:(a + 1) * tq]
            if diagonal:
                sc_a = sc_a + bias_sc[...]
            dst[a] = sc_a
            maxima.append(jnp.max(sc_a, axis=0, keepdims=True))
        return tuple(maxima)

    def consume(src, t, maxima):
        i = tasks_ref[0, t]
        k0 = pl.multiple_of(tasks_ref[1, t] * tk, tk)
        alphas, ps = [], []
        for a in range(2):
            m_run = m_sc[i, a]
            m_new = jnp.maximum(m_run, maxima[a])
            alphas.append(jnp.exp2(m_run - m_new))
            ps.append(jnp.exp2(src[a] - m_new).astype(BF16))
            m_sc[i, a] = m_new
        vt = jnp.concatenate([vt_ref[0, :, pl.ds(k0, tk)], ones_rows], axis=0)
        pv = jnp.dot(vt, jnp.concatenate(ps, axis=1), preferred_element_type=F32)
        for a in range(2):
            pv_a = jnp.concatenate([pv[a * HEAD_DIM:(a + 1) * HEAD_DIM, a * tq:(a + 1) * tq],
                                    pv[2 * HEAD_DIM:, a * tq:(a + 1) * tq]], axis=0)
            acc_sc[i, a] = alphas[a] * acc_sc[i, a] + pv_a

    half = tq // 2

    def produce_diag(dst, i):
        q0 = i * tq
        kcat = kv_tile(q0)
        q_all = jnp.concatenate([qh_sc[a, :, q0:q0 + tq] for a in range(2)], axis=1)
        q_late = jnp.concatenate([qh_sc[a, :, q0 + half:q0 + tq] for a in range(2)], axis=1)
        top = jnp.dot(kcat[0:half], q_all, preferred_element_type=F32)
        bot = jnp.dot(kcat[half:tk], q_late, preferred_element_type=F32)
        maxima = []
        for a in range(2):
            top_a = top[:, a * tq:(a + 1) * tq] + bias_sc[0:half, :]
            bot_a = bot[:, a * half:(a + 1) * half] + bias_sc[half:tk, half:tq]
            dst[a, 0:half, :] = top_a
            dst[a, half:tk, half:tq] = bot_a
            mx_top = jnp.max(top_a, axis=0, keepdims=True)
            mx_bot = jnp.max(bot_a, axis=0, keepdims=True)
            maxima.append(jnp.concatenate(
                [mx_top[:, 0:half], jnp.maximum(mx_top[:, half:tq], mx_bot)], axis=1))
        return tuple(maxima)

    def consume_diag(src, i, maxima):
        q0 = i * tq
        early, late = [], []
        for a in range(2):
            p_top = jnp.exp2(src[a, 0:half, :] - maxima[a]).astype(BF16)
            p_bot = jnp.exp2(src[a, half:tk, half:tq] - maxima[a][:, half:tq]).astype(BF16)
            early.append(p_top[:, 0:half])
            late.append(jnp.concatenate([p_top[:, half:tq], p_bot], axis=0))
        vt = jnp.concatenate([vt_ref[0, :, q0:q0 + tk], ones_rows], axis=0)
        pv_early = jnp.dot(vt[:, 0:half], jnp.concatenate(early, axis=1),
                           preferred_element_type=F32)
        pv_late = jnp.dot(vt, jnp.concatenate(late, axis=1), preferred_element_type=F32)
        for a in range(2):
            pv_a = jnp.concatenate([pv_early[:, a * half:(a + 1) * half],
                                    pv_late[:, a * half:(a + 1) * half]], axis=1)
            acc_sc[i, a] = jnp.concatenate(
                [pv_a[a * HEAD_DIM:(a + 1) * HEAD_DIM], pv_a[2 * HEAD_DIM:]], axis=0)
            m_sc[i, a] = maxima[a]

    bufs = (s0_sc, s1_sc)

    def step(t, parity, diagonal, mx):
        mx_next = produce(bufs[1 - parity], t + 1, diagonal)
        consume(bufs[parity], t, mx)
        return mx_next

    def steps(first, count, diagonal):
        def body(it, mx):
            t = first + it * count
            for u in range(count):
                mx = step(t + u, (first + u) % 2, diagonal, mx)
            return mx
        return body

    n_diag = s_len // tq
    n_tasks = n_diag * (n_diag + 1) // 2
    n_off = n_tasks - n_diag
    assert n_diag % 2 == 0 and n_off % UNROLL_STEPS == 0 and UNROLL_STEPS % 2 == 0
    assert tq == tk
    mx = produce_diag(s0_sc, 0)
    for t in range(n_diag - 1):
        if t + 1 < n_diag - 1:
            mx_next = produce_diag(bufs[(t + 1) % 2], t + 1)
        else:
            mx_next = produce(bufs[(t + 1) % 2], t + 1, True)
        consume_diag(bufs[t % 2], t, mx)
        mx = mx_next
    mx = lax.fori_loop(0, n_off // UNROLL_STEPS, steps(n_diag - 1, UNROLL_STEPS, False), mx)
    consume(bufs[(n_tasks - 1) % 2], n_tasks - 1, mx)

    def finalize(it, _):
        for u in range(FINALIZE_UNROLL):
            i = it * FINALIZE_UNROLL + u
            q0 = pl.multiple_of(i * tq, tq)
            out_t = jnp.concatenate(
                [acc_sc[i, a, 0:HEAD_DIM, :] / acc_sc[i, a, HEAD_DIM:HEAD_DIM + 1, :]
                 for a in range(2)], axis=0)
            o_ref[0, pl.ds(q0, tq), :] = out_t.T
        return 0

    assert n_diag % FINALIZE_UNROLL == 0
    lax.fori_loop(0, n_diag // FINALIZE_UNROLL, finalize, 0)


def _attention(qmt, qat, km, ka, vt, mask, qa_batched, qa_per_pairs):
    b, s, _ = km.shape
    pairs = N_HEADS // 2
    tok = lambda f: pl.BlockSpec((1, s, LANES), f)
    feat = lambda f: pl.BlockSpec((1, LANES, s), f)
    qa_map = (lambda i, p: (i, p // qa_per_pairs, 0)) if qa_batched else (lambda i, p: (0, 0, 0))
    n_q = s // ATTN_TQ
    task_list = [(i, i) for i in range(n_q)] + [(i, j) for i in range(n_q) for j in range(i)]
    tasks = jnp.asarray(np.array(task_list, np.int32).T)
    return pl.pallas_call(
        _attn_kernel,
        grid=(b, pairs),
        in_specs=[pl.BlockSpec(memory_space=pltpu.SMEM),
                  feat(lambda i, p: (i, p, 0)), feat(qa_map), tok(lambda i, p: (i, 0, p)),
                  tok(lambda i, p: (i, 0, 0)), feat(lambda i, p: (i, p, 0)),
                  pl.BlockSpec((1, 2, 2 * LANES, LANES), lambda i, p: (p, 0, 0, 0))],
        out_specs=tok(lambda i, p: (i, 0, p)),
        out_shape=jax.ShapeDtypeStruct((b, s, GROUP_WIDTH), F32),
        scratch_shapes=[pltpu.VMEM((ATTN_TK, ATTN_TQ), F32),
                        pltpu.VMEM((2, 2 * LANES, s), BF16),
                        pltpu.VMEM((2, ATTN_TK, ATTN_TQ), F32),
                        pltpu.VMEM((2, ATTN_TK, ATTN_TQ), F32),
                        pltpu.VMEM((n_q, 2, HEAD_DIM + ONES_ROWS, ATTN_TQ), F32),
                        pltpu.VMEM((n_q, 2, 1, ATTN_TQ), F32)],
        compiler_params=pltpu.CompilerParams(
            dimension_semantics=("arbitrary", "arbitrary"),
            vmem_limit_bytes=VMEM_LIMIT_BYTES),
        name="causal_attention",
    )(tasks, qmt, qat, km, ka, vt, mask)


def _mlp_kernel(x_ref, fox_ref, mla_ref, gfox_ref, gmla_ref, wo_ref, gmlp_ref, wup_ref,
                wdown_ref, gfin_ref, o_ref, *, final):
    mixed = jnp.concatenate([_rms(fox_ref[...], gfox_ref[...]).astype(BF16),
                             _rms(mla_ref[...], gmla_ref[...]).astype(BF16)], axis=-1)
    x1 = x_ref[...] + jnp.dot(mixed, wo_ref[...], preferred_element_type=F32)
    h = _rms(x1, gmlp_ref[...]).astype(BF16)
    u = jnp.dot(h, wup_ref[...], preferred_element_type=F32)
    act = jnp.square(jnp.maximum(u, 0.0)).astype(BF16)
    y = x1 + jnp.dot(act, wdown_ref[...], preferred_element_type=F32)
    o_ref[...] = _rms(y, gfin_ref[...]) if final else y


def _out_mlp(x2d, fox, mla, w, final):
    n = x2d.shape[0]
    tm = MLP_ROWS
    row = lambda width: pl.BlockSpec((tm, width), lambda i: (i, 0))
    resident = lambda a: pl.BlockSpec(a.shape, lambda i: (0,) * a.ndim,
                                      pipeline_mode=pl.Buffered(1))
    consts = [w["gfox"], w["gmla"], w["wo"], w["gmlp"], w["wup"], w["wdown"], w["gfin"]]
    return pl.pallas_call(
        functools.partial(_mlp_kernel, final=final),
        grid=(n // tm,),
        in_specs=[row(D_MODEL), row(GROUP_WIDTH), row(GROUP_WIDTH)] + [resident(c) for c in consts],
        out_specs=row(D_MODEL),
        out_shape=jax.ShapeDtypeStruct((n, D_MODEL), F32),
        compiler_params=pltpu.CompilerParams(
            dimension_semantics=("arbitrary",),
            vmem_limit_bytes=VMEM_LIMIT_BYTES),
        name="out_proj_mlp",
    )(x2d, fox, mla, *consts)


def _rope_split_cols(wr):
    k = wr.shape[0]
    heads = wr.shape[1]
    per_group = LANES // ROPE_DIM
    first = wr[:, :, :ROPE_HALF].reshape(k, heads // per_group, per_group * ROPE_HALF)
    second = wr[:, :, ROPE_HALF:].reshape(k, heads // per_group, per_group * ROPE_HALF)
    return jnp.concatenate([first, second], axis=-1).reshape(k, heads * ROPE_DIM)


def _prep_weights(attn_norm_g, w_in, b_forget, q_norm_g, w_uq, kv_norm_g, w_ukv, fox_out_g,
                  mla_out_g, w_o, mlp_norm_g, w_up, w_down, final_norm_g):
    row = lambda v: v.reshape(1, -1).astype(F32)
    pad_lanes = lambda a: jnp.pad(a, ((0, 0), (0, LANES - a.shape[1])))
    wkr = w_in[:, OFF_KR:IN_COLS].reshape(D_MODEL, 1, ROPE_DIM)
    wkr = _rope_split_cols(jnp.broadcast_to(wkr, (D_MODEL, LANES // ROPE_DIM, ROPE_DIM)))
    uq = w_uq.reshape(Q_RANK, N_HEADS, MLA_QK_DIM)
    ukv = w_ukv.reshape(KV_RANK, N_HEADS, 2 * HEAD_DIM)
    tri = np.tril(np.ones((SCAN_CHUNK, SCAN_CHUNK), np.float32))
    w_rows = jnp.concatenate([w_in[:, OFF_FQ:OFF_FK], w_in[:, OFF_FV:OFF_FF], wkr], axis=1)
    w_cols = jnp.concatenate([w_in[:, OFF_FK:OFF_FV], w_in[:, OFF_CQ:OFF_CKV],
                              w_in[:, OFF_CKV:OFF_KR], pad_lanes(w_in[:, OFF_FF:OFF_CQ])], axis=1)
    wuq = jnp.concatenate([uq[:, :, :HEAD_DIM].reshape(Q_RANK, GROUP_WIDTH),
                           _rope_split_cols(uq[:, :, HEAD_DIM:])], axis=1)
    return {
        "g_attn": row(attn_norm_g),
        "w_rows": w_rows.T.astype(BF16),
        "w_cols": w_cols.astype(BF16),
        "bff": pad_lanes(row(b_forget)),
        "gq": row(q_norm_g),
        "gkv": row(kv_norm_g),
        "wuq": wuq.T.astype(BF16),
        "wuk": ukv[:, :, :HEAD_DIM].reshape(KV_RANK, GROUP_WIDTH).astype(BF16),
        "wuvt": ukv[:, :, HEAD_DIM:].reshape(KV_RANK, GROUP_WIDTH).T.astype(BF16),
        "tri": jnp.asarray(tri, BF16),
        "gfox": row(fox_out_g),
        "gmla": row(mla_out_g),
        "wo": w_o.astype(BF16),
        "gmlp": row(mlp_norm_g),
        "wup": w_up.astype(BF16),
        "wdown": w_down.astype(BF16),
        "gfin": row(final_norm_g),
    }


def _head_masks():
    pairs = N_HEADS // 2
    fox = np.zeros((pairs, 2, 2 * LANES), np.float32)
    mla = np.zeros((pairs, 2, 2 * LANES), np.float32)
    per_group = LANES // ROPE_DIM
    for p in range(pairs):
        for a in range(2):
            h = 2 * p + a
            fox[p, a, a * HEAD_DIM:(a + 1) * HEAD_DIM] = 1.0
            mla[p, a, a * HEAD_DIM:(a + 1) * HEAD_DIM] = 1.0
            for piece in range(3):
                fox[p, a, LANES + piece * N_HEADS + h] = 1.0
            hh = h % per_group
            for half in range(2):
                lo = LANES + half * (LANES // 2) + hh * ROPE_HALF
                mla[p, a, lo:lo + ROPE_HALF] = 1.0
    rep = lambda m: jnp.asarray(np.repeat(m[..., None], LANES, axis=-1), BF16)
    return rep(fox), rep(mla)


def kernel(x, positions, attn_norm_g, w_in, b_forget, q_norm_g, w_uq, kv_norm_g, w_ukv,
           fox_out_g, mla_out_g, w_o, mlp_norm_g, w_up, w_down, final_norm_g):
    b, s, d = x.shape
    depth = w_in.shape[0]
    inv_freq = ROPE_THETA ** (-jnp.arange(0, ROPE_DIM, 2, dtype=F32) / ROPE_DIM)
    fox_mask, mla_mask = _head_masks()
    ones_aux = jnp.ones((1, LANES, s), BF16)

    for l in range(depth):
        w = _prep_weights(attn_norm_g[l], w_in[l], b_forget[l], q_norm_g[l], w_uq[l],
                          kv_norm_g[l], w_ukv[l], fox_out_g[l], mla_out_g[l], w_o[l],
                          mlp_norm_g[l], w_up[l], w_down[l], final_norm_g)
        fqt, fk, fvt, faux, mqnt, mqrt, mkn, mkr, mvt = _projection(x, positions, inv_freq, w)
        fox = _attention(fqt, ones_aux, fk, faux, fvt, fox_mask, False, 1)
        mla = _attention(mqnt, mqrt, mkn, mkr, mvt, mla_mask, True, 2)
        y = _out_mlp(x.reshape(b * s, d), fox.reshape(b * s, GROUP_WIDTH),
                     mla.reshape(b * s, GROUP_WIDTH), w, l == depth - 1)
        x = y.reshape(b, s, d)
    return x
```

```python
import functools
import math

import numpy as np
import jax
import jax.numpy as jnp
from jax import lax
from jax.experimental import pallas as pl
from jax.experimental.pallas import tpu as pltpu

D_MODEL = 1024
HEAD_DIM = 64
N_HEADS = 8
GROUP_WIDTH = N_HEADS * HEAD_DIM
ROPE_DIM = 32
ROPE_HALF = ROPE_DIM // 2
MLA_QK_DIM = HEAD_DIM + ROPE_DIM
Q_RANK = 384
KV_RANK = 256
ROPE_THETA = 10000.0
EPS = 1e-6

OFF_FQ = 0
OFF_FK = OFF_FQ + GROUP_WIDTH
OFF_FV = OFF_FK + GROUP_WIDTH
OFF_FF = OFF_FV + GROUP_WIDTH
OFF_CQ = OFF_FF + N_HEADS
OFF_CKV = OFF_CQ + Q_RANK
OFF_KR = OFF_CKV + KV_RANK
IN_COLS = OFF_KR + ROPE_DIM

LANES = 128
VMEM_LIMIT_BYTES = 56 * 1024 * 1024

LOG2E = math.log2(math.e)
FOX_QSCALE = LOG2E / math.sqrt(HEAD_DIM)
MLA_QSCALE = LOG2E / math.sqrt(MLA_QK_DIM)

PROJ_ROWS = 1024
ATTN_TQ = 512
ATTN_TK = 512
MLP_ROWS = 512
SCAN_CHUNK = 128
ONES_ROWS = 16
UNROLL_STEPS = 14
FINALIZE_UNROLL = 4

F32 = jnp.float32
BF16 = jnp.bfloat16
NT_DIMS = (((1,), (1,)), ((), ()))


def _rms(x, g):
    return x * lax.rsqrt(jnp.mean(x * x, axis=-1, keepdims=True) + EPS) * g


def _split3(x):
    hi = x.astype(BF16).astype(F32)
    mid = (x - hi).astype(BF16).astype(F32)
    lo = (x - hi - mid).astype(BF16).astype(F32)
    return hi, mid, lo


def _pack3(x):
    hi, mid, lo = _split3(x)
    packed = hi + pltpu.roll(mid, N_HEADS, axis=1) + pltpu.roll(lo, 2 * N_HEADS, axis=1)
    return packed.astype(BF16)


def _rope_table_kernel(pos_ref, freq_ref, cos_ref, sin_ref):
    ang = pos_ref[0].astype(F32) * freq_ref[...]
    cos_ref[0] = jnp.cos(ang)
    sin_ref[0] = jnp.sin(ang)


def _rope_tables(positions, inv_freq):
    b, s = positions.shape
    table = pl.BlockSpec((1, ROPE_HALF, s), lambda i: (i, 0, 0))
    return pl.pallas_call(
        _rope_table_kernel,
        grid=(b,),
        in_specs=[pl.BlockSpec((1, 1, s), lambda i: (i, 0, 0)),
                  pl.BlockSpec((ROPE_HALF, 1), lambda i: (0, 0))],
        out_specs=[table, table],
        out_shape=[jax.ShapeDtypeStruct((b, ROPE_HALF, s), F32)] * 2,
        name="rope_tables",
    )(positions.reshape(b, 1, s), inv_freq.reshape(ROPE_HALF, 1))


def _proj_kernel(x_ref, cos_ref, sin_ref, g_attn_ref, w_rows_ref, w_cols_ref, bff_ref, gq_ref,
                 gkv_ref, wuq_ref, wuk_ref, wuvt_ref, tri_ref,
                 fqt_ref, fk_ref, fvt_ref, faux_ref, mqnt_ref, mqrt_ref, mkn_ref, mkr_ref, mvt_ref,
                 h_sc, carry_sc):
    rows = x_ref.shape[1]

    @pl.when(pl.program_id(1) == 0)
    def _():
        carry_sc[...] = jnp.zeros_like(carry_sc)

    h_sc[...] = _rms(x_ref[0], g_attn_ref[...]).astype(BF16)
    h = h_sc[...]

    groups = LANES // ROPE_HALF
    row = lax.broadcasted_iota(jnp.int32, (LANES, rows), 0)
    cos_t = jnp.tile(cos_ref[0], (groups, 1))
    sin_t = jnp.tile(sin_ref[0], (groups, 1))
    sin_s = jnp.where(row < LANES // 2, -sin_t, sin_t)

    def rope_t(xt):
        return xt * cos_t + pltpu.roll(xt, LANES // 2, axis=0) * sin_s


    tk = jnp.dot(h, w_cols_ref[...], preferred_element_type=F32)
    o_cq, o_ckv, o_ff = GROUP_WIDTH, GROUP_WIDTH + Q_RANK, GROUP_WIDTH + Q_RANK + KV_RANK
    fk_ref[0] = tk[:, 0:o_cq].astype(BF16)

    ft = lax.dot_general(w_rows_ref[...], h, NT_DIMS, preferred_element_type=F32)
    fqt_ref[0] = (ft[0:GROUP_WIDTH] * FOX_QSCALE).astype(BF16)
    fvt_ref[0] = ft[GROUP_WIDTH:2 * GROUP_WIDTH].astype(BF16)
    mkr_ref[0] = rope_t(ft[2 * GROUP_WIDTH:2 * GROUP_WIDTH + LANES]).T.astype(BF16)

    lane = lax.broadcasted_iota(jnp.int32, (1, LANES), 1)
    head_lane = lane < N_HEADS
    z = tk[:, o_ff:o_ff + LANES] + bff_ref[...]
    log_f = jnp.where(head_lane, jax.nn.log_sigmoid(z), 0.0)
    n_chunks = rows // SCAN_CHUNK
    packed = jnp.concatenate(
        [_pack3(log_f[c * SCAN_CHUNK:(c + 1) * SCAN_CHUNK, :]) for c in range(n_chunks)], axis=1)
    parts = jnp.dot(tri_ref[...], packed, preferred_element_type=F32)
    carry = carry_sc[0:1, :]
    for c in range(n_chunks):
        part = parts[:, c * LANES:(c + 1) * LANES]
        part = (part + pltpu.roll(part, LANES - N_HEADS, axis=1)
                + pltpu.roll(part, LANES - 2 * N_HEADS, axis=1))
        cum = jnp.where(head_lane, part, 0.0) + carry
        carry = cum[SCAN_CHUNK - 1:SCAN_CHUNK, :]
        faux_ref[0, c * SCAN_CHUNK:(c + 1) * SCAN_CHUNK, :] = _pack3(cum * (-LOG2E))
    carry_sc[...] = jnp.broadcast_to(carry, carry_sc.shape)

    cq = _rms(tk[:, o_cq:o_ckv], gq_ref[...]).astype(BF16)
    qt = lax.dot_general(wuq_ref[...], cq, NT_DIMS, preferred_element_type=F32)
    mqnt_ref[0] = (qt[0:GROUP_WIDTH] * MLA_QSCALE).astype(BF16)
    for g in range(2 * N_HEADS * ROPE_HALF // LANES):
        lo = GROUP_WIDTH + g * LANES
        mqrt_ref[0, g * LANES:(g + 1) * LANES, :] = (
            rope_t(qt[lo:lo + LANES]) * MLA_QSCALE).astype(BF16)

    ckv = _rms(tk[:, o_ckv:o_ff], gkv_ref[...]).astype(BF16)
    mkn_ref[0] = jnp.dot(ckv, wuk_ref[...], preferred_element_type=F32).astype(BF16)
    mvt_ref[0] = lax.dot_general(wuvt_ref[...], ckv, NT_DIMS,
                                 preferred_element_type=F32).astype(BF16)


def _const_spec(shape):
    return pl.BlockSpec(shape, lambda *_: (0,) * len(shape))


def _projection(x, cos_t, sin_t, w):
    b, s, _ = x.shape
    tm = PROJ_ROWS
    row_spec = lambda width: pl.BlockSpec((1, tm, width), lambda i, j: (i, j, 0))
    col_spec = lambda height: pl.BlockSpec((1, height, tm), lambda i, j: (i, 0, j))
    consts = [w["g_attn"], w["w_rows"], w["w_cols"], w["bff"], w["gq"], w["gkv"], w["wuq"],
              w["wuk"], w["wuvt"], w["tri"]]
    row_out = lambda width: jax.ShapeDtypeStruct((b, s, width), BF16)
    col_out = lambda height: jax.ShapeDtypeStruct((b, height, s), BF16)
    return pl.pallas_call(
        _proj_kernel,
        grid=(b, s // tm),
        in_specs=[row_spec(D_MODEL), col_spec(ROPE_HALF), col_spec(ROPE_HALF)]
                 + [_const_spec(c.shape) for c in consts],
        out_specs=[col_spec(GROUP_WIDTH), row_spec(GROUP_WIDTH), col_spec(GROUP_WIDTH),
                   row_spec(LANES), col_spec(GROUP_WIDTH), col_spec(2 * LANES),
                   row_spec(GROUP_WIDTH), row_spec(LANES), col_spec(GROUP_WIDTH)],
        out_shape=[col_out(GROUP_WIDTH), row_out(GROUP_WIDTH), col_out(GROUP_WIDTH),
                   row_out(LANES), col_out(GROUP_WIDTH), col_out(2 * LANES),
                   row_out(GROUP_WIDTH), row_out(LANES), col_out(GROUP_WIDTH)],
        scratch_shapes=[pltpu.VMEM((tm, D_MODEL), BF16), pltpu.VMEM((8, LANES), F32)],
        compiler_params=pltpu.CompilerParams(
            dimension_semantics=("arbitrary", "arbitrary"),
            vmem_limit_bytes=VMEM_LIMIT_BYTES),
        name="input_projection",
    )(x, cos_t, sin_t, *consts)


def _attn_kernel(tasks_ref, qm_ref, qa_ref, km_ref, ka_ref, vt_ref, mask_ref, o_ref,
                 bias_sc, qh_sc, s0_sc, s1_sc, acc_sc, m_sc):
    s_len = km_ref.shape[1]
    tq, tk = ATTN_TQ, ATTN_TK

    @pl.when((pl.program_id(0) == 0) & (pl.program_id(1) == 0))
    def _():
        key_idx = lax.broadcasted_iota(jnp.int32, (tk, tq), 0)
        qry_idx = lax.broadcasted_iota(jnp.int32, (tk, tq), 1)
        bias_sc[...] = jnp.where(key_idx <= qry_idx, 0.0, -jnp.inf)

    def kv_tile(k0):
        return jnp.concatenate([km_ref[0, pl.ds(k0, tk), :], ka_ref[0, pl.ds(k0, tk), :]], axis=-1)

    ones_rows = jnp.ones((ONES_ROWS, tk), BF16)

    qcat = jnp.concatenate([qm_ref[0], qa_ref[0]], axis=0)
    for a in range(2):
        qh_sc[a] = qcat * jnp.tile(mask_ref[0, a], (1, s_len // LANES))
    last_tile = s_len // tq - 1
    acc_sc[last_tile] = jnp.zeros(acc_sc.shape[1:], F32)
    m_sc[last_tile] = jnp.full(m_sc.shape[1:], -jnp.inf, F32)

    def produce(dst, t, diagonal):
        q0 = pl.multiple_of(tasks_ref[0, t] * tq, tq)
        k0 = pl.multiple_of(tasks_ref[1, t] * tk, tk)
        q_pair = jnp.concatenate([qh_sc[a, :, pl.ds(q0, tq)] for a in range(2)], axis=1)
        sc = jnp.dot(kv_tile(k0), q_pair, preferred_element_type=F32)
        maxima = []
        for a in range(2):
            sc_a = sc[:, a * tq:(a + 1) * tq]
            if diagonal:
                sc_a = sc_a + bias_sc[...]
            dst[a] = sc_a
            maxima.append(jnp.max(sc_a, axis=0, keepdims=True))
        return tuple(maxima)

    def consume(src, t, maxima):
        i = tasks_ref[0, t]
        k0 = pl.multiple_of(tasks_ref[1, t] * tk, tk)
        alphas, ps = [], []
        for a in range(2):
            m_run = m_sc[i, a]
            m_new = jnp.maximum(m_run, maxima[a])
            alphas.append(jnp.exp2(m_run - m_new))
            ps.append(jnp.exp2(src[a] - m_new).astype(BF16))
            m_sc[i, a] = m_new
        vt = jnp.concatenate([vt_ref[0, :, pl.ds(k0, tk)], ones_rows], axis=0)
        pv = jnp.dot(vt, jnp.concatenate(ps, axis=1), preferred_element_type=F32)
        for a in range(2):
            pv_a = jnp.concatenate([pv[a * HEAD_DIM:(a + 1) * HEAD_DIM, a * tq:(a + 1) * tq],
                                    pv[2 * HEAD_DIM:, a * tq:(a + 1) * tq]], axis=0)
            acc_sc[i, a] = alphas[a] * acc_sc[i, a] + pv_a

    half = tq // 2

    def produce_diag(dst, i):
        q0 = i * tq
        kcat = kv_tile(q0)
        q_all = jnp.concatenate([qh_sc[a, :, q0:q0 + tq] for a in range(2)], axis=1)
        q_late = jnp.concatenate([qh_sc[a, :, q0 + half:q0 + tq] for a in range(2)], axis=1)
        top = jnp.dot(kcat[0:half], q_all, preferred_element_type=F32)
        bot = jnp.dot(kcat[half:tk], q_late, preferred_element_type=F32)
        maxima = []
        for a in range(2):
            top_a = top[:, a * tq:(a + 1) * tq] + bias_sc[0:half, :]
            bot_a = bot[:, a * half:(a + 1) * half] + bias_sc[half:tk, half:tq]
            dst[a, 0:half, :] = top_a
            dst[a, half:tk, half:tq] = bot_a
            mx_top = jnp.max(top_a, axis=0, keepdims=True)
            mx_bot = jnp.max(bot_a, axis=0, keepdims=True)
            maxima.append(jnp.concatenate(
                [mx_top[:, 0:half], jnp.maximum(mx_top[:, half:tq], mx_bot)], axis=1))
        return tuple(maxima)

    def consume_diag(src, i, maxima):
        q0 = i * tq
        early, late = [], []
        for a in range(2):
            p_top = jnp.exp2(src[a, 0:half, :] - maxima[a]).astype(BF16)
            p_bot = jnp.exp2(src[a, half:tk, half:tq] - maxima[a][:, half:tq]).astype(BF16)
            early.append(p_top[:, 0:half])
            late.append(jnp.concatenate([p_top[:, half:tq], p_bot], axis=0))
        vt = jnp.concatenate([vt_ref[0, :, q0:q0 + tk], ones_rows], axis=0)
        pv_early = jnp.dot(vt[:, 0:half], jnp.concatenate(early, axis=1),
                           preferred_element_type=F32)
        pv_late = jnp.dot(vt, jnp.concatenate(late, axis=1), preferred_element_type=F32)
        for a in range(2):
            pv_a = jnp.concatenate([pv_early[:, a * half:(a + 1) * half],
                                    pv_late[:, a * half:(a + 1) * half]], axis=1)
            acc_sc[i, a] = jnp.concatenate(
                [pv_a[a * HEAD_DIM:(a + 1) * HEAD_DIM], pv_a[2 * HEAD_DIM:]], axis=0)
            m_sc[i, a] = maxima[a]

    bufs = (s0_sc, s1_sc)

    def step(t, parity, diagonal, mx):
        mx_next = produce(bufs[1 - parity], t + 1, diagonal)
        consume(bufs[parity], t, mx)
        return mx_next

    def steps(first, count, diagonal):
        def body(it, mx):
            t = first + it * count
            for u in range(count):
                mx = step(t + u, (first + u) % 2, diagonal, mx)
            return mx
        return body

    n_diag = s_len // tq
    n_tasks = n_diag * (n_diag + 1) // 2
    n_off = n_tasks - n_diag
    assert n_diag % 2 == 0 and n_off % UNROLL_STEPS == 0 and UNROLL_STEPS % 2 == 0
    assert tq == tk
    mx = produce_diag(s0_sc, 0)
    for t in range(n_diag - 1):
        if t + 1 < n_diag - 1:
            mx_next = produce_diag(bufs[(t + 1) % 2], t + 1)
        else:
            mx_next = produce(bufs[(t + 1) % 2], t + 1, True)
        consume_diag(bufs[t % 2], t, mx)
        mx = mx_next
    mx = lax.fori_loop(0, n_off // UNROLL_STEPS, steps(n_diag - 1, UNROLL_STEPS, False), mx)
    consume(bufs[(n_tasks - 1) % 2], n_tasks - 1, mx)

    def finalize(it, _):
        for u in range(FINALIZE_UNROLL):
            i = it * FINALIZE_UNROLL + u
            q0 = pl.multiple_of(i * tq, tq)
            out_t = jnp.concatenate(
                [acc_sc[i, a, 0:HEAD_DIM, :] / acc_sc[i, a, HEAD_DIM:HEAD_DIM + 1, :]
                 for a in range(2)], axis=0)
            o_ref[0, pl.ds(q0, tq), :] = out_t.T
        return 0

    assert n_diag % FINALIZE_UNROLL == 0
    lax.fori_loop(0, n_diag // FINALIZE_UNROLL, finalize, 0)


def _attention(qmt, qat, km, ka, vt, mask, qa_batched, qa_per_pairs):
    b, s, _ = km.shape
    pairs = N_HEADS // 2
    tok = lambda f: pl.BlockSpec((1, s, LANES), f)
    feat = lambda f: pl.BlockSpec((1, LANES, s), f)
    qa_map = (lambda i, p: (i, p // qa_per_pairs, 0)) if qa_batched else (lambda i, p: (0, 0, 0))
    n_q = s // ATTN_TQ
    task_list = [(i, i) for i in range(n_q)] + [(i, j) for i in range(n_q) for j in range(i)]
    tasks = jnp.asarray(np.array(task_list, np.int32).T)
    return pl.pallas_call(
        _attn_kernel,
        grid=(b, pairs),
        in_specs=[pl.BlockSpec(memory_space=pltpu.SMEM),
                  feat(lambda i, p: (i, p, 0)), feat(qa_map), tok(lambda i, p: (i, 0, p)),
                  tok(lambda i, p: (i, 0, 0)), feat(lambda i, p: (i, p, 0)),
                  pl.BlockSpec((1, 2, 2 * LANES, LANES), lambda i, p: (p, 0, 0, 0))],
        out_specs=tok(lambda i, p: (i, 0, p)),
        out_shape=jax.ShapeDtypeStruct((b, s, GROUP_WIDTH), F32),
        scratch_shapes=[pltpu.VMEM((ATTN_TK, ATTN_TQ), F32),
                        pltpu.VMEM((2, 2 * LANES, s), BF16),
                        pltpu.VMEM((2, ATTN_TK, ATTN_TQ), F32),
                        pltpu.VMEM((2, ATTN_TK, ATTN_TQ), F32),
                        pltpu.VMEM((n_q, 2, HEAD_DIM + ONES_ROWS, ATTN_TQ), F32),
                        pltpu.VMEM((n_q, 2, 1, ATTN_TQ), F32)],
        compiler_params=pltpu.CompilerParams(
            dimension_semantics=("arbitrary", "arbitrary"),
            vmem_limit_bytes=VMEM_LIMIT_BYTES),
        name="causal_attention",
    )(tasks, qmt, qat, km, ka, vt, mask)


def _mlp_kernel(x_ref, fox_ref, mla_ref, gfox_ref, gmla_ref, wo_ref, gmlp_ref, wup_ref,
                wdown_ref, gfin_ref, o_ref, *, final):
    mixed = jnp.concatenate([_rms(fox_ref[...], gfox_ref[...]).astype(BF16),
                             _rms(mla_ref[...], gmla_ref[...]).astype(BF16)], axis=-1)
    x1 = x_ref[...] + jnp.dot(mixed, wo_ref[...], preferred_element_type=F32)
    h = _rms(x1, gmlp_ref[...]).astype(BF16)
    u = jnp.dot(h, wup_ref[...], preferred_element_type=F32)
    act = jnp.square(jnp.maximum(u, 0.0)).astype(BF16)
    y = x1 + jnp.dot(act, wdown_ref[...], preferred_element_type=F32)
    o_ref[...] = _rms(y, gfin_ref[...]) if final else y


def _out_mlp(x2d, fox, mla, w, final):
    n = x2d.shape[0]
    tm = MLP_ROWS
    row = lambda width: pl.BlockSpec((tm, width), lambda i: (i, 0))
    resident = lambda a: pl.BlockSpec(a.shape, lambda i: (0,) * a.ndim,
                                      pipeline_mode=pl.Buffered(1))
    consts = [w["gfox"], w["gmla"], w["wo"], w["gmlp"], w["wup"], w["wdown"], w["gfin"]]
    return pl.pallas_call(
        functools.partial(_mlp_kernel, final=final),
        grid=(n // tm,),
        in_specs=[row(D_MODEL), row(GROUP_WIDTH), row(GROUP_WIDTH)] + [resident(c) for c in consts],
        out_specs=row(D_MODEL),
        out_shape=jax.ShapeDtypeStruct((n, D_MODEL), F32),
        compiler_params=pltpu.CompilerParams(
            dimension_semantics=("arbitrary",),
            vmem_limit_bytes=VMEM_LIMIT_BYTES),
        name="out_proj_mlp",
    )(x2d, fox, mla, *consts)


def _rope_split_cols(wr):
    k = wr.shape[0]
    heads = wr.shape[1]
    per_group = LANES // ROPE_DIM
    first = wr[:, :, :ROPE_HALF].reshape(k, heads // per_group, per_group * ROPE_HALF)
    second = wr[:, :, ROPE_HALF:].reshape(k, heads // per_group, per_group * ROPE_HALF)
    return jnp.concatenate([first, second], axis=-1).reshape(k, heads * ROPE_DIM)


def _prep_weights(attn_norm_g, w_in, b_forget, q_norm_g, w_uq, kv_norm_g, w_ukv, fox_out_g,
                  mla_out_g, w_o, mlp_norm_g, w_up, w_down, final_norm_g):
    row = lambda v: v.reshape(1, -1).astype(F32)
    pad_lanes = lambda a: jnp.pad(a, ((0, 0), (0, LANES - a.shape[1])))
    wkr = w_in[:, OFF_KR:IN_COLS].reshape(D_MODEL, 1, ROPE_DIM)
    wkr = _rope_split_cols(jnp.broadcast_to(wkr, (D_MODEL, LANES // ROPE_DIM, ROPE_DIM)))
    uq = w_uq.reshape(Q_RANK, N_HEADS, MLA_QK_DIM)
    ukv = w_ukv.reshape(KV_RANK, N_HEADS, 2 * HEAD_DIM)
    tri = np.tril(np.ones((SCAN_CHUNK, SCAN_CHUNK), np.float32))
    w_rows = jnp.concatenate([w_in[:, OFF_FQ:OFF_FK], w_in[:, OFF_FV:OFF_FF], wkr], axis=1)
    w_cols = jnp.concatenate([w_in[:, OFF_FK:OFF_FV], w_in[:, OFF_CQ:OFF_CKV],
                              w_in[:, OFF_CKV:OFF_KR], pad_lanes(w_in[:, OFF_FF:OFF_CQ])], axis=1)
    wuq = jnp.concatenate([uq[:, :, :HEAD_DIM].reshape(Q_RANK, GROUP_WIDTH),
                           _rope_split_cols(uq[:, :, HEAD_DIM:])], axis=1)
    return {
        "g_attn": row(attn_norm_g),
        "w_rows": w_rows.T.astype(BF16),
        "w_cols": w_cols.astype(BF16),
        "bff": pad_lanes(row(b_forget)),
        "gq": row(q_norm_g),
        "gkv": row(kv_norm_g),
        "wuq": wuq.T.astype(BF16),
        "wuk": ukv[:, :, :HEAD_DIM].reshape(KV_RANK, GROUP_WIDTH).astype(BF16),
        "wuvt": ukv[:, :, HEAD_DIM:].reshape(KV_RANK, GROUP_WIDTH).T.astype(BF16),
        "tri": jnp.asarray(tri, BF16),
        "gfox": row(fox_out_g),
        "gmla": row(mla_out_g),
        "wo": w_o.astype(BF16),
        "gmlp": row(mlp_norm_g),
        "wup": w_up.astype(BF16),
        "wdown": w_down.astype(BF16),
        "gfin": row(final_norm_g),
    }


def _head_masks():
    pairs = N_HEADS // 2
    fox = np.zeros((pairs, 2, 2 * LANES), np.float32)
    mla = np.zeros((pairs, 2, 2 * LANES), np.float32)
    per_group = LANES // ROPE_DIM
    for p in range(pairs):
        for a in range(2):
            h = 2 * p + a
            fox[p, a, a * HEAD_DIM:(a + 1) * HEAD_DIM] = 1.0
            mla[p, a, a * HEAD_DIM:(a + 1) * HEAD_DIM] = 1.0
            for piece in range(3):
                fox[p, a, LANES + piece * N_HEADS + h] = 1.0
            hh = h % per_group
            for half in range(2):
                lo = LANES + half * (LANES // 2) + hh * ROPE_HALF
                mla[p, a, lo:lo + ROPE_HALF] = 1.0
    rep = lambda m: jnp.asarray(np.repeat(m[..., None], LANES, axis=-1), BF16)
    return rep(fox), rep(mla)


def kernel(x, positions, attn_norm_g, w_in, b_forget, q_norm_g, w_uq, kv_norm_g, w_ukv,
           fox_out_g, mla_out_g, w_o, mlp_norm_g, w_up, w_down, final_norm_g):
    b, s, d = x.shape
    depth = w_in.shape[0]
    inv_freq = ROPE_THETA ** (-jnp.arange(0, ROPE_DIM, 2, dtype=F32) / ROPE_DIM)
    cos_t, sin_t = _rope_tables(positions, inv_freq)
    fox_mask, mla_mask = _head_masks()
    ones_aux = jnp.ones((1, LANES, s), BF16)

    for l in range(depth):
        w = _prep_weights(attn_norm_g[l], w_in[l], b_forget[l], q_norm_g[l], w_uq[l],
                          kv_norm_g[l], w_ukv[l], fox_out_g[l], mla_out_g[l], w_o[l],
                          mlp_norm_g[l], w_up[l], w_down[l], final_norm_g)
        fqt, fk, fvt, faux, mqnt, mqrt, mkn, mkr, mvt = _projection(x, cos_t, sin_t, w)
        fox = _attention(fqt, ones_aux, fk, faux, fvt, fox_mask, False, 1)
        mla = _attention(mqnt, mqrt, mkn, mkr, mvt, mla_mask, True, 2)
        y = _out_mlp(x.reshape(b * s, d), fox.reshape(b * s, GROUP_WIDTH),
                     mla.reshape(b * s, GROUP_WIDTH), w, l == depth - 1)
        x = y.reshape(b, s, d)
    return x
```

```python
import functools
import math

import numpy as np
import jax
import jax.numpy as jnp
from jax import lax
from jax.experimental import pallas as pl
from jax.experimental.pallas import tpu as pltpu

D_MODEL = 1024
HEAD_DIM = 64
N_HEADS = 8
GROUP_WIDTH = N_HEADS * HEAD_DIM
ROPE_DIM = 32
ROPE_HALF = ROPE_DIM // 2
MLA_QK_DIM = HEAD_DIM + ROPE_DIM
Q_RANK = 384
KV_RANK = 256
D_FF = 4096
ROPE_THETA = 10000.0
EPS = 1e-6

OFF_FQ = 0
OFF_FK = OFF_FQ + GROUP_WIDTH
OFF_FV = OFF_FK + GROUP_WIDTH
OFF_FF = OFF_FV + GROUP_WIDTH
OFF_CQ = OFF_FF + N_HEADS
OFF_CKV = OFF_CQ + Q_RANK
OFF_KR = OFF_CKV + KV_RANK
IN_COLS = OFF_KR + ROPE_DIM

LANES = 128
VMEM_LIMIT_BYTES = 56 * 1024 * 1024

LOG2E = math.log2(math.e)
FOX_QSCALE = LOG2E / math.sqrt(HEAD_DIM)
MLA_QSCALE = LOG2E / math.sqrt(MLA_QK_DIM)

PROJ_ROWS = 1024
ATTN_TQ = 512
ATTN_TK = 512
MLP_ROWS = 512
FF_CHUNK = 4096
SCAN_CHUNK = 128
ONES_ROWS = 16
UNROLL_STEPS = 14
FINALIZE_UNROLL = 4

F32 = jnp.float32
BF16 = jnp.bfloat16
NT_DIMS = (((1,), (1,)), ((), ()))


def _rms(x, g):
    return x * lax.rsqrt(jnp.mean(x * x, axis=-1, keepdims=True) + EPS) * g


def _split3(x):
    hi = x.astype(BF16).astype(F32)
    mid = (x - hi).astype(BF16).astype(F32)
    lo = (x - hi - mid).astype(BF16).astype(F32)
    return hi, mid, lo


def _pack3(x):
    hi, mid, lo = _split3(x)
    packed = hi + pltpu.roll(mid, N_HEADS, axis=1) + pltpu.roll(lo, 2 * N_HEADS, axis=1)
    return packed.astype(BF16)


def _rope_table_kernel(pos_ref, freq_ref, cos_ref, sin_ref):
    ang = pos_ref[0].astype(F32) * freq_ref[...]
    cos_ref[0] = jnp.cos(ang)
    sin_ref[0] = jnp.sin(ang)


def _rope_tables(positions, inv_freq):
    b, s = positions.shape
    table = pl.BlockSpec((1, ROPE_HALF, s), lambda i: (i, 0, 0))
    return pl.pallas_call(
        _rope_table_kernel,
        grid=(b,),
        in_specs=[pl.BlockSpec((1, 1, s), lambda i: (i, 0, 0)),
                  pl.BlockSpec((ROPE_HALF, 1), lambda i: (0, 0))],
        out_specs=[table, table],
        out_shape=[jax.ShapeDtypeStruct((b, ROPE_HALF, s), F32)] * 2,
        name="rope_tables",
    )(positions.reshape(b, 1, s), inv_freq.reshape(ROPE_HALF, 1))


def _proj_kernel(x_ref, cos_ref, sin_ref, g_attn_ref, w_rows_ref, w_cols_ref, bff_ref, gq_ref,
                 gkv_ref, wuq_ref, wuk_ref, wuvt_ref, tri_ref,
                 fqt_ref, fk_ref, fvt_ref, faux_ref, mqnt_ref, mqrt_ref, mkn_ref, mkr_ref, mvt_ref,
                 h_sc, carry_sc):
    rows = x_ref.shape[1]

    @pl.when(pl.program_id(1) == 0)
    def _():
        carry_sc[...] = jnp.zeros_like(carry_sc)

    h_sc[...] = _rms(x_ref[0], g_attn_ref[...]).astype(BF16)
    h = h_sc[...]

    groups = LANES // ROPE_HALF
    row = lax.broadcasted_iota(jnp.int32, (LANES, rows), 0)
    cos_t = jnp.tile(cos_ref[0], (groups, 1))
    sin_t = jnp.tile(sin_ref[0], (groups, 1))
    sin_s = jnp.where(row < LANES // 2, -sin_t, sin_t)

    def rope_t(xt):
        return xt * cos_t + pltpu.roll(xt, LANES // 2, axis=0) * sin_s


    tk = jnp.dot(h, w_cols_ref[...], preferred_element_type=F32)
    o_cq, o_ckv, o_ff = GROUP_WIDTH, GROUP_WIDTH + Q_RANK, GROUP_WIDTH + Q_RANK + KV_RANK
    fk_ref[0] = tk[:, 0:o_cq].astype(BF16)

    ft = lax.dot_general(w_rows_ref[...], h, NT_DIMS, preferred_element_type=F32)
    fqt_ref[0] = (ft[0:GROUP_WIDTH] * FOX_QSCALE).astype(BF16)
    fvt_ref[0] = ft[GROUP_WIDTH:2 * GROUP_WIDTH].astype(BF16)
    mkr_ref[0] = rope_t(ft[2 * GROUP_WIDTH:2 * GROUP_WIDTH + LANES]).T.astype(BF16)

    lane = lax.broadcasted_iota(jnp.int32, (1, LANES), 1)
    head_lane = lane < N_HEADS
    z = tk[:, o_ff:o_ff + LANES] + bff_ref[...]
    log_f = jnp.where(head_lane, jax.nn.log_sigmoid(z), 0.0)
    tri = tri_ref[...]
    carry = carry_sc[0:1, :]
    for c in range(rows // SCAN_CHUNK):
        part = jnp.dot(tri, _pack3(log_f[c * SCAN_CHUNK:(c + 1) * SCAN_CHUNK, :]),
                       preferred_element_type=F32)
        part = (part + pltpu.roll(part, LANES - N_HEADS, axis=1)
                + pltpu.roll(part, LANES - 2 * N_HEADS, axis=1))
        cum = jnp.where(head_lane, part, 0.0) + carry
        carry = cum[SCAN_CHUNK - 1:SCAN_CHUNK, :]
        faux_ref[0, c * SCAN_CHUNK:(c + 1) * SCAN_CHUNK, :] = _pack3(cum * (-LOG2E))
    carry_sc[...] = jnp.broadcast_to(carry, carry_sc.shape)

    cq = _rms(tk[:, o_cq:o_ckv], gq_ref[...]).astype(BF16)
    qt = lax.dot_general(wuq_ref[...], cq, NT_DIMS, preferred_element_type=F32)
    mqnt_ref[0] = (qt[0:GROUP_WIDTH] * MLA_QSCALE).astype(BF16)
    for g in range(2 * N_HEADS * ROPE_HALF // LANES):
        lo = GROUP_WIDTH + g * LANES
        mqrt_ref[0, g * LANES:(g + 1) * LANES, :] = (
            rope_t(qt[lo:lo + LANES]) * MLA_QSCALE).astype(BF16)

    ckv = _rms(tk[:, o_ckv:o_ff], gkv_ref[...]).astype(BF16)
    mkn_ref[0] = jnp.dot(ckv, wuk_ref[...], preferred_element_type=F32).astype(BF16)
    mvt_ref[0] = lax.dot_general(wuvt_ref[...], ckv, NT_DIMS,
                                 preferred_element_type=F32).astype(BF16)


def _const_spec(shape):
    return pl.BlockSpec(shape, lambda *_: (0,) * len(shape))


def _projection(x, cos_t, sin_t, w):
    b, s, _ = x.shape
    tm = PROJ_ROWS
    row_spec = lambda width: pl.BlockSpec((1, tm, width), lambda i, j: (i, j, 0))
    col_spec = lambda height: pl.BlockSpec((1, height, tm), lambda i, j: (i, 0, j))
    consts = [w["g_attn"], w["w_rows"], w["w_cols"], w["bff"], w["gq"], w["gkv"], w["wuq"],
              w["wuk"], w["wuvt"], w["tri"]]
    row_out = lambda width: jax.ShapeDtypeStruct((b, s, width), BF16)
    col_out = lambda height: jax.ShapeDtypeStruct((b, height, s), BF16)
    return pl.pallas_call(
        _proj_kernel,
        grid=(b, s // tm),
        in_specs=[row_spec(D_MODEL), col_spec(ROPE_HALF), col_spec(ROPE_HALF)]
                 + [_const_spec(c.shape) for c in consts],
        out_specs=[col_spec(GROUP_WIDTH), row_spec(GROUP_WIDTH), col_spec(GROUP_WIDTH),
                   row_spec(LANES), col_spec(GROUP_WIDTH), col_spec(2 * LANES),
                   row_spec(GROUP_WIDTH), row_spec(LANES), col_spec(GROUP_WIDTH)],
        out_shape=[col_out(GROUP_WIDTH), row_out(GROUP_WIDTH), col_out(GROUP_WIDTH),
                   row_out(LANES), col_out(GROUP_WIDTH), col_out(2 * LANES),
                   row_out(GROUP_WIDTH), row_out(LANES), col_out(GROUP_WIDTH)],
        scratch_shapes=[pltpu.VMEM((tm, D_MODEL), BF16), pltpu.VMEM((8, LANES), F32)],
        compiler_params=pltpu.CompilerParams(
            dimension_semantics=("arbitrary", "arbitrary"),
            vmem_limit_bytes=VMEM_LIMIT_BYTES),
        name="input_projection",
    )(x, cos_t, sin_t, *consts)


def _attn_kernel(tasks_ref, qm_ref, qa_ref, km_ref, ka_ref, vt_ref, mask_ref, o_ref,
                 bias_sc, qh_sc, s0_sc, s1_sc, acc_sc, m_sc):
    s_len = km_ref.shape[1]
    tq, tk = ATTN_TQ, ATTN_TK

    @pl.when((pl.program_id(0) == 0) & (pl.program_id(1) == 0))
    def _():
        key_idx = lax.broadcasted_iota(jnp.int32, (tk, tq), 0)
        qry_idx = lax.broadcasted_iota(jnp.int32, (tk, tq), 1)
        bias_sc[...] = jnp.where(key_idx <= qry_idx, 0.0, -jnp.inf)

    def kv_tile(k0):
        return jnp.concatenate([km_ref[0, pl.ds(k0, tk), :], ka_ref[0, pl.ds(k0, tk), :]], axis=-1)

    ones_rows = jnp.ones((ONES_ROWS, tk), BF16)

    qcat = jnp.concatenate([qm_ref[0], qa_ref[0]], axis=0)
    for a in range(2):
        qh_sc[a] = qcat * jnp.tile(mask_ref[0, a], (1, s_len // LANES))
    last_tile = s_len // tq - 1
    acc_sc[last_tile] = jnp.zeros(acc_sc.shape[1:], F32)
    m_sc[last_tile] = jnp.full(m_sc.shape[1:], -jnp.inf, F32)

    def produce(dst, t, diagonal):
        q0 = pl.multiple_of(tasks_ref[0, t] * tq, tq)
        k0 = pl.multiple_of(tasks_ref[1, t] * tk, tk)
        q_pair = jnp.concatenate([qh_sc[a, :, pl.ds(q0, tq)] for a in range(2)], axis=1)
        sc = jnp.dot(kv_tile(k0), q_pair, preferred_element_type=F32)
        maxima = []
        for a in range(2):
            sc_a = sc[:, a * tq:(a + 1) * tq]
            if diagonal:
                sc_a = sc_a + bias_sc[...]
            dst[a] = sc_a
            maxima.append(jnp.max(sc_a, axis=0, keepdims=True))
        return tuple(maxima)

    def consume(src, t, maxima):
        i = tasks_ref[0, t]
        k0 = pl.multiple_of(tasks_ref[1, t] * tk, tk)
        alphas, ps = [], []
        for a in range(2):
            m_run = m_sc[i, a]
            m_new = jnp.maximum(m_run, maxima[a])
            alphas.append(jnp.exp2(m_run - m_new))
            ps.append(jnp.exp2(src[a] - m_new).astype(BF16))
            m_sc[i, a] = m_new
        vt = jnp.concatenate([vt_ref[0, :, pl.ds(k0, tk)], ones_rows], axis=0)
        pv = jnp.dot(vt, jnp.concatenate(ps, axis=1), preferred_element_type=F32)
        for a in range(2):
            pv_a = jnp.concatenate([pv[a * HEAD_DIM:(a + 1) * HEAD_DIM, a * tq:(a + 1) * tq],
                                    pv[2 * HEAD_DIM:, a * tq:(a + 1) * tq]], axis=0)
            acc_sc[i, a] = alphas[a] * acc_sc[i, a] + pv_a

    half = tq // 2

    def produce_diag(dst, i):
        q0 = i * tq
        kcat = kv_tile(q0)
        q_all = jnp.concatenate([qh_sc[a, :, q0:q0 + tq] for a in range(2)], axis=1)
        q_late = jnp.concatenate([qh_sc[a, :, q0 + half:q0 + tq] for a in range(2)], axis=1)
        top = jnp.dot(kcat[0:half], q_all, preferred_element_type=F32)
        bot = jnp.dot(kcat[half:tk], q_late, preferred_element_type=F32)
        maxima = []
        for a in range(2):
            top_a = top[:, a * tq:(a + 1) * tq] + bias_sc[0:half, :]
            bot_a = bot[:, a * half:(a + 1) * half] + bias_sc[half:tk, half:tq]
            dst[a, 0:half, :] = top_a
            dst[a, half:tk, half:tq] = bot_a
            mx_top = jnp.max(top_a, axis=0, keepdims=True)
            mx_bot = jnp.max(bot_a, axis=0, keepdims=True)
            maxima.append(jnp.concatenate(
                [mx_top[:, 0:half], jnp.maximum(mx_top[:, half:tq], mx_bot)], axis=1))
        return tuple(maxima)

    def consume_diag(src, i, maxima):
        q0 = i * tq
        early, late = [], []
        for a in range(2):
            p_top = jnp.exp2(src[a, 0:half, :] - maxima[a]).astype(BF16)
            p_bot = jnp.exp2(src[a, half:tk, half:tq] - maxima[a][:, half:tq]).astype(BF16)
            early.append(p_top[:, 0:half])
            late.append(jnp.concatenate([p_top[:, half:tq], p_bot], axis=0))
        vt = jnp.concatenate([vt_ref[0, :, q0:q0 + tk], ones_rows], axis=0)
        pv_early = jnp.dot(vt[:, 0:half], jnp.concatenate(early, axis=1),
                           preferred_element_type=F32)
        pv_late = jnp.dot(vt, jnp.concatenate(late, axis=1), preferred_element_type=F32)
        for a in range(2):
            pv_a = jnp.concatenate([pv_early[:, a * half:(a + 1) * half],
                                    pv_late[:, a * half:(a + 1) * half]], axis=1)
            acc_sc[i, a] = jnp.concatenate(
                [pv_a[a * HEAD_DIM:(a + 1) * HEAD_DIM], pv_a[2 * HEAD_DIM:]], axis=0)
            m_sc[i, a] = maxima[a]

    bufs = (s0_sc, s1_sc)

    def step(t, parity, diagonal, mx):
        mx_next = produce(bufs[1 - parity], t + 1, diagonal)
        consume(bufs[parity], t, mx)
        return mx_next

    def steps(first, count, diagonal):
        def body(it, mx):
            t = first + it * count
            for u in range(count):
                mx = step(t + u, (first + u) % 2, diagonal, mx)
            return mx
        return body

    n_diag = s_len // tq
    n_tasks = n_diag * (n_diag + 1) // 2
    n_off = n_tasks - n_diag
    assert n_diag % 2 == 0 and n_off % UNROLL_STEPS == 0 and UNROLL_STEPS % 2 == 0
    assert tq == tk
    mx = produce_diag(s0_sc, 0)
    for t in range(n_diag - 1):
        if t + 1 < n_diag - 1:
            mx_next = produce_diag(bufs[(t + 1) % 2], t + 1)
        else:
            mx_next = produce(bufs[(t + 1) % 2], t + 1, True)
        consume_diag(bufs[t % 2], t, mx)
        mx = mx_next
    mx = lax.fori_loop(0, n_off // UNROLL_STEPS, steps(n_diag - 1, UNROLL_STEPS, False), mx)
    consume(bufs[(n_tasks - 1) % 2], n_tasks - 1, mx)

    def finalize(it, _):
        for u in range(FINALIZE_UNROLL):
            i = it * FINALIZE_UNROLL + u
            q0 = pl.multiple_of(i * tq, tq)
            out_t = jnp.concatenate(
                [acc_sc[i, a, 0:HEAD_DIM, :] / acc_sc[i, a, HEAD_DIM:HEAD_DIM + 1, :]
                 for a in range(2)], axis=0)
            o_ref[0, pl.ds(q0, tq), :] = out_t.astype(BF16).T
        return 0

    assert n_diag % FINALIZE_UNROLL == 0
    lax.fori_loop(0, n_diag // FINALIZE_UNROLL, finalize, 0)


def _attention(qmt, qat, km, ka, vt, mask, qa_batched, qa_per_pairs):
    b, s, _ = km.shape
    pairs = N_HEADS // 2
    tok = lambda f: pl.BlockSpec((1, s, LANES), f)
    feat = lambda f: pl.BlockSpec((1, LANES, s), f)
    qa_map = (lambda i, p: (i, p // qa_per_pairs, 0)) if qa_batched else (lambda i, p: (0, 0, 0))
    n_q = s // ATTN_TQ
    task_list = [(i, i) for i in range(n_q)] + [(i, j) for i in range(n_q) for j in range(i)]
    tasks = jnp.asarray(np.array(task_list, np.int32).T)
    return pl.pallas_call(
        _attn_kernel,
        grid=(b, pairs),
        in_specs=[pl.BlockSpec(memory_space=pltpu.SMEM),
                  feat(lambda i, p: (i, p, 0)), feat(qa_map), tok(lambda i, p: (i, 0, p)),
                  tok(lambda i, p: (i, 0, 0)), feat(lambda i, p: (i, p, 0)),
                  pl.BlockSpec((1, 2, 2 * LANES, LANES), lambda i, p: (p, 0, 0, 0))],
        out_specs=tok(lambda i, p: (i, 0, p)),
        out_shape=jax.ShapeDtypeStruct((b, s, GROUP_WIDTH), BF16),
        scratch_shapes=[pltpu.VMEM((ATTN_TK, ATTN_TQ), F32),
                        pltpu.VMEM((2, 2 * LANES, s), BF16),
                        pltpu.VMEM((2, ATTN_TK, ATTN_TQ), F32),
                        pltpu.VMEM((2, ATTN_TK, ATTN_TQ), F32),
                        pltpu.VMEM((n_q, 2, HEAD_DIM + ONES_ROWS, ATTN_TQ), F32),
                        pltpu.VMEM((n_q, 2, 1, ATTN_TQ), F32)],
        compiler_params=pltpu.CompilerParams(
            dimension_semantics=("arbitrary", "arbitrary"),
            vmem_limit_bytes=VMEM_LIMIT_BYTES),
        name="causal_attention",
    )(tasks, qmt, qat, km, ka, vt, mask)


def _mlp_kernel(x_ref, fox_ref, mla_ref, gfox_ref, gmla_ref, wo_ref, gmlp_ref, wup_ref,
                wdown_ref, gfin_ref, o_ref, *, final):
    mixed = jnp.concatenate([_rms(fox_ref[...].astype(F32), gfox_ref[...]).astype(BF16),
                             _rms(mla_ref[...].astype(F32), gmla_ref[...]).astype(BF16)], axis=-1)
    x1 = x_ref[...] + jnp.dot(mixed, wo_ref[...], preferred_element_type=F32)
    h = _rms(x1, gmlp_ref[...]).astype(BF16)
    y = x1
    for c in range(D_FF // FF_CHUNK):
        u = jnp.dot(h, wup_ref[:, c * FF_CHUNK:(c + 1) * FF_CHUNK], preferred_element_type=F32)
        act = jnp.square(jnp.maximum(u, 0.0)).astype(BF16)
        y = y + jnp.dot(act, wdown_ref[c * FF_CHUNK:(c + 1) * FF_CHUNK, :],
                        preferred_element_type=F32)
    o_ref[...] = _rms(y, gfin_ref[...]) if final else y


def _out_mlp(x2d, fox, mla, w, final):
    n = x2d.shape[0]
    tm = MLP_ROWS
    row = lambda width: pl.BlockSpec((tm, width), lambda i: (i, 0))
    resident = lambda a: pl.BlockSpec(a.shape, lambda i: (0,) * a.ndim,
                                      pipeline_mode=pl.Buffered(1))
    consts = [w["gfox"], w["gmla"], w["wo"], w["gmlp"], w["wup"], w["wdown"], w["gfin"]]
    return pl.pallas_call(
        functools.partial(_mlp_kernel, final=final),
        grid=(n // tm,),
        in_specs=[row(D_MODEL), row(GROUP_WIDTH), row(GROUP_WIDTH)] + [resident(c) for c in consts],
        out_specs=row(D_MODEL),
        out_shape=jax.ShapeDtypeStruct((n, D_MODEL), F32),
        compiler_params=pltpu.CompilerParams(
            dimension_semantics=("arbitrary",),
            vmem_limit_bytes=VMEM_LIMIT_BYTES),
        name="out_proj_mlp",
    )(x2d, fox, mla, *consts)


def _rope_split_cols(wr):
    k = wr.shape[0]
    heads = wr.shape[1]
    per_group = LANES // ROPE_DIM
    first = wr[:, :, :ROPE_HALF].reshape(k, heads // per_group, per_group * ROPE_HALF)
    second = wr[:, :, ROPE_HALF:].reshape(k, heads // per_group, per_group * ROPE_HALF)
    return jnp.concatenate([first, second], axis=-1).reshape(k, heads * ROPE_DIM)


def _prep_weights(attn_norm_g, w_in, b_forget, q_norm_g, w_uq, kv_norm_g, w_ukv, fox_out_g,
                  mla_out_g, w_o, mlp_norm_g, w_up, w_down, final_norm_g):
    row = lambda v: v.reshape(1, -1).astype(F32)
    pad_lanes = lambda a: jnp.pad(a, ((0, 0), (0, LANES - a.shape[1])))
    wkr = w_in[:, OFF_KR:IN_COLS].reshape(D_MODEL, 1, ROPE_DIM)
    wkr = _rope_split_cols(jnp.broadcast_to(wkr, (D_MODEL, LANES // ROPE_DIM, ROPE_DIM)))
    uq = w_uq.reshape(Q_RANK, N_HEADS, MLA_QK_DIM)
    ukv = w_ukv.reshape(KV_RANK, N_HEADS, 2 * HEAD_DIM)
    tri = np.tril(np.ones((SCAN_CHUNK, SCAN_CHUNK), np.float32))
    w_rows = jnp.concatenate([w_in[:, OFF_FQ:OFF_FK], w_in[:, OFF_FV:OFF_FF], wkr], axis=1)
    w_cols = jnp.concatenate([w_in[:, OFF_FK:OFF_FV], w_in[:, OFF_CQ:OFF_CKV],
                              w_in[:, OFF_CKV:OFF_KR], pad_lanes(w_in[:, OFF_FF:OFF_CQ])], axis=1)
    wuq = jnp.concatenate([uq[:, :, :HEAD_DIM].reshape(Q_RANK, GROUP_WIDTH),
                           _rope_split_cols(uq[:, :, HEAD_DIM:])], axis=1)
    return {
        "g_attn": row(attn_norm_g),
        "w_rows": w_rows.T.astype(BF16),
        "w_cols": w_cols.astype(BF16),
        "bff": pad_lanes(row(b_forget)),
        "gq": row(q_norm_g),
        "gkv": row(kv_norm_g),
        "wuq": wuq.T.astype(BF16),
        "wuk": ukv[:, :, :HEAD_DIM].reshape(KV_RANK, GROUP_WIDTH).astype(BF16),
        "wuvt": ukv[:, :, HEAD_DIM:].reshape(KV_RANK, GROUP_WIDTH).T.astype(BF16),
        "tri": jnp.asarray(tri, BF16),
        "gfox": row(fox_out_g),
        "gmla": row(mla_out_g),
        "wo": w_o.astype(BF16),
        "gmlp": row(mlp_norm_g),
        "wup": w_up.astype(BF16),
        "wdown": w_down.astype(BF16),
        "gfin": row(final_norm_g),
    }


def _head_masks():
    pairs = N_HEADS // 2
    fox = np.zeros((pairs, 2, 2 * LANES), np.float32)
    mla = np.zeros((pairs, 2, 2 * LANES), np.float32)
    per_group = LANES // ROPE_DIM
    for p in range(pairs):
        for a in range(2):
            h = 2 * p + a
            fox[p, a, a * HEAD_DIM:(a + 1) * HEAD_DIM] = 1.0
            mla[p, a, a * HEAD_DIM:(a + 1) * HEAD_DIM] = 1.0
            for piece in range(3):
                fox[p, a, LANES + piece * N_HEADS + h] = 1.0
            hh = h % per_group
            for half in range(2):
                lo = LANES + half * (LANES // 2) + hh * ROPE_HALF
                mla[p, a, lo:lo + ROPE_HALF] = 1.0
    rep = lambda m: jnp.asarray(np.repeat(m[..., None], LANES, axis=-1), BF16)
    return rep(fox), rep(mla)


def kernel(x, positions, attn_norm_g, w_in, b_forget, q_norm_g, w_uq, kv_norm_g, w_ukv,
           fox_out_g, mla_out_g, w_o, mlp_norm_g, w_up, w_down, final_norm_g):
    b, s, d = x.shape
    depth = w_in.shape[0]
    inv_freq = ROPE_THETA ** (-jnp.arange(0, ROPE_DIM, 2, dtype=F32) / ROPE_DIM)
    cos_t, sin_t = _rope_tables(positions, inv_freq)
    fox_mask, mla_mask = _head_masks()
    ones_aux = jnp.ones((1, LANES, s), BF16)

    for l in range(depth):
        w = _prep_weights(attn_norm_g[l], w_in[l], b_forget[l], q_norm_g[l], w_uq[l],
                          kv_norm_g[l], w_ukv[l], fox_out_g[l], mla_out_g[l], w_o[l],
                          mlp_norm_g[l], w_up[l], w_down[l], final_norm_g)
        fqt, fk, fvt, faux, mqnt, mqrt, mkn, mkr, mvt = _projection(x, cos_t, sin_t, w)
        fox = _attention(fqt, ones_aux, fk, faux, fvt, fox_mask, False, 1)
        mla = _attention(mqnt, mqrt, mkn, mkr, mvt, mla_mask, True, 2)
        y = _out_mlp(x.reshape(b * s, d), fox.reshape(b * s, GROUP_WIDTH),
                     mla.reshape(b * s, GROUP_WIDTH), w, l == depth - 1)
        x = y.reshape(b, s, d)
    return x
```

```python
import functools
import math

import numpy as np
import jax
import jax.numpy as jnp
from jax import lax
from jax.experimental import pallas as pl
from jax.experimental.pallas import tpu as pltpu

D_MODEL = 1024
HEAD_DIM = 64
N_HEADS = 8
GROUP_WIDTH = N_HEADS * HEAD_DIM
ROPE_DIM = 32
ROPE_HALF = ROPE_DIM // 2
MLA_QK_DIM = HEAD_DIM + ROPE_DIM
Q_RANK = 384
KV_RANK = 256
D_FF = 4096
ROPE_THETA = 10000.0
EPS = 1e-6

OFF_FQ = 0
OFF_FK = OFF_FQ + GROUP_WIDTH
OFF_FV = OFF_FK + GROUP_WIDTH
OFF_FF = OFF_FV + GROUP_WIDTH
OFF_CQ = OFF_FF + N_HEADS
OFF_CKV = OFF_CQ + Q_RANK
OFF_KR = OFF_CKV + KV_RANK
IN_COLS = OFF_KR + ROPE_DIM

LANES = 128
VMEM_LIMIT_BYTES = 56 * 1024 * 1024

LOG2E = math.log2(math.e)
FOX_QSCALE = LOG2E / math.sqrt(HEAD_DIM)
MLA_QSCALE = LOG2E / math.sqrt(MLA_QK_DIM)

PROJ_ROWS = 1024
ATTN_TQ = 512
ATTN_TK = 512
MLP_ROWS = 512
FF_CHUNK = 4096
SCAN_CHUNK = 128
ONES_ROWS = 16
UNROLL_STEPS = 14
FINALIZE_UNROLL = 4

F32 = jnp.float32
BF16 = jnp.bfloat16
NT_DIMS = (((1,), (1,)), ((), ()))


def _rms(x, g):
    return x * lax.rsqrt(jnp.mean(x * x, axis=-1, keepdims=True) + EPS) * g


def _split3(x):
    hi = x.astype(BF16).astype(F32)
    mid = (x - hi).astype(BF16).astype(F32)
    lo = (x - hi - mid).astype(BF16).astype(F32)
    return hi, mid, lo


def _pack3(x):
    hi, mid, lo = _split3(x)
    packed = hi + pltpu.roll(mid, N_HEADS, axis=1) + pltpu.roll(lo, 2 * N_HEADS, axis=1)
    return packed.astype(BF16)


def _rope_table_kernel(pos_ref, freq_ref, cos_ref, sin_ref):
    ang = pos_ref[0].astype(F32) * freq_ref[...]
    cos_ref[0] = jnp.cos(ang)
    sin_ref[0] = jnp.sin(ang)


def _rope_tables(positions, inv_freq):
    b, s = positions.shape
    table = pl.BlockSpec((1, ROPE_HALF, s), lambda i: (i, 0, 0))
    return pl.pallas_call(
        _rope_table_kernel,
        grid=(b,),
        in_specs=[pl.BlockSpec((1, 1, s), lambda i: (i, 0, 0)),
                  pl.BlockSpec((ROPE_HALF, 1), lambda i: (0, 0))],
        out_specs=[table, table],
        out_shape=[jax.ShapeDtypeStruct((b, ROPE_HALF, s), F32)] * 2,
        name="rope_tables",
    )(positions.reshape(b, 1, s), inv_freq.reshape(ROPE_HALF, 1))


def _proj_kernel(x_ref, cos_ref, sin_ref, g_attn_ref, w_rows_ref, w_cols_ref, bff_ref, gq_ref,
                 gkv_ref, wuq_ref, wuk_ref, wuvt_ref, tri_ref,
                 fqt_ref, fk_ref, fvt_ref, faux_ref, mqnt_ref, mqrt_ref, mkn_ref, mkr_ref, mvt_ref,
                 h_sc, ht_sc, carry_sc):
    rows = x_ref.shape[1]

    @pl.when(pl.program_id(1) == 0)
    def _():
        carry_sc[...] = jnp.zeros_like(carry_sc)

    h_sc[...] = _rms(x_ref[0], g_attn_ref[...]).astype(BF16)
    h = h_sc[...]
    ht_sc[...] = h.T

    groups = LANES // ROPE_HALF
    row = lax.broadcasted_iota(jnp.int32, (LANES, rows), 0)
    cos_t = jnp.tile(cos_ref[0], (groups, 1))
    sin_t = jnp.tile(sin_ref[0], (groups, 1))
    sin_s = jnp.where(row < LANES // 2, -sin_t, sin_t)

    def rope_t(xt):
        return xt * cos_t + pltpu.roll(xt, LANES // 2, axis=0) * sin_s


    tk = jnp.dot(h, w_cols_ref[...], preferred_element_type=F32)
    o_cq, o_ckv, o_ff = GROUP_WIDTH, GROUP_WIDTH + Q_RANK, GROUP_WIDTH + Q_RANK + KV_RANK
    fk_ref[0] = tk[:, 0:o_cq].astype(BF16)

    ft = jnp.dot(w_rows_ref[...], ht_sc[...], preferred_element_type=F32)
    fqt_ref[0] = (ft[0:GROUP_WIDTH] * FOX_QSCALE).astype(BF16)
    fvt_ref[0] = ft[GROUP_WIDTH:2 * GROUP_WIDTH].astype(BF16)
    mkr_ref[0] = rope_t(ft[2 * GROUP_WIDTH:2 * GROUP_WIDTH + LANES]).T.astype(BF16)

    lane = lax.broadcasted_iota(jnp.int32, (1, LANES), 1)
    head_lane = lane < N_HEADS
    z = tk[:, o_ff:o_ff + LANES] + bff_ref[...]
    log_f = jnp.where(head_lane, jax.nn.log_sigmoid(z), 0.0)
    tri = tri_ref[...]
    carry = carry_sc[0:1, :]
    for c in range(rows // SCAN_CHUNK):
        part = jnp.dot(tri, _pack3(log_f[c * SCAN_CHUNK:(c + 1) * SCAN_CHUNK, :]),
                       preferred_element_type=F32)
        part = (part + pltpu.roll(part, LANES - N_HEADS, axis=1)
                + pltpu.roll(part, LANES - 2 * N_HEADS, axis=1))
        cum = jnp.where(head_lane, part, 0.0) + carry
        carry = cum[SCAN_CHUNK - 1:SCAN_CHUNK, :]
        faux_ref[0, c * SCAN_CHUNK:(c + 1) * SCAN_CHUNK, :] = _pack3(cum * (-LOG2E))
    carry_sc[...] = jnp.broadcast_to(carry, carry_sc.shape)

    cq = _rms(tk[:, o_cq:o_ckv], gq_ref[...]).astype(BF16)
    qt = jnp.dot(wuq_ref[...], cq.T, preferred_element_type=F32)
    mqnt_ref[0] = (qt[0:GROUP_WIDTH] * MLA_QSCALE).astype(BF16)
    for g in range(2 * N_HEADS * ROPE_HALF // LANES):
        lo = GROUP_WIDTH + g * LANES
        mqrt_ref[0, g * LANES:(g + 1) * LANES, :] = (
            rope_t(qt[lo:lo + LANES]) * MLA_QSCALE).astype(BF16)

    ckv = _rms(tk[:, o_ckv:o_ff], gkv_ref[...]).astype(BF16)
    mkn_ref[0] = jnp.dot(ckv, wuk_ref[...], preferred_element_type=F32).astype(BF16)
    mvt_ref[0] = lax.dot_general(wuvt_ref[...], ckv, NT_DIMS,
                                 preferred_element_type=F32).astype(BF16)


def _const_spec(shape):
    return pl.BlockSpec(shape, lambda *_: (0,) * len(shape))


def _projection(x, cos_t, sin_t, w):
    b, s, _ = x.shape
    tm = PROJ_ROWS
    row_spec = lambda width: pl.BlockSpec((1, tm, width), lambda i, j: (i, j, 0))
    col_spec = lambda height: pl.BlockSpec((1, height, tm), lambda i, j: (i, 0, j))
    consts = [w["g_attn"], w["w_rows"], w["w_cols"], w["bff"], w["gq"], w["gkv"], w["wuq"],
              w["wuk"], w["wuvt"], w["tri"]]
    row_out = lambda width: jax.ShapeDtypeStruct((b, s, width), BF16)
    col_out = lambda height: jax.ShapeDtypeStruct((b, height, s), BF16)
    return pl.pallas_call(
        _proj_kernel,
        grid=(b, s // tm),
        in_specs=[row_spec(D_MODEL), col_spec(ROPE_HALF), col_spec(ROPE_HALF)]
                 + [_const_spec(c.shape) for c in consts],
        out_specs=[col_spec(GROUP_WIDTH), row_spec(GROUP_WIDTH), col_spec(GROUP_WIDTH),
                   row_spec(LANES), col_spec(GROUP_WIDTH), col_spec(2 * LANES),
                   row_spec(GROUP_WIDTH), row_spec(LANES), col_spec(GROUP_WIDTH)],
        out_shape=[col_out(GROUP_WIDTH), row_out(GROUP_WIDTH), col_out(GROUP_WIDTH),
                   row_out(LANES), col_out(GROUP_WIDTH), col_out(2 * LANES),
                   row_out(GROUP_WIDTH), row_out(LANES), col_out(GROUP_WIDTH)],
        scratch_shapes=[pltpu.VMEM((tm, D_MODEL), BF16), pltpu.VMEM((D_MODEL, tm), BF16),
                        pltpu.VMEM((8, LANES), F32)],
        compiler_params=pltpu.CompilerParams(
            dimension_semantics=("arbitrary", "arbitrary"),
            vmem_limit_bytes=VMEM_LIMIT_BYTES),
        name="input_projection",
    )(x, cos_t, sin_t, *consts)


def _attn_kernel(tasks_ref, qm_ref, qa_ref, km_ref, ka_ref, vt_ref, mask_ref, o_ref,
                 bias_sc, qh_sc, s0_sc, s1_sc, acc_sc, m_sc):
    s_len = km_ref.shape[1]
    tq, tk = ATTN_TQ, ATTN_TK

    @pl.when((pl.program_id(0) == 0) & (pl.program_id(1) == 0))
    def _():
        key_idx = lax.broadcasted_iota(jnp.int32, (tk, tq), 0)
        qry_idx = lax.broadcasted_iota(jnp.int32, (tk, tq), 1)
        bias_sc[...] = jnp.where(key_idx <= qry_idx, 0.0, -jnp.inf)

    def kv_tile(k0):
        return jnp.concatenate([km_ref[0, pl.ds(k0, tk), :], ka_ref[0, pl.ds(k0, tk), :]], axis=-1)

    ones_rows = jnp.ones((ONES_ROWS, tk), BF16)

    qcat = jnp.concatenate([qm_ref[0], qa_ref[0]], axis=0)
    for a in range(2):
        qh_sc[a] = qcat * jnp.tile(mask_ref[0, a], (1, s_len // LANES))
    last_tile = s_len // tq - 1
    acc_sc[last_tile] = jnp.zeros(acc_sc.shape[1:], F32)
    m_sc[last_tile] = jnp.full(m_sc.shape[1:], -jnp.inf, F32)

    def produce(dst, t, diagonal):
        q0 = pl.multiple_of(tasks_ref[0, t] * tq, tq)
        k0 = pl.multiple_of(tasks_ref[1, t] * tk, tk)
        q_pair = jnp.concatenate([qh_sc[a, :, pl.ds(q0, tq)] for a in range(2)], axis=1)
        sc = jnp.dot(kv_tile(k0), q_pair, preferred_element_type=F32)
        maxima = []
        for a in range(2):
            sc_a = sc[:, a * tq:(a + 1) * tq]
            if diagonal:
                sc_a = sc_a + bias_sc[...]
            dst[a] = sc_a
            maxima.append(jnp.max(sc_a, axis=0, keepdims=True))
        return tuple(maxima)

    def consume(src, t, maxima):
        i = tasks_ref[0, t]
        k0 = pl.multiple_of(tasks_ref[1, t] * tk, tk)
        alphas, ps = [], []
        for a in range(2):
            m_run = m_sc[i, a]
            m_new = jnp.maximum(m_run, maxima[a])
            alphas.append(jnp.exp2(m_run - m_new))
            ps.append(jnp.exp2(src[a] - m_new).astype(BF16))
            m_sc[i, a] = m_new
        vt = jnp.concatenate([vt_ref[0, :, pl.ds(k0, tk)], ones_rows], axis=0)
        pv = jnp.dot(vt, jnp.concatenate(ps, axis=1), preferred_element_type=F32)
        for a in range(2):
            pv_a = jnp.concatenate([pv[a * HEAD_DIM:(a + 1) * HEAD_DIM, a * tq:(a + 1) * tq],
                                    pv[2 * HEAD_DIM:, a * tq:(a + 1) * tq]], axis=0)
            acc_sc[i, a] = alphas[a] * acc_sc[i, a] + pv_a

    half = tq // 2

    def produce_diag(dst, i):
        q0 = i * tq
        kcat = kv_tile(q0)
        q_all = jnp.concatenate([qh_sc[a, :, q0:q0 + tq] for a in range(2)], axis=1)
        q_late = jnp.concatenate([qh_sc[a, :, q0 + half:q0 + tq] for a in range(2)], axis=1)
        top = jnp.dot(kcat[0:half], q_all, preferred_element_type=F32)
        bot = jnp.dot(kcat[half:tk], q_late, preferred_element_type=F32)
        maxima = []
        for a in range(2):
            top_a = top[:, a * tq:(a + 1) * tq] + bias_sc[0:half, :]
            bot_a = bot[:, a * half:(a + 1) * half] + bias_sc[half:tk, half:tq]
            dst[a, 0:half, :] = top_a
            dst[a, half:tk, half:tq] = bot_a
            mx_top = jnp.max(top_a, axis=0, keepdims=True)
            mx_bot = jnp.max(bot_a, axis=0, keepdims=True)
            maxima.append(jnp.concatenate(
                [mx_top[:, 0:half], jnp.maximum(mx_top[:, half:tq], mx_bot)], axis=1))
        return tuple(maxima)

    def consume_diag(src, i, maxima):
        q0 = i * tq
        early, late = [], []
        for a in range(2):
            p_top = jnp.exp2(src[a, 0:half, :] - maxima[a]).astype(BF16)
            p_bot = jnp.exp2(src[a, half:tk, half:tq] - maxima[a][:, half:tq]).astype(BF16)
            early.append(p_top[:, 0:half])
            late.append(jnp.concatenate([p_top[:, half:tq], p_bot], axis=0))
        vt = jnp.concatenate([vt_ref[0, :, q0:q0 + tk], ones_rows], axis=0)
        pv_early = jnp.dot(vt[:, 0:half], jnp.concatenate(early, axis=1),
                           preferred_element_type=F32)
        pv_late = jnp.dot(vt, jnp.concatenate(late, axis=1), preferred_element_type=F32)
        for a in range(2):
            pv_a = jnp.concatenate([pv_early[:, a * half:(a + 1) * half],
                                    pv_late[:, a * half:(a + 1) * half]], axis=1)
            acc_sc[i, a] = jnp.concatenate(
                [pv_a[a * HEAD_DIM:(a + 1) * HEAD_DIM], pv_a[2 * HEAD_DIM:]], axis=0)
            m_sc[i, a] = maxima[a]

    bufs = (s0_sc, s1_sc)

    def step(t, parity, diagonal, mx):
        mx_next = produce(bufs[1 - parity], t + 1, diagonal)
        consume(bufs[parity], t, mx)
        return mx_next

    def steps(first, count, diagonal):
        def body(it, mx):
            t = first + it * count
            for u in range(count):
                mx = step(t + u, (first + u) % 2, diagonal, mx)
            return mx
        return body

    n_diag = s_len // tq
    n_tasks = n_diag * (n_diag + 1) // 2
    n_off = n_tasks - n_diag
    assert n_diag % 2 == 0 and n_off % UNROLL_STEPS == 0 and UNROLL_STEPS % 2 == 0
    assert tq == tk
    mx = produce_diag(s0_sc, 0)
    for t in range(n_diag - 1):
        if t + 1 < n_diag - 1:
            mx_next = produce_diag(bufs[(t + 1) % 2], t + 1)
        else:
            mx_next = produce(bufs[(t + 1) % 2], t + 1, True)
        consume_diag(bufs[t % 2], t, mx)
        mx = mx_next
    mx = lax.fori_loop(0, n_off // UNROLL_STEPS, steps(n_diag - 1, UNROLL_STEPS, False), mx)
    consume(bufs[(n_tasks - 1) % 2], n_tasks - 1, mx)

    def finalize(it, _):
        for u in range(FINALIZE_UNROLL):
            i = it * FINALIZE_UNROLL + u
            q0 = pl.multiple_of(i * tq, tq)
            out_t = jnp.concatenate(
                [acc_sc[i, a, 0:HEAD_DIM, :] / acc_sc[i, a, HEAD_DIM:HEAD_DIM + 1, :]
                 for a in range(2)], axis=0)
            o_ref[0, pl.ds(q0, tq), :] = out_t.T
        return 0

    assert n_diag % FINALIZE_UNROLL == 0
    lax.fori_loop(0, n_diag // FINALIZE_UNROLL, finalize, 0)


def _attention(qmt, qat, km, ka, vt, mask, qa_batched, qa_per_pairs):
    b, s, _ = km.shape
    pairs = N_HEADS // 2
    tok = lambda f: pl.BlockSpec((1, s, LANES), f)
    feat = lambda f: pl.BlockSpec((1, LANES, s), f)
    qa_map = (lambda i, p: (i, p // qa_per_pairs, 0)) if qa_batched else (lambda i, p: (0, 0, 0))
    n_q = s // ATTN_TQ
    task_list = [(i, i) for i in range(n_q)] + [(i, j) for i in range(n_q) for j in range(i)]
    tasks = jnp.asarray(np.array(task_list, np.int32).T)
    return pl.pallas_call(
        _attn_kernel,
        grid=(b, pairs),
        in_specs=[pl.BlockSpec(memory_space=pltpu.SMEM),
                  feat(lambda i, p: (i, p, 0)), feat(qa_map), tok(lambda i, p: (i, 0, p)),
                  tok(lambda i, p: (i, 0, 0)), feat(lambda i, p: (i, p, 0)),
                  pl.BlockSpec((1, 2, 2 * LANES, LANES), lambda i, p: (p, 0, 0, 0))],
        out_specs=tok(lambda i, p: (i, 0, p)),
        out_shape=jax.ShapeDtypeStruct((b, s, GROUP_WIDTH), F32),
        scratch_shapes=[pltpu.VMEM((ATTN_TK, ATTN_TQ), F32),
                        pltpu.VMEM((2, 2 * LANES, s), BF16),
                        pltpu.VMEM((2, ATTN_TK, ATTN_TQ), F32),
                        pltpu.VMEM((2, ATTN_TK, ATTN_TQ), F32),
                        pltpu.VMEM((n_q, 2, HEAD_DIM + ONES_ROWS, ATTN_TQ), F32),
                        pltpu.VMEM((n_q, 2, 1, ATTN_TQ), F32)],
        compiler_params=pltpu.CompilerParams(
            dimension_semantics=("arbitrary", "arbitrary"),
            vmem_limit_bytes=VMEM_LIMIT_BYTES),
        name="causal_attention",
    )(tasks, qmt, qat, km, ka, vt, mask)


def _mlp_kernel(x_ref, fox_ref, mla_ref, gfox_ref, gmla_ref, wo_ref, gmlp_ref, wup_ref,
                wdown_ref, gfin_ref, o_ref, *, final):
    mixed = jnp.concatenate([_rms(fox_ref[...], gfox_ref[...]).astype(BF16),
                             _rms(mla_ref[...], gmla_ref[...]).astype(BF16)], axis=-1)
    x1 = x_ref[...] + jnp.dot(mixed, wo_ref[...], preferred_element_type=F32)
    h = _rms(x1, gmlp_ref[...]).astype(BF16)
    y = x1
    for c in range(D_FF // FF_CHUNK):
        u = jnp.dot(h, wup_ref[:, c * FF_CHUNK:(c + 1) * FF_CHUNK], preferred_element_type=F32)
        act = jnp.square(jnp.maximum(u, 0.0)).astype(BF16)
        y = y + jnp.dot(act, wdown_ref[c * FF_CHUNK:(c + 1) * FF_CHUNK, :],
                        preferred_element_type=F32)
    o_ref[...] = _rms(y, gfin_ref[...]) if final else y


def _out_mlp(x2d, fox, mla, w, final):
    n = x2d.shape[0]
    tm = MLP_ROWS
    row = lambda width: pl.BlockSpec((tm, width), lambda i: (i, 0))
    resident = lambda a: pl.BlockSpec(a.shape, lambda i: (0,) * a.ndim,
                                      pipeline_mode=pl.Buffered(1))
    consts = [w["gfox"], w["gmla"], w["wo"], w["gmlp"], w["wup"], w["wdown"], w["gfin"]]
    return pl.pallas_call(
        functools.partial(_mlp_kernel, final=final),
        grid=(n // tm,),
        in_specs=[row(D_MODEL), row(GROUP_WIDTH), row(GROUP_WIDTH)] + [resident(c) for c in consts],
        out_specs=row(D_MODEL),
        out_shape=jax.ShapeDtypeStruct((n, D_MODEL), F32),
        compiler_params=pltpu.CompilerParams(
            dimension_semantics=("arbitrary",),
            vmem_limit_bytes=VMEM_LIMIT_BYTES),
        name="out_proj_mlp",
    )(x2d, fox, mla, *consts)


def _rope_split_cols(wr):
    k = wr.shape[0]
    heads = wr.shape[1]
    per_group = LANES // ROPE_DIM
    first = wr[:, :, :ROPE_HALF].reshape(k, heads // per_group, per_group * ROPE_HALF)
    second = wr[:, :, ROPE_HALF:].reshape(k, heads // per_group, per_group * ROPE_HALF)
    return jnp.concatenate([first, second], axis=-1).reshape(k, heads * ROPE_DIM)


def _prep_weights(attn_norm_g, w_in, b_forget, q_norm_g, w_uq, kv_norm_g, w_ukv, fox_out_g,
                  mla_out_g, w_o, mlp_norm_g, w_up, w_down, final_norm_g):
    row = lambda v: v.reshape(1, -1).astype(F32)
    pad_lanes = lambda a: jnp.pad(a, ((0, 0), (0, LANES - a.shape[1])))
    wkr = w_in[:, OFF_KR:IN_COLS].reshape(D_MODEL, 1, ROPE_DIM)
    wkr = _rope_split_cols(jnp.broadcast_to(wkr, (D_MODEL, LANES // ROPE_DIM, ROPE_DIM)))
    uq = w_uq.reshape(Q_RANK, N_HEADS, MLA_QK_DIM)
    ukv = w_ukv.reshape(KV_RANK, N_HEADS, 2 * HEAD_DIM)
    tri = np.tril(np.ones((SCAN_CHUNK, SCAN_CHUNK), np.float32))
    w_rows = jnp.concatenate([w_in[:, OFF_FQ:OFF_FK], w_in[:, OFF_FV:OFF_FF], wkr], axis=1)
    w_cols = jnp.concatenate([w_in[:, OFF_FK:OFF_FV], w_in[:, OFF_CQ:OFF_CKV],
                              w_in[:, OFF_CKV:OFF_KR], pad_lanes(w_in[:, OFF_FF:OFF_CQ])], axis=1)
    wuq = jnp.concatenate([uq[:, :, :HEAD_DIM].reshape(Q_RANK, GROUP_WIDTH),
                           _rope_split_cols(uq[:, :, HEAD_DIM:])], axis=1)
    return {
        "g_attn": row(attn_norm_g),
        "w_rows": w_rows.T.astype(BF16),
        "w_cols": w_cols.astype(BF16),
        "bff": pad_lanes(row(b_forget)),
        "gq": row(q_norm_g),
        "gkv": row(kv_norm_g),
        "wuq": wuq.T.astype(BF16),
        "wuk": ukv[:, :, :HEAD_DIM].reshape(KV_RANK, GROUP_WIDTH).astype(BF16),
        "wuvt": ukv[:, :, HEAD_DIM:].reshape(KV_RANK, GROUP_WIDTH).T.astype(BF16),
        "tri": jnp.asarray(tri, BF16),
        "gfox": row(fox_out_g),
        "gmla": row(mla_out_g),
        "wo": w_o.astype(BF16),
        "gmlp": row(mlp_norm_g),
        "wup": w_up.astype(BF16),
        "wdown": w_down.astype(BF16),
        "gfin": row(final_norm_g),
    }


def _head_masks():
    pairs = N_HEADS // 2
    fox = np.zeros((pairs, 2, 2 * LANES), np.float32)
    mla = np.zeros((pairs, 2, 2 * LANES), np.float32)
    per_group = LANES // ROPE_DIM
    for p in range(pairs):
        for a in range(2):
            h = 2 * p + a
            fox[p, a, a * HEAD_DIM:(a + 1) * HEAD_DIM] = 1.0
            mla[p, a, a * HEAD_DIM:(a + 1) * HEAD_DIM] = 1.0
            for piece in range(3):
                fox[p, a, LANES + piece * N_HEADS + h] = 1.0
            hh = h % per_group
            for half in range(2):
                lo = LANES + half * (LANES // 2) + hh * ROPE_HALF
                mla[p, a, lo:lo + ROPE_HALF] = 1.0
    rep = lambda m: jnp.asarray(np.repeat(m[..., None], LANES, axis=-1), BF16)
    return rep(fox), rep(mla)


def kernel(x, positions, attn_norm_g, w_in, b_forget, q_norm_g, w_uq, kv_norm_g, w_ukv,
           fox_out_g, mla_out_g, w_o, mlp_norm_g, w_up, w_down, final_norm_g):
    b, s, d = x.shape
    depth = w_in.shape[0]
    inv_freq = ROPE_THETA ** (-jnp.arange(0, ROPE_DIM, 2, dtype=F32) / ROPE_DIM)
    cos_t, sin_t = _rope_tables(positions, inv_freq)
    fox_mask, mla_mask = _head_masks()
    ones_aux = jnp.ones((1, LANES, s), BF16)

    for l in range(depth):
        w = _prep_weights(attn_norm_g[l], w_in[l], b_forget[l], q_norm_g[l], w_uq[l],
                          kv_norm_g[l], w_ukv[l], fox_out_g[l], mla_out_g[l], w_o[l],
                          mlp_norm_g[l], w_up[l], w_down[l], final_norm_g)
        fqt, fk, fvt, faux, mqnt, mqrt, mkn, mkr, mvt = _projection(x, cos_t, sin_t, w)
        fox = _attention(fqt, ones_aux, fk, faux, fvt, fox_mask, False, 1)
        mla = _attention(mqnt, mqrt, mkn, mkr, mvt, mla_mask, True, 2)
        y = _out_mlp(x.reshape(b * s, d), fox.reshape(b * s, GROUP_WIDTH),
                     mla.reshape(b * s, GROUP_WIDTH), w, l == depth - 1)
        x = y.reshape(b, s, d)
    return x
```

```python
import functools
import math

import numpy as np
import jax
import jax.numpy as jnp
from jax import lax
from jax.experimental import pallas as pl
from jax.experimental.pallas import tpu as pltpu

D_MODEL = 1024
HEAD_DIM = 64
N_HEADS = 8
GROUP_WIDTH = N_HEADS * HEAD_DIM
ROPE_DIM = 32
ROPE_HALF = ROPE_DIM // 2
MLA_QK_DIM = HEAD_DIM + ROPE_DIM
Q_RANK = 384
KV_RANK = 256
D_FF = 4096
ROPE_THETA = 10000.0
EPS = 1e-6

OFF_FQ = 0
OFF_FK = OFF_FQ + GROUP_WIDTH
OFF_FV = OFF_FK + GROUP_WIDTH
OFF_FF = OFF_FV + GROUP_WIDTH
OFF_CQ = OFF_FF + N_HEADS
OFF_CKV = OFF_CQ + Q_RANK
OFF_KR = OFF_CKV + KV_RANK
IN_COLS = OFF_KR + ROPE_DIM

LANES = 128
VMEM_LIMIT_BYTES = 56 * 1024 * 1024

LOG2E = math.log2(math.e)
FOX_QSCALE = LOG2E / math.sqrt(HEAD_DIM)
MLA_QSCALE = LOG2E / math.sqrt(MLA_QK_DIM)

PROJ_ROWS = 1024
ATTN_TQ = 512
ATTN_TK = 512
MLP_ROWS = 512
FF_CHUNK = 4096
SCAN_CHUNK = 128
ONES_ROWS = 16
UNROLL_STEPS = 14
FINALIZE_UNROLL = 4

F32 = jnp.float32
BF16 = jnp.bfloat16
NT_DIMS = (((1,), (1,)), ((), ()))


def _rms(x, g):
    return x * lax.rsqrt(jnp.mean(x * x, axis=-1, keepdims=True) + EPS) * g


def _split3(x):
    hi = x.astype(BF16).astype(F32)
    mid = (x - hi).astype(BF16).astype(F32)
    lo = (x - hi - mid).astype(BF16).astype(F32)
    return hi, mid, lo


def _pack3(x):
    hi, mid, lo = _split3(x)
    packed = hi + pltpu.roll(mid, N_HEADS, axis=1) + pltpu.roll(lo, 2 * N_HEADS, axis=1)
    return packed.astype(BF16)


def _rope_table_kernel(pos_ref, freq_ref, cos_ref, sin_ref):
    ang = pos_ref[0].astype(F32) * freq_ref[...]
    cos_ref[0] = jnp.cos(ang)
    sin_ref[0] = jnp.sin(ang)


def _rope_tables(positions, inv_freq):
    b, s = positions.shape
    table = pl.BlockSpec((1, ROPE_HALF, s), lambda i: (i, 0, 0))
    return pl.pallas_call(
        _rope_table_kernel,
        grid=(b,),
        in_specs=[pl.BlockSpec((1, 1, s), lambda i: (i, 0, 0)),
                  pl.BlockSpec((ROPE_HALF, 1), lambda i: (0, 0))],
        out_specs=[table, table],
        out_shape=[jax.ShapeDtypeStruct((b, ROPE_HALF, s), F32)] * 2,
        name="rope_tables",
    )(positions.reshape(b, 1, s), inv_freq.reshape(ROPE_HALF, 1))


def _proj_kernel(x_ref, cos_ref, sin_ref, g_attn_ref, w_rows_ref, w_cols_ref, bff_ref, gq_ref,
                 gkv_ref, wuq_ref, wuk_ref, wuvt_ref, tri_ref,
                 fqt_ref, fk_ref, fvt_ref, faux_ref, mqnt_ref, mqrt_ref, mkn_ref, mkr_ref, mvt_ref,
                 h_sc, carry_sc):
    rows = x_ref.shape[1]

    @pl.when(pl.program_id(1) == 0)
    def _():
        carry_sc[...] = jnp.zeros_like(carry_sc)

    h_sc[...] = _rms(x_ref[0], g_attn_ref[...]).astype(BF16)
    h = h_sc[...]

    groups = LANES // ROPE_HALF
    row = lax.broadcasted_iota(jnp.int32, (LANES, rows), 0)
    cos_t = jnp.tile(cos_ref[0], (groups, 1))
    sin_t = jnp.tile(sin_ref[0], (groups, 1))
    sin_s = jnp.where(row < LANES // 2, -sin_t, sin_t)

    def rope_t(xt):
        return xt * cos_t + pltpu.roll(xt, LANES // 2, axis=0) * sin_s


    tk = jnp.dot(h, w_cols_ref[...], preferred_element_type=F32)
    o_cq, o_ckv, o_ff = GROUP_WIDTH, GROUP_WIDTH + Q_RANK, GROUP_WIDTH + Q_RANK + KV_RANK
    fk_ref[0] = tk[:, 0:o_cq].astype(BF16)

    ft = lax.dot_general(w_rows_ref[...], h, NT_DIMS, preferred_element_type=F32)
    fqt_ref[0] = (ft[0:GROUP_WIDTH] * FOX_QSCALE).astype(BF16)
    fvt_ref[0] = ft[GROUP_WIDTH:2 * GROUP_WIDTH].astype(BF16)
    mkr_ref[0] = rope_t(ft[2 * GROUP_WIDTH:2 * GROUP_WIDTH + LANES]).T.astype(BF16)

    lane = lax.broadcasted_iota(jnp.int32, (1, LANES), 1)
    head_lane = lane < N_HEADS
    z = tk[:, o_ff:o_ff + LANES] + bff_ref[...]
    log_f = jnp.where(head_lane, jax.nn.log_sigmoid(z), 0.0)
    tri = tri_ref[...]
    carry = carry_sc[0:1, :]
    for c in range(rows // SCAN_CHUNK):
        part = jnp.dot(tri, _pack3(log_f[c * SCAN_CHUNK:(c + 1) * SCAN_CHUNK, :]),
                       preferred_element_type=F32)
        part = (part + pltpu.roll(part, LANES - N_HEADS, axis=1)
                + pltpu.roll(part, LANES - 2 * N_HEADS, axis=1))
        cum = jnp.where(head_lane, part, 0.0) + carry
        carry = cum[SCAN_CHUNK - 1:SCAN_CHUNK, :]
        faux_ref[0, c * SCAN_CHUNK:(c + 1) * SCAN_CHUNK, :] = _pack3(cum * (-LOG2E))
    carry_sc[...] = jnp.broadcast_to(carry, carry_sc.shape)

    cq = _rms(tk[:, o_cq:o_ckv], gq_ref[...]).astype(BF16)
    qt = lax.dot_general(wuq_ref[...], cq, NT_DIMS, preferred_element_type=F32)
    mqnt_ref[0] = (qt[0:GROUP_WIDTH] * MLA_QSCALE).astype(BF16)
    for g in range(2 * N_HEADS * ROPE_HALF // LANES):
        lo = GROUP_WIDTH + g * LANES
        mqrt_ref[0, g * LANES:(g + 1) * LANES, :] = (
            rope_t(qt[lo:lo + LANES]) * MLA_QSCALE).astype(BF16)

    ckv = _rms(tk[:, o_ckv:o_ff], gkv_ref[...]).astype(BF16)
    mkn_ref[0] = jnp.dot(ckv, wuk_ref[...], preferred_element_type=F32).astype(BF16)
    mvt_ref[0] = lax.dot_general(wuvt_ref[...], ckv, NT_DIMS,
                                 preferred_element_type=F32).astype(BF16)


def _const_spec(shape):
    return pl.BlockSpec(shape, lambda *_: (0,) * len(shape))


def _projection(x, cos_t, sin_t, w):
    b, s, _ = x.shape
    tm = PROJ_ROWS
    row_spec = lambda width: pl.BlockSpec((1, tm, width), lambda i, j: (i, j, 0))
    col_spec = lambda height: pl.BlockSpec((1, height, tm), lambda i, j: (i, 0, j))
    consts = [w["g_attn"], w["w_rows"], w["w_cols"], w["bff"], w["gq"], w["gkv"], w["wuq"],
              w["wuk"], w["wuvt"], w["tri"]]
    row_out = lambda width: jax.ShapeDtypeStruct((b, s, width), BF16)
    col_out = lambda height: jax.ShapeDtypeStruct((b, height, s), BF16)
    return pl.pallas_call(
        _proj_kernel,
        grid=(b, s // tm),
        in_specs=[row_spec(D_MODEL), col_spec(ROPE_HALF), col_spec(ROPE_HALF)]
                 + [_const_spec(c.shape) for c in consts],
        out_specs=[col_spec(GROUP_WIDTH), row_spec(GROUP_WIDTH), col_spec(GROUP_WIDTH),
                   row_spec(LANES), col_spec(GROUP_WIDTH), col_spec(2 * LANES),
                   row_spec(GROUP_WIDTH), row_spec(LANES), col_spec(GROUP_WIDTH)],
        out_shape=[col_out(GROUP_WIDTH), row_out(GROUP_WIDTH), col_out(GROUP_WIDTH),
                   row_out(LANES), col_out(GROUP_WIDTH), col_out(2 * LANES),
                   row_out(GROUP_WIDTH), row_out(LANES), col_out(GROUP_WIDTH)],
        scratch_shapes=[pltpu.VMEM((tm, D_MODEL), BF16), pltpu.VMEM((8, LANES), F32)],
        compiler_params=pltpu.CompilerParams(
            dimension_semantics=("arbitrary", "arbitrary"),
            vmem_limit_bytes=VMEM_LIMIT_BYTES),
        name="input_projection",
    )(x, cos_t, sin_t, *consts)


def _attn_kernel(tasks_ref, qm_ref, qa_ref, km_ref, ka_ref, vt_ref, mask_ref, o_ref,
                 bias_sc, qh_sc, s0_sc, s1_sc, acc_sc, m_sc):
    s_len = km_ref.shape[1]
    tq, tk = ATTN_TQ, ATTN_TK

    @pl.when((pl.program_id(0) == 0) & (pl.program_id(1) == 0))
    def _():
        key_idx = lax.broadcasted_iota(jnp.int32, (tk, tq), 0)
        qry_idx = lax.broadcasted_iota(jnp.int32, (tk, tq), 1)
        bias_sc[...] = jnp.where(key_idx <= qry_idx, 0.0, -jnp.inf)

    def kv_tile(k0):
        return jnp.concatenate([km_ref[0, pl.ds(k0, tk), :], ka_ref[0, pl.ds(k0, tk), :]], axis=-1)

    ones_rows = jnp.ones((ONES_ROWS, tk), BF16)

    qcat = jnp.concatenate([qm_ref[0], qa_ref[0]], axis=0)
    for a in range(2):
        qh_sc[a] = qcat * jnp.tile(mask_ref[0, a], (1, s_len // LANES))
    last_tile = s_len // tq - 1
    acc_sc[last_tile] = jnp.zeros(acc_sc.shape[1:], F32)
    m_sc[last_tile] = jnp.full(m_sc.shape[1:], -jnp.inf, F32)

    def produce(dst, t, diagonal):
        q0 = pl.multiple_of(tasks_ref[0, t] * tq, tq)
        k0 = pl.multiple_of(tasks_ref[1, t] * tk, tk)
        q_pair = jnp.concatenate([qh_sc[a, :, pl.ds(q0, tq)] for a in range(2)], axis=1)
        sc = jnp.dot(kv_tile(k0), q_pair, preferred_element_type=F32)
        maxima = []
        for a in range(2):
            sc_a = sc[:, a * tq:(a + 1) * tq]
            if diagonal:
                sc_a = sc_a + bias_sc[...]
            dst[a] = sc_a
            maxima.append(jnp.max(sc_a, axis=0, keepdims=True))
        return tuple(maxima)

    def consume(src, t, maxima):
        i = tasks_ref[0, t]
        k0 = pl.multiple_of(tasks_ref[1, t] * tk, tk)
        alphas, ps = [], []
        for a in range(2):
            m_run = m_sc[i, a]
            m_new = jnp.maximum(m_run, maxima[a])
            alphas.append(jnp.exp2(m_run - m_new))
            ps.append(jnp.exp2(src[a] - m_new).astype(BF16))
            m_sc[i, a] = m_new
        vt = jnp.concatenate([vt_ref[0, :, pl.ds(k0, tk)], ones_rows], axis=0)
        pv = jnp.dot(vt, jnp.concatenate(ps, axis=1), preferred_element_type=F32)
        for a in range(2):
            pv_a = jnp.concatenate([pv[a * HEAD_DIM:(a + 1) * HEAD_DIM, a * tq:(a + 1) * tq],
                                    pv[2 * HEAD_DIM:, a * tq:(a + 1) * tq]], axis=0)
            acc_sc[i, a] = alphas[a] * acc_sc[i, a] + pv_a

    half = tq // 2

    def produce_diag(dst, i):
        q0 = i * tq
        kcat = kv_tile(q0)
        q_all = jnp.concatenate([qh_sc[a, :, q0:q0 + tq] for a in range(2)], axis=1)
        q_late = jnp.concatenate([qh_sc[a, :, q0 + half:q0 + tq] for a in range(2)], axis=1)
        top = jnp.dot(kcat[0:half], q_all, preferred_element_type=F32)
        bot = jnp.dot(kcat[half:tk], q_late, preferred_element_type=F32)
        maxima = []
        for a in range(2):
            top_a = top[:, a * tq:(a + 1) * tq] + bias_sc[0:half, :]
            bot_a = bot[:, a * half:(a + 1) * half] + bias_sc[half:tk, half:tq]
            dst[a, 0:half, :] = top_a
            dst[a, half:tk, half:tq] = bot_a
            mx_top = jnp.max(top_a, axis=0, keepdims=True)
            mx_bot = jnp.max(bot_a, axis=0, keepdims=True)
            maxima.append(jnp.concatenate(
                [mx_top[:, 0:half], jnp.maximum(mx_top[:, half:tq], mx_bot)], axis=1))
        return tuple(maxima)

    def consume_diag(src, i, maxima):
        q0 = i * tq
        early, late = [], []
        for a in range(2):
            p_top = jnp.exp2(src[a, 0:half, :] - maxima[a]).astype(BF16)
            p_bot = jnp.exp2(src[a, half:tk, half:tq] - maxima[a][:, half:tq]).astype(BF16)
            early.append(p_top[:, 0:half])
            late.append(jnp.concatenate([p_top[:, half:tq], p_bot], axis=0))
        vt = jnp.concatenate([vt_ref[0, :, q0:q0 + tk], ones_rows], axis=0)
        pv_early = jnp.dot(vt[:, 0:half], jnp.concatenate(early, axis=1),
                           preferred_element_type=F32)
        pv_late = jnp.dot(vt, jnp.concatenate(late, axis=1), preferred_element_type=F32)
        for a in range(2):
            pv_a = jnp.concatenate([pv_early[:, a * half:(a + 1) * half],
                                    pv_late[:, a * half:(a + 1) * half]], axis=1)
            acc_sc[i, a] = jnp.concatenate(
                [pv_a[a * HEAD_DIM:(a + 1) * HEAD_DIM], pv_a[2 * HEAD_DIM:]], axis=0)
            m_sc[i, a] = maxima[a]

    bufs = (s0_sc, s1_sc)

    def step(t, parity, diagonal, mx):
        mx_next = produce(bufs[1 - parity], t + 1, diagonal)
        consume(bufs[parity], t, mx)
        return mx_next

    def steps(first, count, diagonal):
        def body(it, mx):
            t = first + it * count
            for u in range(count):
                mx = step(t + u, (first + u) % 2, diagonal, mx)
            return mx
        return body

    n_diag = s_len // tq
    n_tasks = n_diag * (n_diag + 1) // 2
    n_off = n_tasks - n_diag
    assert n_diag % 2 == 0 and n_off % UNROLL_STEPS == 0 and UNROLL_STEPS % 2 == 0
    assert tq == tk
    mx = produce_diag(s0_sc, 0)
    for t in range(n_diag - 1):
        if t + 1 < n_diag - 1:
            mx_next = produce_diag(bufs[(t + 1) % 2], t + 1)
        else:
            mx_next = produce(bufs[(t + 1) % 2], t + 1, True)
        consume_diag(bufs[t % 2], t, mx)
        mx = mx_next
    mx = lax.fori_loop(0, n_off // UNROLL_STEPS, steps(n_diag - 1, UNROLL_STEPS, False), mx)
    consume(bufs[(n_tasks - 1) % 2], n_tasks - 1, mx)

    def finalize(it, _):
        for u in range(FINALIZE_UNROLL):
            i = it * FINALIZE_UNROLL + u
            q0 = pl.multiple_of(i * tq, tq)
            out_t = jnp.concatenate(
                [acc_sc[i, a, 0:HEAD_DIM, :] / acc_sc[i, a, HEAD_DIM:HEAD_DIM + 1, :]
                 for a in range(2)], axis=0)
            o_ref[0, :, pl.ds(q0, tq)] = out_t
        return 0

    assert n_diag % FINALIZE_UNROLL == 0
    lax.fori_loop(0, n_diag // FINALIZE_UNROLL, finalize, 0)


def _attention(qmt, qat, km, ka, vt, mask, qa_batched, qa_per_pairs):
    b, s, _ = km.shape
    pairs = N_HEADS // 2
    tok = lambda f: pl.BlockSpec((1, s, LANES), f)
    feat = lambda f: pl.BlockSpec((1, LANES, s), f)
    qa_map = (lambda i, p: (i, p // qa_per_pairs, 0)) if qa_batched else (lambda i, p: (0, 0, 0))
    n_q = s // ATTN_TQ
    task_list = [(i, i) for i in range(n_q)] + [(i, j) for i in range(n_q) for j in range(i)]
    tasks = jnp.asarray(np.array(task_list, np.int32).T)
    return pl.pallas_call(
        _attn_kernel,
        grid=(b, pairs),
        in_specs=[pl.BlockSpec(memory_space=pltpu.SMEM),
                  feat(lambda i, p: (i, p, 0)), feat(qa_map), tok(lambda i, p: (i, 0, p)),
                  tok(lambda i, p: (i, 0, 0)), feat(lambda i, p: (i, p, 0)),
                  pl.BlockSpec((1, 2, 2 * LANES, LANES), lambda i, p: (p, 0, 0, 0))],
        out_specs=feat(lambda i, p: (i, p, 0)),
        out_shape=jax.ShapeDtypeStruct((b, GROUP_WIDTH, s), F32),
        scratch_shapes=[pltpu.VMEM((ATTN_TK, ATTN_TQ), F32),
                        pltpu.VMEM((2, 2 * LANES, s), BF16),
                        pltpu.VMEM((2, ATTN_TK, ATTN_TQ), F32),
                        pltpu.VMEM((2, ATTN_TK, ATTN_TQ), F32),
                        pltpu.VMEM((n_q, 2, HEAD_DIM + ONES_ROWS, ATTN_TQ), F32),
                        pltpu.VMEM((n_q, 2, 1, ATTN_TQ), F32)],
        compiler_params=pltpu.CompilerParams(
            dimension_semantics=("arbitrary", "arbitrary"),
            vmem_limit_bytes=VMEM_LIMIT_BYTES),
        name="causal_attention",
    )(tasks, qmt, qat, km, ka, vt, mask)


def _mlp_kernel(x_ref, fox_ref, mla_ref, gfox_ref, gmla_ref, wo_ref, gmlp_ref, wup_ref,
                wdown_ref, gfin_ref, o_ref, *, final):
    def rms_features(v_t, g_col):
        ms = jnp.mean(v_t * v_t, axis=0, keepdims=True)
        return v_t * lax.rsqrt(ms + EPS) * g_col

    mixed_t = jnp.concatenate([rms_features(fox_ref[0], gfox_ref[...]).astype(BF16),
                               rms_features(mla_ref[0], gmla_ref[...]).astype(BF16)], axis=0)
    x1 = x_ref[...] + lax.dot_general(mixed_t, wo_ref[...], (((0,), (0,)), ((), ())),
                                      preferred_element_type=F32)
    h = _rms(x1, gmlp_ref[...]).astype(BF16)
    y = x1
    for c in range(D_FF // FF_CHUNK):
        u = jnp.dot(h, wup_ref[:, c * FF_CHUNK:(c + 1) * FF_CHUNK], preferred_element_type=F32)
        act = jnp.square(jnp.maximum(u, 0.0)).astype(BF16)
        y = y + jnp.dot(act, wdown_ref[c * FF_CHUNK:(c + 1) * FF_CHUNK, :],
                        preferred_element_type=F32)
    o_ref[...] = _rms(y, gfin_ref[...]) if final else y


def _out_mlp(x2d, fox_t, mla_t, w, final):
    n = x2d.shape[0]
    tm = MLP_ROWS
    tiles_per_seq = fox_t.shape[2] // tm
    row = lambda width: pl.BlockSpec((tm, width), lambda i: (i, 0))
    feat = pl.BlockSpec((1, GROUP_WIDTH, tm), lambda i: (i // tiles_per_seq, 0, i % tiles_per_seq))
    resident = lambda a: pl.BlockSpec(a.shape, lambda i: (0,) * a.ndim,
                                      pipeline_mode=pl.Buffered(1))
    consts = [w["gfox"], w["gmla"], w["wo"], w["gmlp"], w["wup"], w["wdown"], w["gfin"]]
    return pl.pallas_call(
        functools.partial(_mlp_kernel, final=final),
        grid=(n // tm,),
        in_specs=[row(D_MODEL), feat, feat] + [resident(c) for c in consts],
        out_specs=row(D_MODEL),
        out_shape=jax.ShapeDtypeStruct((n, D_MODEL), F32),
        compiler_params=pltpu.CompilerParams(
            dimension_semantics=("arbitrary",),
            vmem_limit_bytes=VMEM_LIMIT_BYTES),
        name="out_proj_mlp",
    )(x2d, fox_t, mla_t, *consts)


def _rope_split_cols(wr):
    k = wr.shape[0]
    heads = wr.shape[1]
    per_group = LANES // ROPE_DIM
    first = wr[:, :, :ROPE_HALF].reshape(k, heads // per_group, per_group * ROPE_HALF)
    second = wr[:, :, ROPE_HALF:].reshape(k, heads // per_group, per_group * ROPE_HALF)
    return jnp.concatenate([first, second], axis=-1).reshape(k, heads * ROPE_DIM)


def _prep_weights(attn_norm_g, w_in, b_forget, q_norm_g, w_uq, kv_norm_g, w_ukv, fox_out_g,
                  mla_out_g, w_o, mlp_norm_g, w_up, w_down, final_norm_g):
    row = lambda v: v.reshape(1, -1).astype(F32)
    pad_lanes = lambda a: jnp.pad(a, ((0, 0), (0, LANES - a.shape[1])))
    wkr = w_in[:, OFF_KR:IN_COLS].reshape(D_MODEL, 1, ROPE_DIM)
    wkr = _rope_split_cols(jnp.broadcast_to(wkr, (D_MODEL, LANES // ROPE_DIM, ROPE_DIM)))
    uq = w_uq.reshape(Q_RANK, N_HEADS, MLA_QK_DIM)
    ukv = w_ukv.reshape(KV_RANK, N_HEADS, 2 * HEAD_DIM)
    tri = np.tril(np.ones((SCAN_CHUNK, SCAN_CHUNK), np.float32))
    w_rows = jnp.concatenate([w_in[:, OFF_FQ:OFF_FK], w_in[:, OFF_FV:OFF_FF], wkr], axis=1)
    w_cols = jnp.concatenate([w_in[:, OFF_FK:OFF_FV], w_in[:, OFF_CQ:OFF_CKV],
                              w_in[:, OFF_CKV:OFF_KR], pad_lanes(w_in[:, OFF_FF:OFF_CQ])], axis=1)
    wuq = jnp.concatenate([uq[:, :, :HEAD_DIM].reshape(Q_RANK, GROUP_WIDTH),
                           _rope_split_cols(uq[:, :, HEAD_DIM:])], axis=1)
    return {
        "g_attn": row(attn_norm_g),
        "w_rows": w_rows.T.astype(BF16),
        "w_cols": w_cols.astype(BF16),
        "bff": pad_lanes(row(b_forget)),
        "gq": row(q_norm_g),
        "gkv": row(kv_norm_g),
        "wuq": wuq.T.astype(BF16),
        "wuk": ukv[:, :, :HEAD_DIM].reshape(KV_RANK, GROUP_WIDTH).astype(BF16),
        "wuvt": ukv[:, :, HEAD_DIM:].reshape(KV_RANK, GROUP_WIDTH).T.astype(BF16),
        "tri": jnp.asarray(tri, BF16),
        "gfox": fox_out_g.reshape(-1, 1).astype(F32),
        "gmla": mla_out_g.reshape(-1, 1).astype(F32),
        "wo": w_o.astype(BF16),
        "gmlp": row(mlp_norm_g),
        "wup": w_up.astype(BF16),
        "wdown": w_down.astype(BF16),
        "gfin": row(final_norm_g),
    }


def _head_masks():
    pairs = N_HEADS // 2
    fox = np.zeros((pairs, 2, 2 * LANES), np.float32)
    mla = np.zeros((pairs, 2, 2 * LANES), np.float32)
    per_group = LANES // ROPE_DIM
    for p in range(pairs):
        for a in range(2):
            h = 2 * p + a
            fox[p, a, a * HEAD_DIM:(a + 1) * HEAD_DIM] = 1.0
            mla[p, a, a * HEAD_DIM:(a + 1) * HEAD_DIM] = 1.0
            for piece in range(3):
                fox[p, a, LANES + piece * N_HEADS + h] = 1.0
            hh = h % per_group
            for half in range(2):
                lo = LANES + half * (LANES // 2) + hh * ROPE_HALF
                mla[p, a, lo:lo + ROPE_HALF] = 1.0
    rep = lambda m: jnp.asarray(np.repeat(m[..., None], LANES, axis=-1), BF16)
    return rep(fox), rep(mla)


def kernel(x, positions, attn_norm_g, w_in, b_forget, q_norm_g, w_uq, kv_norm_g, w_ukv,
           fox_out_g, mla_out_g, w_o, mlp_norm_g, w_up, w_down, final_norm_g):
    b, s, d = x.shape
    depth = w_in.shape[0]
    inv_freq = ROPE_THETA ** (-jnp.arange(0, ROPE_DIM, 2, dtype=F32) / ROPE_DIM)
    cos_t, sin_t = _rope_tables(positions, inv_freq)
    fox_mask, mla_mask = _head_masks()
    ones_aux = jnp.ones((1, LANES, s), BF16)

    for l in range(depth):
        w = _prep_weights(attn_norm_g[l], w_in[l], b_forget[l], q_norm_g[l], w_uq[l],
                          kv_norm_g[l], w_ukv[l], fox_out_g[l], mla_out_g[l], w_o[l],
                          mlp_norm_g[l], w_up[l], w_down[l], final_norm_g)
        fqt, fk, fvt, faux, mqnt, mqrt, mkn, mkr, mvt = _projection(x, cos_t, sin_t, w)
        fox = _attention(fqt, ones_aux, fk, faux, fvt, fox_mask, False, 1)
        mla = _attention(mqnt, mqrt, mkn, mkr, mvt, mla_mask, True, 2)
        y = _out_mlp(x.reshape(b * s, d), fox, mla, w, l == depth - 1)
        x = y.reshape(b, s, d)
    return x
```

```python
import functools
import math

import numpy as np
import jax
import jax.numpy as jnp
from jax import lax
from jax.experimental import pallas as pl
from jax.experimental.pallas import tpu as pltpu

D_MODEL = 1024
HEAD_DIM = 64
N_HEADS = 8
GROUP_WIDTH = N_HEADS * HEAD_DIM
ROPE_DIM = 32
ROPE_HALF = ROPE_DIM // 2
MLA_QK_DIM = HEAD_DIM + ROPE_DIM
Q_RANK = 384
KV_RANK = 256
D_FF = 4096
ROPE_THETA = 10000.0
EPS = 1e-6

OFF_FQ = 0
OFF_FK = OFF_FQ + GROUP_WIDTH
OFF_FV = OFF_FK + GROUP_WIDTH
OFF_FF = OFF_FV + GROUP_WIDTH
OFF_CQ = OFF_FF + N_HEADS
OFF_CKV = OFF_CQ + Q_RANK
OFF_KR = OFF_CKV + KV_RANK
IN_COLS = OFF_KR + ROPE_DIM

LANES = 128
VMEM_LIMIT_BYTES = 56 * 1024 * 1024

LOG2E = math.log2(math.e)
FOX_QSCALE = LOG2E / math.sqrt(HEAD_DIM)
MLA_QSCALE = LOG2E / math.sqrt(MLA_QK_DIM)

PROJ_ROWS = 1024
ATTN_TQ = 512
ATTN_TK = 512
MLP_ROWS = 512
FF_CHUNK = 4096
SCAN_CHUNK = 128
ONES_ROWS = 16
UNROLL_STEPS = 14
FINALIZE_UNROLL = 4

F32 = jnp.float32
BF16 = jnp.bfloat16
NT_DIMS = (((1,), (1,)), ((), ()))


def _rms(x, g):
    return x * lax.rsqrt(jnp.mean(x * x, axis=-1, keepdims=True) + EPS) * g


def _split3(x):
    hi = x.astype(BF16).astype(F32)
    mid = (x - hi).astype(BF16).astype(F32)
    lo = (x - hi - mid).astype(BF16).astype(F32)
    return hi, mid, lo


def _pack3(x):
    hi, mid, lo = _split3(x)
    packed = hi + pltpu.roll(mid, N_HEADS, axis=1) + pltpu.roll(lo, 2 * N_HEADS, axis=1)
    return packed.astype(BF16)


def _rope_table_kernel(pos_ref, freq_ref, cos_ref, sin_ref):
    ang = pos_ref[0].astype(F32) * freq_ref[...]
    cos_ref[0] = jnp.cos(ang)
    sin_ref[0] = jnp.sin(ang)


def _rope_tables(positions, inv_freq):
    b, s = positions.shape
    table = pl.BlockSpec((1, ROPE_HALF, s), lambda i: (i, 0, 0))
    return pl.pallas_call(
        _rope_table_kernel,
        grid=(b,),
        in_specs=[pl.BlockSpec((1, 1, s), lambda i: (i, 0, 0)),
                  pl.BlockSpec((ROPE_HALF, 1), lambda i: (0, 0))],
        out_specs=[table, table],
        out_shape=[jax.ShapeDtypeStruct((b, ROPE_HALF, s), F32)] * 2,
        name="rope_tables",
    )(positions.reshape(b, 1, s), inv_freq.reshape(ROPE_HALF, 1))


def _proj_kernel(x_ref, cos_ref, sin_ref, g_attn_ref, w_rows_ref, w_cols_ref, bff_ref, gq_ref,
                 gkv_ref, wuq_ref, wuk_ref, wuvt_ref, tri_ref,
                 fqt_ref, fk_ref, fvt_ref, faux_ref, mqnt_ref, mqrt_ref, mkn_ref, mkr_ref, mvt_ref,
                 h_sc, carry_sc):
    rows = x_ref.shape[1]

    @pl.when(pl.program_id(1) == 0)
    def _():
        carry_sc[...] = jnp.zeros_like(carry_sc)

    h_sc[...] = _rms(x_ref[0], g_attn_ref[...]).astype(BF16)
    h = h_sc[...]

    groups = LANES // ROPE_HALF
    row = lax.broadcasted_iota(jnp.int32, (LANES, rows), 0)
    cos_t = jnp.tile(cos_ref[0], (groups, 1))
    sin_t = jnp.tile(sin_ref[0], (groups, 1))
    sin_s = jnp.where(row < LANES // 2, -sin_t, sin_t)

    def rope_t(xt):
        return xt * cos_t + pltpu.roll(xt, LANES // 2, axis=0) * sin_s


    tk = jnp.dot(h, w_cols_ref[...], preferred_element_type=F32)
    o_cq, o_ckv, o_ff = GROUP_WIDTH, GROUP_WIDTH + Q_RANK, GROUP_WIDTH + Q_RANK + KV_RANK
    fk_ref[0] = tk[:, 0:o_cq].astype(BF16)

    ft = lax.dot_general(w_rows_ref[...], h, NT_DIMS, preferred_element_type=F32)
    fqt_ref[0] = (ft[0:GROUP_WIDTH] * FOX_QSCALE).astype(BF16)
    fvt_ref[0] = ft[GROUP_WIDTH:2 * GROUP_WIDTH].astype(BF16)
    mkr_ref[0] = rope_t(ft[2 * GROUP_WIDTH:2 * GROUP_WIDTH + LANES]).T.astype(BF16)

    lane = lax.broadcasted_iota(jnp.int32, (1, LANES), 1)
    head_lane = lane < N_HEADS
    z = tk[:, o_ff:o_ff + LANES] + bff_ref[...]
    log_f = jnp.where(head_lane, jax.nn.log_sigmoid(z), 0.0)
    tri = tri_ref[...]
    carry = carry_sc[0:1, :]
    for c in range(rows // SCAN_CHUNK):
        part = jnp.dot(tri, _pack3(log_f[c * SCAN_CHUNK:(c + 1) * SCAN_CHUNK, :]),
                       preferred_element_type=F32)
        part = (part + pltpu.roll(part, LANES - N_HEADS, axis=1)
                + pltpu.roll(part, LANES - 2 * N_HEADS, axis=1))
        cum = jnp.where(head_lane, part, 0.0) + carry
        carry = cum[SCAN_CHUNK - 1:SCAN_CHUNK, :]
        faux_ref[0, c * SCAN_CHUNK:(c + 1) * SCAN_CHUNK, :] = _pack3(cum * (-LOG2E))
    carry_sc[...] = jnp.broadcast_to(carry, carry_sc.shape)

    cq = _rms(tk[:, o_cq:o_ckv], gq_ref[...]).astype(BF16)
    qt = lax.dot_general(wuq_ref[...], cq, NT_DIMS, preferred_element_type=F32)
    mqnt_ref[0] = (qt[0:GROUP_WIDTH] * MLA_QSCALE).astype(BF16)
    for g in range(2 * N_HEADS * ROPE_HALF // LANES):
        lo = GROUP_WIDTH + g * LANES
        mqrt_ref[0, g * LANES:(g + 1) * LANES, :] = (
            rope_t(qt[lo:lo + LANES]) * MLA_QSCALE).astype(BF16)

    ckv = _rms(tk[:, o_ckv:o_ff], gkv_ref[...]).astype(BF16)
    mkn_ref[0] = jnp.dot(ckv, wuk_ref[...], preferred_element_type=F32).astype(BF16)
    mvt_ref[0] = lax.dot_general(wuvt_ref[...], ckv, NT_DIMS,
                                 preferred_element_type=F32).astype(BF16)


def _const_spec(shape):
    return pl.BlockSpec(shape, lambda *_: (0,) * len(shape))


def _projection(x, cos_t, sin_t, w):
    b, s, _ = x.shape
    tm = PROJ_ROWS
    row_spec = lambda width: pl.BlockSpec((1, tm, width), lambda i, j: (i, j, 0))
    col_spec = lambda height: pl.BlockSpec((1, height, tm), lambda i, j: (i, 0, j))
    consts = [w["g_attn"], w["w_rows"], w["w_cols"], w["bff"], w["gq"], w["gkv"], w["wuq"],
              w["wuk"], w["wuvt"], w["tri"]]
    row_out = lambda width: jax.ShapeDtypeStruct((b, s, width), BF16)
    col_out = lambda height: jax.ShapeDtypeStruct((b, height, s), BF16)
    return pl.pallas_call(
        _proj_kernel,
        grid=(b, s // tm),
        in_specs=[row_spec(D_MODEL), col_spec(ROPE_HALF), col_spec(ROPE_HALF)]
                 + [_const_spec(c.shape) for c in consts],
        out_specs=[col_spec(GROUP_WIDTH), row_spec(GROUP_WIDTH), col_spec(GROUP_WIDTH),
                   row_spec(LANES), col_spec(GROUP_WIDTH), col_spec(2 * LANES),
                   row_spec(GROUP_WIDTH), row_spec(LANES), col_spec(GROUP_WIDTH)],
        out_shape=[col_out(GROUP_WIDTH), row_out(GROUP_WIDTH), col_out(GROUP_WIDTH),
                   row_out(LANES), col_out(GROUP_WIDTH), col_out(2 * LANES),
                   row_out(GROUP_WIDTH), row_out(LANES), col_out(GROUP_WIDTH)],
        scratch_shapes=[pltpu.VMEM((tm, D_MODEL), BF16), pltpu.VMEM((8, LANES), F32)],
        compiler_params=pltpu.CompilerParams(
            dimension_semantics=("arbitrary", "arbitrary"),
            vmem_limit_bytes=VMEM_LIMIT_BYTES),
        name="input_projection",
    )(x, cos_t, sin_t, *consts)


def _attn_kernel(tasks_ref, qm_ref, qa_ref, km_ref, ka_ref, vt_ref, mask_ref, o_ref,
                 bias_sc, qh_sc, s0_sc, s1_sc, acc_sc, m_sc):
    s_len = km_ref.shape[1]
    tq, tk = ATTN_TQ, ATTN_TK

    @pl.when((pl.program_id(0) == 0) & (pl.program_id(1) == 0))
    def _():
        key_idx = lax.broadcasted_iota(jnp.int32, (tk, tq), 0)
        qry_idx = lax.broadcasted_iota(jnp.int32, (tk, tq), 1)
        bias_sc[...] = jnp.where(key_idx <= qry_idx, 0.0, -jnp.inf)

    def kv_tile(k0):
        return jnp.concatenate([km_ref[0, pl.ds(k0, tk), :], ka_ref[0, pl.ds(k0, tk), :]], axis=-1)

    ones_rows = jnp.ones((ONES_ROWS, tk), BF16)

    qcat = jnp.concatenate([qm_ref[0], qa_ref[0]], axis=0)
    for a in range(2):
        qh_sc[a] = qcat * jnp.tile(mask_ref[0, a], (1, s_len // LANES))
    last_tile = s_len // tq - 1
    acc_sc[last_tile] = jnp.zeros(acc_sc.shape[1:], F32)
    m_sc[last_tile] = jnp.full(m_sc.shape[1:], -jnp.inf, F32)

    def produce(dst, t, diagonal):
        q0 = pl.multiple_of(tasks_ref[0, t] * tq, tq)
        k0 = pl.multiple_of(tasks_ref[1, t] * tk, tk)
        q_pair = jnp.concatenate([qh_sc[a, :, pl.ds(q0, tq)] for a in range(2)], axis=1)
        sc = jnp.dot(kv_tile(k0), q_pair, preferred_element_type=F32)
        maxima = []
        for a in range(2):
            sc_a = sc[:, a * tq:(a + 1) * tq]
            if diagonal:
                sc_a = sc_a + bias_sc[...]
            dst[a] = sc_a
            maxima.append(jnp.max(sc_a, axis=0, keepdims=True))
        return tuple(maxima)

    def consume(src, t, maxima):
        i = tasks_ref[0, t]
        k0 = pl.multiple_of(tasks_ref[1, t] * tk, tk)
        alphas, ps = [], []
        for a in range(2):
            m_run = m_sc[i, a]
            m_new = jnp.maximum(m_run, maxima[a])
            alphas.append(jnp.exp2(m_run - m_new))
            ps.append(jnp.exp2(src[a] - m_new).astype(BF16))
            m_sc[i, a] = m_new
        vt = jnp.concatenate([vt_ref[0, :, pl.ds(k0, tk)], ones_rows], axis=0)
        pv = jnp.dot(vt, jnp.concatenate(ps, axis=1), preferred_element_type=F32)
        for a in range(2):
            pv_a = jnp.concatenate([pv[a * HEAD_DIM:(a + 1) * HEAD_DIM, a * tq:(a + 1) * tq],
                                    pv[2 * HEAD_DIM:, a * tq:(a + 1) * tq]], axis=0)
            acc_sc[i, a] = alphas[a] * acc_sc[i, a] + pv_a

    half = tq // 2

    def produce_diag(dst, i):
        q0 = i * tq
        kcat = kv_tile(q0)
        q_all = jnp.concatenate([qh_sc[a, :, q0:q0 + tq] for a in range(2)], axis=1)
        q_late = jnp.concatenate([qh_sc[a, :, q0 + half:q0 + tq] for a in range(2)], axis=1)
        top = jnp.dot(kcat[0:half], q_all, preferred_element_type=F32)
        bot = jnp.dot(kcat[half:tk], q_late, preferred_element_type=F32)
        maxima = []
        for a in range(2):
            top_a = top[:, a * tq:(a + 1) * tq] + bias_sc[0:half, :]
            bot_a = bot[:, a * half:(a + 1) * half] + bias_sc[half:tk, half:tq]
            dst[a, 0:half, :] = top_a
            dst[a, half:tk, half:tq] = bot_a
            mx_top = jnp.max(top_a, axis=0, keepdims=True)
            mx_bot = jnp.max(bot_a, axis=0, keepdims=True)
            maxima.append(jnp.concatenate(
                [mx_top[:, 0:half], jnp.maximum(mx_top[:, half:tq], mx_bot)], axis=1))
        return tuple(maxima)

    def consume_diag(src, i, maxima):
        q0 = i * tq
        early, late = [], []
        for a in range(2):
            p_top = jnp.exp2(src[a, 0:half, :] - maxima[a]).astype(BF16)
            p_bot = jnp.exp2(src[a, half:tk, half:tq] - maxima[a][:, half:tq]).astype(BF16)
            early.append(p_top[:, 0:half])
            late.append(jnp.concatenate([p_top[:, half:tq], p_bot], axis=0))
        vt = jnp.concatenate([vt_ref[0, :, q0:q0 + tk], ones_rows], axis=0)
        pv_early = jnp.dot(vt[:, 0:half], jnp.concatenate(early, axis=1),
                           preferred_element_type=F32)
        pv_late = jnp.dot(vt, jnp.concatenate(late, axis=1), preferred_element_type=F32)
        for a in range(2):
            pv_a = jnp.concatenate([pv_early[:, a * half:(a + 1) * half],
                                    pv_late[:, a * half:(a + 1) * half]], axis=1)
            acc_sc[i, a] = jnp.concatenate(
                [pv_a[a * HEAD_DIM:(a + 1) * HEAD_DIM], pv_a[2 * HEAD_DIM:]], axis=0)
            m_sc[i, a] = maxima[a]

    bufs = (s0_sc, s1_sc)

    def step(t, parity, diagonal, mx):
        mx_next = produce(bufs[1 - parity], t + 1, diagonal)
        consume(bufs[parity], t, mx)
        return mx_next

    def steps(first, count, diagonal):
        def body(it, mx):
            t = first + it * count
            for u in range(count):
                mx = step(t + u, (first + u) % 2, diagonal, mx)
            return mx
        return body

    n_diag = s_len // tq
    n_tasks = n_diag * (n_diag + 1) // 2
    n_off = n_tasks - n_diag
    assert n_diag % 2 == 0 and n_off % UNROLL_STEPS == 0 and UNROLL_STEPS % 2 == 0
    assert tq == tk
    mx = produce_diag(s0_sc, 0)
    for t in range(n_diag - 1):
        if t + 1 < n_diag - 1:
            mx_next = produce_diag(bufs[(t + 1) % 2], t + 1)
        else:
            mx_next = produce(bufs[(t + 1) % 2], t + 1, True)
        consume_diag(bufs[t % 2], t, mx)
        mx = mx_next
    mx = lax.fori_loop(0, n_off // UNROLL_STEPS, steps(n_diag - 1, UNROLL_STEPS, False), mx)
    consume(bufs[(n_tasks - 1) % 2], n_tasks - 1, mx)

    def finalize(it, _):
        for u in range(FINALIZE_UNROLL):
            i = it * FINALIZE_UNROLL + u
            q0 = pl.multiple_of(i * tq, tq)
            out_t = jnp.concatenate(
                [acc_sc[i, a, 0:HEAD_DIM, :] / acc_sc[i, a, HEAD_DIM:HEAD_DIM + 1, :]
                 for a in range(2)], axis=0)
            o_ref[0, :, pl.ds(q0, tq)] = out_t
        return 0

    assert n_diag % FINALIZE_UNROLL == 0
    lax.fori_loop(0, n_diag // FINALIZE_UNROLL, finalize, 0)


def _attention(qmt, qat, km, ka, vt, mask, qa_batched, qa_per_pairs):
    b, s, _ = km.shape
    pairs = N_HEADS // 2
    tok = lambda f: pl.BlockSpec((1, s, LANES), f)
    feat = lambda f: pl.BlockSpec((1, LANES, s), f)
    qa_map = (lambda i, p: (i, p // qa_per_pairs, 0)) if qa_batched else (lambda i, p: (0, 0, 0))
    n_q = s // ATTN_TQ
    task_list = [(i, i) for i in range(n_q)] + [(i, j) for j in range(n_q) for i in range(j + 1, n_q)]
    tasks = jnp.asarray(np.array(task_list, np.int32).T)
    return pl.pallas_call(
        _attn_kernel,
        grid=(b, pairs),
        in_specs=[pl.BlockSpec(memory_space=pltpu.SMEM),
                  feat(lambda i, p: (i, p, 0)), feat(qa_map), tok(lambda i, p: (i, 0, p)),
                  tok(lambda i, p: (i, 0, 0)), feat(lambda i, p: (i, p, 0)),
                  pl.BlockSpec((1, 2, 2 * LANES, LANES), lambda i, p: (p, 0, 0, 0))],
        out_specs=feat(lambda i, p: (i, p, 0)),
        out_shape=jax.ShapeDtypeStruct((b, GROUP_WIDTH, s), F32),
        scratch_shapes=[pltpu.VMEM((ATTN_TK, ATTN_TQ), F32),
                        pltpu.VMEM((2, 2 * LANES, s), BF16),
                        pltpu.VMEM((2, ATTN_TK, ATTN_TQ), F32),
                        pltpu.VMEM((2, ATTN_TK, ATTN_TQ), F32),
                        pltpu.VMEM((n_q, 2, HEAD_DIM + ONES_ROWS, ATTN_TQ), F32),
                        pltpu.VMEM((n_q, 2, 1, ATTN_TQ), F32)],
        compiler_params=pltpu.CompilerParams(
            dimension_semantics=("arbitrary", "arbitrary"),
            vmem_limit_bytes=VMEM_LIMIT_BYTES),
        name="causal_attention",
    )(tasks, qmt, qat, km, ka, vt, mask)


def _mlp_kernel(x_ref, fox_ref, mla_ref, gfox_ref, gmla_ref, wo_ref, gmlp_ref, wup_ref,
                wdown_ref, gfin_ref, o_ref, *, final):
    def rms_features(v_t, g_col):
        ms = jnp.mean(v_t * v_t, axis=0, keepdims=True)
        return v_t * lax.rsqrt(ms + EPS) * g_col

    mixed_t = jnp.concatenate([rms_features(fox_ref[0], gfox_ref[...]).astype(BF16),
                               rms_features(mla_ref[0], gmla_ref[...]).astype(BF16)], axis=0)
    x1 = x_ref[...] + lax.dot_general(mixed_t, wo_ref[...], (((0,), (0,)), ((), ())),
                                      preferred_element_type=F32)
    h = _rms(x1, gmlp_ref[...]).astype(BF16)
    y = x1
    for c in range(D_FF // FF_CHUNK):
        u = jnp.dot(h, wup_ref[:, c * FF_CHUNK:(c + 1) * FF_CHUNK], preferred_element_type=F32)
        act = jnp.square(jnp.maximum(u, 0.0)).astype(BF16)
        y = y + jnp.dot(act, wdown_ref[c * FF_CHUNK:(c + 1) * FF_CHUNK, :],
                        preferred_element_type=F32)
    o_ref[...] = _rms(y, gfin_ref[...]) if final else y


def _out_mlp(x2d, fox_t, mla_t, w, final):
    n = x2d.shape[0]
    tm = MLP_ROWS
    tiles_per_seq = fox_t.shape[2] // tm
    row = lambda width: pl.BlockSpec((tm, width), lambda i: (i, 0))
    feat = pl.BlockSpec((1, GROUP_WIDTH, tm), lambda i: (i // tiles_per_seq, 0, i % tiles_per_seq))
    resident = lambda a: pl.BlockSpec(a.shape, lambda i: (0,) * a.ndim,
                                      pipeline_mode=pl.Buffered(1))
    consts = [w["gfox"], w["gmla"], w["wo"], w["gmlp"], w["wup"], w["wdown"], w["gfin"]]
    return pl.pallas_call(
        functools.partial(_mlp_kernel, final=final),
        grid=(n // tm,),
        in_specs=[row(D_MODEL), feat, feat] + [resident(c) for c in consts],
        out_specs=row(D_MODEL),
        out_shape=jax.ShapeDtypeStruct((n, D_MODEL), F32),
        compiler_params=pltpu.CompilerParams(
            dimension_semantics=("arbitrary",),
            vmem_limit_bytes=VMEM_LIMIT_BYTES),
        name="out_proj_mlp",
    )(x2d, fox_t, mla_t, *consts)


def _rope_split_cols(wr):
    k = wr.shape[0]
    heads = wr.shape[1]
    per_group = LANES // ROPE_DIM
    first = wr[:, :, :ROPE_HALF].reshape(k, heads // per_group, per_group * ROPE_HALF)
    second = wr[:, :, ROPE_HALF:].reshape(k, heads // per_group, per_group * ROPE_HALF)
    return jnp.concatenate([first, second], axis=-1).reshape(k, heads * ROPE_DIM)


def _prep_weights(attn_norm_g, w_in, b_forget, q_norm_g, w_uq, kv_norm_g, w_ukv, fox_out_g,
                  mla_out_g, w_o, mlp_norm_g, w_up, w_down, final_norm_g):
    row = lambda v: v.reshape(1, -1).astype(F32)
    pad_lanes = lambda a: jnp.pad(a, ((0, 0), (0, LANES - a.shape[1])))
    wkr = w_in[:, OFF_KR:IN_COLS].reshape(D_MODEL, 1, ROPE_DIM)
    wkr = _rope_split_cols(jnp.broadcast_to(wkr, (D_MODEL, LANES // ROPE_DIM, ROPE_DIM)))
    uq = w_uq.reshape(Q_RANK, N_HEADS, MLA_QK_DIM)
    ukv = w_ukv.reshape(KV_RANK, N_HEADS, 2 * HEAD_DIM)
    tri = np.tril(np.ones((SCAN_CHUNK, SCAN_CHUNK), np.float32))
    w_rows = jnp.concatenate([w_in[:, OFF_FQ:OFF_FK], w_in[:, OFF_FV:OFF_FF], wkr], axis=1)
    w_cols = jnp.concatenate([w_in[:, OFF_FK:OFF_FV], w_in[:, OFF_CQ:OFF_CKV],
                              w_in[:, OFF_CKV:OFF_KR], pad_lanes(w_in[:, OFF_FF:OFF_CQ])], axis=1)
    wuq = jnp.concatenate([uq[:, :, :HEAD_DIM].reshape(Q_RANK, GROUP_WIDTH),
                           _rope_split_cols(uq[:, :, HEAD_DIM:])], axis=1)
    return {
        "g_attn": row(attn_norm_g),
        "w_rows": w_rows.T.astype(BF16),
        "w_cols": w_cols.astype(BF16),
        "bff": pad_lanes(row(b_forget)),
        "gq": row(q_norm_g),
        "gkv": row(kv_norm_g),
        "wuq": wuq.T.astype(BF16),
        "wuk": ukv[:, :, :HEAD_DIM].reshape(KV_RANK, GROUP_WIDTH).astype(BF16),
        "wuvt": ukv[:, :, HEAD_DIM:].reshape(KV_RANK, GROUP_WIDTH).T.astype(BF16),
        "tri": jnp.asarray(tri, BF16),
        "gfox": fox_out_g.reshape(-1, 1).astype(F32),
        "gmla": mla_out_g.reshape(-1, 1).astype(F32),
        "wo": w_o.astype(BF16),
        "gmlp": row(mlp_norm_g),
        "wup": w_up.astype(BF16),
        "wdown": w_down.astype(BF16),
        "gfin": row(final_norm_g),
    }


def _head_masks():
    pairs = N_HEADS // 2
    fox = np.zeros((pairs, 2, 2 * LANES), np.float32)
    mla = np.zeros((pairs, 2, 2 * LANES), np.float32)
    per_group = LANES // ROPE_DIM
    for p in range(pairs):
        for a in range(2):
            h = 2 * p + a
            fox[p, a, a * HEAD_DIM:(a + 1) * HEAD_DIM] = 1.0
            mla[p, a, a * HEAD_DIM:(a + 1) * HEAD_DIM] = 1.0
            for piece in range(3):
                fox[p, a, LANES + piece * N_HEADS + h] = 1.0
            hh = h % per_group
            for half in range(2):
                lo = LANES + half * (LANES // 2) + hh * ROPE_HALF
                mla[p, a, lo:lo + ROPE_HALF] = 1.0
    rep = lambda m: jnp.asarray(np.repeat(m[..., None], LANES, axis=-1), BF16)
    return rep(fox), rep(mla)


def kernel(x, positions, attn_norm_g, w_in, b_forget, q_norm_g, w_uq, kv_norm_g, w_ukv,
           fox_out_g, mla_out_g, w_o, mlp_norm_g, w_up, w_down, final_norm_g):
    b, s, d = x.shape
    depth = w_in.shape[0]
    inv_freq = ROPE_THETA ** (-jnp.arange(0, ROPE_DIM, 2, dtype=F32) / ROPE_DIM)
    cos_t, sin_t = _rope_tables(positions, inv_freq)
    fox_mask, mla_mask = _head_masks()
    ones_aux = jnp.ones((1, LANES, s), BF16)

    for l in range(depth):
        w = _prep_weights(attn_norm_g[l], w_in[l], b_forget[l], q_norm_g[l], w_uq[l],
                          kv_norm_g[l], w_ukv[l], fox_out_g[l], mla_out_g[l], w_o[l],
                          mlp_norm_g[l], w_up[l], w_down[l], final_norm_g)
        fqt, fk, fvt, faux, mqnt, mqrt, mkn, mkr, mvt = _projection(x, cos_t, sin_t, w)
        fox = _attention(fqt, ones_aux, fk, faux, fvt, fox_mask, False, 1)
        mla = _attention(mqnt, mqrt, mkn, mkr, mvt, mla_mask, True, 2)
        y = _out_mlp(x.reshape(b * s, d), fox, mla, w, l == depth - 1)
        x = y.reshape(b, s, d)
    return x
```

```python
import functools
import math

import numpy as np
import jax
import jax.numpy as jnp
from jax import lax
from jax.experimental import pallas as pl
from jax.experimental.pallas import tpu as pltpu

D_MODEL = 1024
HEAD_DIM = 64
N_HEADS = 8
GROUP_WIDTH = N_HEADS * HEAD_DIM
ROPE_DIM = 32
ROPE_HALF = ROPE_DIM // 2
MLA_QK_DIM = HEAD_DIM + ROPE_DIM
Q_RANK = 384
KV_RANK = 256
D_FF = 4096
ROPE_THETA = 10000.0
EPS = 1e-6

OFF_FQ = 0
OFF_FK = OFF_FQ + GROUP_WIDTH
OFF_FV = OFF_FK + GROUP_WIDTH
OFF_FF = OFF_FV + GROUP_WIDTH
OFF_CQ = OFF_FF + N_HEADS
OFF_CKV = OFF_CQ + Q_RANK
OFF_KR = OFF_CKV + KV_RANK
IN_COLS = OFF_KR + ROPE_DIM

LANES = 128
VMEM_LIMIT_BYTES = 56 * 1024 * 1024

LOG2E = math.log2(math.e)
FOX_QSCALE = LOG2E / math.sqrt(HEAD_DIM)
MLA_QSCALE = LOG2E / math.sqrt(MLA_QK_DIM)

PROJ_ROWS = 1024
ATTN_TQ = 512
ATTN_TK = 512
MLP_ROWS = 512
FF_CHUNK = 4096
SCAN_CHUNK = 128
ONES_ROWS = 16
UNROLL_STEPS = 28
FINALIZE_UNROLL = 4

F32 = jnp.float32
BF16 = jnp.bfloat16
NT_DIMS = (((1,), (1,)), ((), ()))


def _rms(x, g):
    return x * lax.rsqrt(jnp.mean(x * x, axis=-1, keepdims=True) + EPS) * g


def _split3(x):
    hi = x.astype(BF16).astype(F32)
    mid = (x - hi).astype(BF16).astype(F32)
    lo = (x - hi - mid).astype(BF16).astype(F32)
    return hi, mid, lo


def _pack3(x):
    hi, mid, lo = _split3(x)
    packed = hi + pltpu.roll(mid, N_HEADS, axis=1) + pltpu.roll(lo, 2 * N_HEADS, axis=1)
    return packed.astype(BF16)


def _rope_table_kernel(pos_ref, freq_ref, cos_ref, sin_ref):
    ang = pos_ref[0].astype(F32) * freq_ref[...]
    cos_ref[0] = jnp.cos(ang)
    sin_ref[0] = jnp.sin(ang)


def _rope_tables(positions, inv_freq):
    b, s = positions.shape
    table = pl.BlockSpec((1, ROPE_HALF, s), lambda i: (i, 0, 0))
    return pl.pallas_call(
        _rope_table_kernel,
        grid=(b,),
        in_specs=[pl.BlockSpec((1, 1, s), lambda i: (i, 0, 0)),
                  pl.BlockSpec((ROPE_HALF, 1), lambda i: (0, 0))],
        out_specs=[table, table],
        out_shape=[jax.ShapeDtypeStruct((b, ROPE_HALF, s), F32)] * 2,
        name="rope_tables",
    )(positions.reshape(b, 1, s), inv_freq.reshape(ROPE_HALF, 1))


def _proj_kernel(x_ref, cos_ref, sin_ref, g_attn_ref, w_rows_ref, w_cols_ref, bff_ref, gq_ref,
                 gkv_ref, wuq_ref, wuk_ref, wuvt_ref, tri_ref,
                 fqt_ref, fk_ref, fvt_ref, faux_ref, mqnt_ref, mqrt_ref, mkn_ref, mkr_ref, mvt_ref,
                 h_sc, carry_sc):
    rows = x_ref.shape[1]

    @pl.when(pl.program_id(1) == 0)
    def _():
        carry_sc[...] = jnp.zeros_like(carry_sc)

    h_sc[...] = _rms(x_ref[0], g_attn_ref[...]).astype(BF16)
    h = h_sc[...]

    groups = LANES // ROPE_HALF
    row = lax.broadcasted_iota(jnp.int32, (LANES, rows), 0)
    cos_t = jnp.tile(cos_ref[0], (groups, 1))
    sin_t = jnp.tile(sin_ref[0], (groups, 1))
    sin_s = jnp.where(row < LANES // 2, -sin_t, sin_t)

    def rope_t(xt):
        return xt * cos_t + pltpu.roll(xt, LANES // 2, axis=0) * sin_s


    tk = jnp.dot(h, w_cols_ref[...], preferred_element_type=F32)
    o_cq, o_ckv, o_ff = GROUP_WIDTH, GROUP_WIDTH + Q_RANK, GROUP_WIDTH + Q_RANK + KV_RANK
    fk_ref[0] = tk[:, 0:o_cq].astype(BF16)

    ft = lax.dot_general(w_rows_ref[...], h, NT_DIMS, preferred_element_type=F32)
    fqt_ref[0] = (ft[0:GROUP_WIDTH] * FOX_QSCALE).astype(BF16)
    fvt_ref[0] = ft[GROUP_WIDTH:2 * GROUP_WIDTH].astype(BF16)
    mkr_ref[0] = rope_t(ft[2 * GROUP_WIDTH:2 * GROUP_WIDTH + LANES]).T.astype(BF16)

    lane = lax.broadcasted_iota(jnp.int32, (1, LANES), 1)
    head_lane = lane < N_HEADS
    z = tk[:, o_ff:o_ff + LANES] + bff_ref[...]
    log_f = jnp.where(head_lane, jax.nn.log_sigmoid(z), 0.0)
    tri = tri_ref[...]
    carry = carry_sc[0:1, :]
    for c in range(rows // SCAN_CHUNK):
        part = jnp.dot(tri, _pack3(log_f[c * SCAN_CHUNK:(c + 1) * SCAN_CHUNK, :]),
                       preferred_element_type=F32)
        part = (part + pltpu.roll(part, LANES - N_HEADS, axis=1)
                + pltpu.roll(part, LANES - 2 * N_HEADS, axis=1))
        cum = jnp.where(head_lane, part, 0.0) + carry
        carry = cum[SCAN_CHUNK - 1:SCAN_CHUNK, :]
        faux_ref[0, c * SCAN_CHUNK:(c + 1) * SCAN_CHUNK, :] = _pack3(cum * (-LOG2E))
    carry_sc[...] = jnp.broadcast_to(carry, carry_sc.shape)

    cq = _rms(tk[:, o_cq:o_ckv], gq_ref[...]).astype(BF16)
    qt = lax.dot_general(wuq_ref[...], cq, NT_DIMS, preferred_element_type=F32)
    mqnt_ref[0] = (qt[0:GROUP_WIDTH] * MLA_QSCALE).astype(BF16)
    for g in range(2 * N_HEADS * ROPE_HALF // LANES):
        lo = GROUP_WIDTH + g * LANES
        mqrt_ref[0, g * LANES:(g + 1) * LANES, :] = (
            rope_t(qt[lo:lo + LANES]) * MLA_QSCALE).astype(BF16)

    ckv = _rms(tk[:, o_ckv:o_ff], gkv_ref[...]).astype(BF16)
    mkn_ref[0] = jnp.dot(ckv, wuk_ref[...], preferred_element_type=F32).astype(BF16)
    mvt_ref[0] = lax.dot_general(wuvt_ref[...], ckv, NT_DIMS,
                                 preferred_element_type=F32).astype(BF16)


def _const_spec(shape):
    return pl.BlockSpec(shape, lambda *_: (0,) * len(shape))


def _projection(x, cos_t, sin_t, w):
    b, s, _ = x.shape
    tm = PROJ_ROWS
    row_spec = lambda width: pl.BlockSpec((1, tm, width), lambda i, j: (i, j, 0))
    col_spec = lambda height: pl.BlockSpec((1, height, tm), lambda i, j: (i, 0, j))
    consts = [w["g_attn"], w["w_rows"], w["w_cols"], w["bff"], w["gq"], w["gkv"], w["wuq"],
              w["wuk"], w["wuvt"], w["tri"]]
    row_out = lambda width: jax.ShapeDtypeStruct((b, s, width), BF16)
    col_out = lambda height: jax.ShapeDtypeStruct((b, height, s), BF16)
    return pl.pallas_call(
        _proj_kernel,
        grid=(b, s // tm),
        in_specs=[row_spec(D_MODEL), col_spec(ROPE_HALF), col_spec(ROPE_HALF)]
                 + [_const_spec(c.shape) for c in consts],
        out_specs=[col_spec(GROUP_WIDTH), row_spec(GROUP_WIDTH), col_spec(GROUP_WIDTH),
                   row_spec(LANES), col_spec(GROUP_WIDTH), col_spec(2 * LANES),
                   row_spec(GROUP_WIDTH), row_spec(LANES), col_spec(GROUP_WIDTH)],
        out_shape=[col_out(GROUP_WIDTH), row_out(GROUP_WIDTH), col_out(GROUP_WIDTH),
                   row_out(LANES), col_out(GROUP_WIDTH), col_out(2 * LANES),
                   row_out(GROUP_WIDTH), row_out(LANES), col_out(GROUP_WIDTH)],
        scratch_shapes=[pltpu.VMEM((tm, D_MODEL), BF16), pltpu.VMEM((8, LANES), F32)],
        compiler_params=pltpu.CompilerParams(
            dimension_semantics=("arbitrary", "arbitrary"),
            vmem_limit_bytes=VMEM_LIMIT_BYTES),
        name="input_projection",
    )(x, cos_t, sin_t, *consts)


def _attn_kernel(tasks_ref, qm_ref, qa_ref, km_ref, ka_ref, vt_ref, mask_ref, o_ref,
                 bias_sc, qh_sc, s0_sc, s1_sc, acc_sc, m_sc):
    s_len = km_ref.shape[1]
    tq, tk = ATTN_TQ, ATTN_TK

    @pl.when((pl.program_id(0) == 0) & (pl.program_id(1) == 0))
    def _():
        key_idx = lax.broadcasted_iota(jnp.int32, (tk, tq), 0)
        qry_idx = lax.broadcasted_iota(jnp.int32, (tk, tq), 1)
        bias_sc[...] = jnp.where(key_idx <= qry_idx, 0.0, -jnp.inf)

    def kv_tile(k0):
        return jnp.concatenate([km_ref[0, pl.ds(k0, tk), :], ka_ref[0, pl.ds(k0, tk), :]], axis=-1)

    ones_rows = jnp.ones((ONES_ROWS, tk), BF16)

    qcat = jnp.concatenate([qm_ref[0], qa_ref[0]], axis=0)
    for a in range(2):
        qh_sc[a] = qcat * jnp.tile(mask_ref[0, a], (1, s_len // LANES))
    last_tile = s_len // tq - 1
    acc_sc[last_tile] = jnp.zeros(acc_sc.shape[1:], F32)
    m_sc[last_tile] = jnp.full(m_sc.shape[1:], -jnp.inf, F32)

    def produce(dst, t, diagonal):
        q0 = pl.multiple_of(tasks_ref[0, t] * tq, tq)
        k0 = pl.multiple_of(tasks_ref[1, t] * tk, tk)
        q_pair = jnp.concatenate([qh_sc[a, :, pl.ds(q0, tq)] for a in range(2)], axis=1)
        sc = jnp.dot(kv_tile(k0), q_pair, preferred_element_type=F32)
        maxima = []
        for a in range(2):
            sc_a = sc[:, a * tq:(a + 1) * tq]
            if diagonal:
                sc_a = sc_a + bias_sc[...]
            dst[a] = sc_a
            maxima.append(jnp.max(sc_a, axis=0, keepdims=True))
        return tuple(maxima)

    def consume(src, t, maxima):
        i = tasks_ref[0, t]
        k0 = pl.multiple_of(tasks_ref[1, t] * tk, tk)
        alphas, ps = [], []
        for a in range(2):
            m_run = m_sc[i, a]
            m_new = jnp.maximum(m_run, maxima[a])
            alphas.append(jnp.exp2(m_run - m_new))
            ps.append(jnp.exp2(src[a] - m_new).astype(BF16))
            m_sc[i, a] = m_new
        vt = jnp.concatenate([vt_ref[0, :, pl.ds(k0, tk)], ones_rows], axis=0)
        pv = jnp.dot(vt, jnp.concatenate(ps, axis=1), preferred_element_type=F32)
        for a in range(2):
            pv_a = jnp.concatenate([pv[a * HEAD_DIM:(a + 1) * HEAD_DIM, a * tq:(a + 1) * tq],
                                    pv[2 * HEAD_DIM:, a * tq:(a + 1) * tq]], axis=0)
            acc_sc[i, a] = alphas[a] * acc_sc[i, a] + pv_a

    half = tq // 2

    def produce_diag(dst, i):
        q0 = i * tq
        kcat = kv_tile(q0)
        q_all = jnp.concatenate([qh_sc[a, :, q0:q0 + tq] for a in range(2)], axis=1)
        q_late = jnp.concatenate([qh_sc[a, :, q0 + half:q0 + tq] for a in range(2)], axis=1)
        top = jnp.dot(kcat[0:half], q_all, preferred_element_type=F32)
        bot = jnp.dot(kcat[half:tk], q_late, preferred_element_type=F32)
        maxima = []
        for a in range(2):
            top_a = top[:, a * tq:(a + 1) * tq] + bias_sc[0:half, :]
            bot_a = bot[:, a * half:(a + 1) * half] + bias_sc[half:tk, half:tq]
            dst[a, 0:half, :] = top_a
            dst[a, half:tk, half:tq] = bot_a
            mx_top = jnp.max(top_a, axis=0, keepdims=True)
            mx_bot = jnp.max(bot_a, axis=0, keepdims=True)
            maxima.append(jnp.concatenate(
                [mx_top[:, 0:half], jnp.maximum(mx_top[:, half:tq], mx_bot)], axis=1))
        return tuple(maxima)

    def consume_diag(src, i, maxima):
        q0 = i * tq
        early, late = [], []
        for a in range(2):
            p_top = jnp.exp2(src[a, 0:half, :] - maxima[a]).astype(BF16)
            p_bot = jnp.exp2(src[a, half:tk, half:tq] - maxima[a][:, half:tq]).astype(BF16)
            early.append(p_top[:, 0:half])
            late.append(jnp.concatenate([p_top[:, half:tq], p_bot], axis=0))
        vt = jnp.concatenate([vt_ref[0, :, q0:q0 + tk], ones_rows], axis=0)
        pv_early = jnp.dot(vt[:, 0:half], jnp.concatenate(early, axis=1),
                           preferred_element_type=F32)
        pv_late = jnp.dot(vt, jnp.concatenate(late, axis=1), preferred_element_type=F32)
        for a in range(2):
            pv_a = jnp.concatenate([pv_early[:, a * half:(a + 1) * half],
                                    pv_late[:, a * half:(a + 1) * half]], axis=1)
            acc_sc[i, a] = jnp.concatenate(
                [pv_a[a * HEAD_DIM:(a + 1) * HEAD_DIM], pv_a[2 * HEAD_DIM:]], axis=0)
            m_sc[i, a] = maxima[a]

    bufs = (s0_sc, s1_sc)

    def step(t, parity, diagonal, mx):
        mx_next = produce(bufs[1 - parity], t + 1, diagonal)
        consume(bufs[parity], t, mx)
        return mx_next

    def steps(first, count, diagonal):
        def body(it, mx):
            t = first + it * count
            for u in range(count):
                mx = step(t + u, (first + u) % 2, diagonal, mx)
            return mx
        return body

    n_diag = s_len // tq
    n_tasks = n_diag * (n_diag + 1) // 2
    n_off = n_tasks - n_diag
    assert n_diag % 2 == 0 and n_off % UNROLL_STEPS == 0 and UNROLL_STEPS % 2 == 0
    assert tq == tk
    mx = produce_diag(s0_sc, 0)
    for t in range(n_diag - 1):
        if t + 1 < n_diag - 1:
            mx_next = produce_diag(bufs[(t + 1) % 2], t + 1)
        else:
            mx_next = produce(bufs[(t + 1) % 2], t + 1, True)
        consume_diag(bufs[t % 2], t, mx)
        mx = mx_next
    mx = lax.fori_loop(0, tasks_ref[2, 0], steps(n_diag - 1, UNROLL_STEPS, False), mx)
    consume(bufs[(n_tasks - 1) % 2], n_tasks - 1, mx)

    def finalize(it, _):
        for u in range(FINALIZE_UNROLL):
            i = it * FINALIZE_UNROLL + u
            q0 = pl.multiple_of(i * tq, tq)
            out_t = jnp.concatenate(
                [acc_sc[i, a, 0:HEAD_DIM, :] / acc_sc[i, a, HEAD_DIM:HEAD_DIM + 1, :]
                 for a in range(2)], axis=0)
            o_ref[0, :, pl.ds(q0, tq)] = out_t
        return 0

    assert n_diag % FINALIZE_UNROLL == 0
    lax.fori_loop(0, n_diag // FINALIZE_UNROLL, finalize, 0)


def _attention(qmt, qat, km, ka, vt, mask, qa_batched, qa_per_pairs):
    b, s, _ = km.shape
    pairs = N_HEADS // 2
    tok = lambda f: pl.BlockSpec((1, s, LANES), f)
    feat = lambda f: pl.BlockSpec((1, LANES, s), f)
    qa_map = (lambda i, p: (i, p // qa_per_pairs, 0)) if qa_batched else (lambda i, p: (0, 0, 0))
    n_q = s // ATTN_TQ
    task_list = [(i, i) for i in range(n_q)] + [(i, j) for i in range(n_q) for j in range(i)]
    trips = (len(task_list) - n_q) // UNROLL_STEPS
    tasks = jnp.asarray(np.concatenate(
        [np.array(task_list, np.int32).T, np.full((1, len(task_list)), trips, np.int32)], axis=0))
    return pl.pallas_call(
        _attn_kernel,
        grid=(b, pairs),
        in_specs=[pl.BlockSpec(memory_space=pltpu.SMEM),
                  feat(lambda i, p: (i, p, 0)), feat(qa_map), tok(lambda i, p: (i, 0, p)),
                  tok(lambda i, p: (i, 0, 0)), feat(lambda i, p: (i, p, 0)),
                  pl.BlockSpec((1, 2, 2 * LANES, LANES), lambda i, p: (p, 0, 0, 0))],
        out_specs=feat(lambda i, p: (i, p, 0)),
        out_shape=jax.ShapeDtypeStruct((b, GROUP_WIDTH, s), F32),
        scratch_shapes=[pltpu.VMEM((ATTN_TK, ATTN_TQ), F32),
                        pltpu.VMEM((2, 2 * LANES, s), BF16),
                        pltpu.VMEM((2, ATTN_TK, ATTN_TQ), F32),
                        pltpu.VMEM((2, ATTN_TK, ATTN_TQ), F32),
                        pltpu.VMEM((n_q, 2, HEAD_DIM + ONES_ROWS, ATTN_TQ), F32),
                        pltpu.VMEM((n_q, 2, 1, ATTN_TQ), F32)],
        compiler_params=pltpu.CompilerParams(
            dimension_semantics=("arbitrary", "arbitrary"),
            vmem_limit_bytes=VMEM_LIMIT_BYTES),
        name="causal_attention",
    )(tasks, qmt, qat, km, ka, vt, mask)


def _mlp_kernel(x_ref, fox_ref, mla_ref, gfox_ref, gmla_ref, wo_ref, gmlp_ref, wup_ref,
                wdown_ref, gfin_ref, o_ref, *, final):
    def rms_features(v_t, g_col):
        ms = jnp.mean(v_t * v_t, axis=0, keepdims=True)
        return v_t * lax.rsqrt(ms + EPS) * g_col

    mixed_t = jnp.concatenate([rms_features(fox_ref[0], gfox_ref[...]).astype(BF16),
                               rms_features(mla_ref[0], gmla_ref[...]).astype(BF16)], axis=0)
    x1 = x_ref[...] + lax.dot_general(mixed_t, wo_ref[...], (((0,), (0,)), ((), ())),
                                      preferred_element_type=F32)
    h = _rms(x1, gmlp_ref[...]).astype(BF16)
    y = x1
    for c in range(D_FF // FF_CHUNK):
        u = jnp.dot(h, wup_ref[:, c * FF_CHUNK:(c + 1) * FF_CHUNK], preferred_element_type=F32)
        act = jnp.square(jnp.maximum(u, 0.0)).astype(BF16)
        y = y + jnp.dot(act, wdown_ref[c * FF_CHUNK:(c + 1) * FF_CHUNK, :],
                        preferred_element_type=F32)
    o_ref[...] = _rms(y, gfin_ref[...]) if final else y


def _out_mlp(x2d, fox_t, mla_t, w, final):
    n = x2d.shape[0]
    tm = MLP_ROWS
    tiles_per_seq = fox_t.shape[2] // tm
    row = lambda width: pl.BlockSpec((tm, width), lambda i: (i, 0))
    feat = pl.BlockSpec((1, GROUP_WIDTH, tm), lambda i: (i // tiles_per_seq, 0, i % tiles_per_seq))
    resident = lambda a: pl.BlockSpec(a.shape, lambda i: (0,) * a.ndim,
                                      pipeline_mode=pl.Buffered(1))
    consts = [w["gfox"], w["gmla"], w["wo"], w["gmlp"], w["wup"], w["wdown"], w["gfin"]]
    return pl.pallas_call(
        functools.partial(_mlp_kernel, final=final),
        grid=(n // tm,),
        in_specs=[row(D_MODEL), feat, feat] + [resident(c) for c in consts],
        out_specs=row(D_MODEL),
        out_shape=jax.ShapeDtypeStruct((n, D_MODEL), F32),
        compiler_params=pltpu.CompilerParams(
            dimension_semantics=("arbitrary",),
            vmem_limit_bytes=VMEM_LIMIT_BYTES),
        name="out_proj_mlp",
    )(x2d, fox_t, mla_t, *consts)


def _rope_split_cols(wr):
    k = wr.shape[0]
    heads = wr.shape[1]
    per_group = LANES // ROPE_DIM
    first = wr[:, :, :ROPE_HALF].reshape(k, heads // per_group, per_group * ROPE_HALF)
    second = wr[:, :, ROPE_HALF:].reshape(k, heads // per_group, per_group * ROPE_HALF)
    return jnp.concatenate([first, second], axis=-1).reshape(k, heads * ROPE_DIM)


def _prep_weights(attn_norm_g, w_in, b_forget, q_norm_g, w_uq, kv_norm_g, w_ukv, fox_out_g,
                  mla_out_g, w_o, mlp_norm_g, w_up, w_down, final_norm_g):
    row = lambda v: v.reshape(1, -1).astype(F32)
    pad_lanes = lambda a: jnp.pad(a, ((0, 0), (0, LANES - a.shape[1])))
    wkr = w_in[:, OFF_KR:IN_COLS].reshape(D_MODEL, 1, ROPE_DIM)
    wkr = _rope_split_cols(jnp.broadcast_to(wkr, (D_MODEL, LANES // ROPE_DIM, ROPE_DIM)))
    uq = w_uq.reshape(Q_RANK, N_HEADS, MLA_QK_DIM)
    ukv = w_ukv.reshape(KV_RANK, N_HEADS, 2 * HEAD_DIM)
    tri = np.tril(np.ones((SCAN_CHUNK, SCAN_CHUNK), np.float32))
    w_rows = jnp.concatenate([w_in[:, OFF_FQ:OFF_FK], w_in[:, OFF_FV:OFF_FF], wkr], axis=1)
    w_cols = jnp.concatenate([w_in[:, OFF_FK:OFF_FV], w_in[:, OFF_CQ:OFF_CKV],
                              w_in[:, OFF_CKV:OFF_KR], pad_lanes(w_in[:, OFF_FF:OFF_CQ])], axis=1)
    wuq = jnp.concatenate([uq[:, :, :HEAD_DIM].reshape(Q_RANK, GROUP_WIDTH),
                           _rope_split_cols(uq[:, :, HEAD_DIM:])], axis=1)
    return {
        "g_attn": row(attn_norm_g),
        "w_rows": w_rows.T.astype(BF16),
        "w_cols": w_cols.astype(BF16),
        "bff": pad_lanes(row(b_forget)),
        "gq": row(q_norm_g),
        "gkv": row(kv_norm_g),
        "wuq": wuq.T.astype(BF16),
        "wuk": ukv[:, :, :HEAD_DIM].reshape(KV_RANK, GROUP_WIDTH).astype(BF16),
        "wuvt": ukv[:, :, HEAD_DIM:].reshape(KV_RANK, GROUP_WIDTH).T.astype(BF16),
        "tri": jnp.asarray(tri, BF16),
        "gfox": fox_out_g.reshape(-1, 1).astype(F32),
        "gmla": mla_out_g.reshape(-1, 1).astype(F32),
        "wo": w_o.astype(BF16),
        "gmlp": row(mlp_norm_g),
        "wup": w_up.astype(BF16),
        "wdown": w_down.astype(BF16),
        "gfin": row(final_norm_g),
    }


def _head_masks():
    pairs = N_HEADS // 2
    fox = np.zeros((pairs, 2, 2 * LANES), np.float32)
    mla = np.zeros((pairs, 2, 2 * LANES), np.float32)
    per_group = LANES // ROPE_DIM
    for p in range(pairs):
        for a in range(2):
            h = 2 * p + a
            fox[p, a, a * HEAD_DIM:(a + 1) * HEAD_DIM] = 1.0
            mla[p, a, a * HEAD_DIM:(a + 1) * HEAD_DIM] = 1.0
            for piece in range(3):
                fox[p, a, LANES + piece * N_HEADS + h] = 1.0
            hh = h % per_group
            for half in range(2):
                lo = LANES + half * (LANES // 2) + hh * ROPE_HALF
                mla[p, a, lo:lo + ROPE_HALF] = 1.0
    rep = lambda m: jnp.asarray(np.repeat(m[..., None], LANES, axis=-1), BF16)
    return rep(fox), rep(mla)


def kernel(x, positions, attn_norm_g, w_in, b_forget, q_norm_g, w_uq, kv_norm_g, w_ukv,
           fox_out_g, mla_out_g, w_o, mlp_norm_g, w_up, w_down, final_norm_g):
    b, s, d = x.shape
    depth = w_in.shape[0]
    inv_freq = ROPE_THETA ** (-jnp.arange(0, ROPE_DIM, 2, dtype=F32) / ROPE_DIM)
    cos_t, sin_t = _rope_tables(positions, inv_freq)
    fox_mask, mla_mask = _head_masks()
    ones_aux = jnp.ones((1, LANES, s), BF16)

    for l in range(depth):
        w = _prep_weights(attn_norm_g[l], w_in[l], b_forget[l], q_norm_g[l], w_uq[l],
                          kv_norm_g[l], w_ukv[l], fox_out_g[l], mla_out_g[l], w_o[l],
                          mlp_norm_g[l], w_up[l], w_down[l], final_norm_g)
        fqt, fk, fvt, faux, mqnt, mqrt, mkn, mkr, mvt = _projection(x, cos_t, sin_t, w)
        fox = _attention(fqt, ones_aux, fk, faux, fvt, fox_mask, False, 1)
        mla = _attention(mqnt, mqrt, mkn, mkr, mvt, mla_mask, True, 2)
        y = _out_mlp(x.reshape(b * s, d), fox, mla, w, l == depth - 1)
        x = y.reshape(b, s, d)
    return x
```

```python
import functools
import math

import numpy as np
import jax
import jax.numpy as jnp
from jax import lax
from jax.experimental import pallas as pl
from jax.experimental.pallas import tpu as pltpu

D_MODEL = 1024
HEAD_DIM = 64
N_HEADS = 8
GROUP_WIDTH = N_HEADS * HEAD_DIM
ROPE_DIM = 32
ROPE_HALF = ROPE_DIM // 2
MLA_QK_DIM = HEAD_DIM + ROPE_DIM
Q_RANK = 384
KV_RANK = 256
D_FF = 4096
ROPE_THETA = 10000.0
EPS = 1e-6

OFF_FQ = 0
OFF_FK = OFF_FQ + GROUP_WIDTH
OFF_FV = OFF_FK + GROUP_WIDTH
OFF_FF = OFF_FV + GROUP_WIDTH
OFF_CQ = OFF_FF + N_HEADS
OFF_CKV = OFF_CQ + Q_RANK
OFF_KR = OFF_CKV + KV_RANK
IN_COLS = OFF_KR + ROPE_DIM

LANES = 128
VMEM_LIMIT_BYTES = 56 * 1024 * 1024

LOG2E = math.log2(math.e)
FOX_QSCALE = LOG2E / math.sqrt(HEAD_DIM)
MLA_QSCALE = LOG2E / math.sqrt(MLA_QK_DIM)

PROJ_ROWS = 1024
ATTN_TQ = 512
ATTN_TK = 512
MLP_ROWS = 1024
FF_CHUNK = 1024
SCAN_CHUNK = 128
ONES_ROWS = 16
UNROLL_STEPS = 14
FINALIZE_UNROLL = 4

F32 = jnp.float32
BF16 = jnp.bfloat16
NT_DIMS = (((1,), (1,)), ((), ()))


def _rms(x, g):
    return x * lax.rsqrt(jnp.mean(x * x, axis=-1, keepdims=True) + EPS) * g


def _split3(x):
    hi = x.astype(BF16).astype(F32)
    mid = (x - hi).astype(BF16).astype(F32)
    lo = (x - hi - mid).astype(BF16).astype(F32)
    return hi, mid, lo


def _pack3(x):
    hi, mid, lo = _split3(x)
    packed = hi + pltpu.roll(mid, N_HEADS, axis=1) + pltpu.roll(lo, 2 * N_HEADS, axis=1)
    return packed.astype(BF16)


def _rope_table_kernel(pos_ref, freq_ref, cos_ref, sin_ref):
    ang = pos_ref[0].astype(F32) * freq_ref[...]
    cos_ref[0] = jnp.cos(ang)
    sin_ref[0] = jnp.sin(ang)


def _rope_tables(positions, inv_freq):
    b, s = positions.shape
    table = pl.BlockSpec((1, ROPE_HALF, s), lambda i: (i, 0, 0))
    return pl.pallas_call(
        _rope_table_kernel,
        grid=(b,),
        in_specs=[pl.BlockSpec((1, 1, s), lambda i: (i, 0, 0)),
                  pl.BlockSpec((ROPE_HALF, 1), lambda i: (0, 0))],
        out_specs=[table, table],
        out_shape=[jax.ShapeDtypeStruct((b, ROPE_HALF, s), F32)] * 2,
        name="rope_tables",
    )(positions.reshape(b, 1, s), inv_freq.reshape(ROPE_HALF, 1))


def _proj_kernel(x_ref, cos_ref, sin_ref, g_attn_ref, w_rows_ref, w_cols_ref, bff_ref, gq_ref,
                 gkv_ref, wuq_ref, wuk_ref, wuvt_ref, tri_ref,
                 fqt_ref, fk_ref, fvt_ref, faux_ref, mqnt_ref, mqrt_ref, mkn_ref, mkr_ref, mvt_ref,
                 h_sc, carry_sc):
    rows = x_ref.shape[1]

    @pl.when(pl.program_id(1) == 0)
    def _():
        carry_sc[...] = jnp.zeros_like(carry_sc)

    h_sc[...] = _rms(x_ref[0], g_attn_ref[...]).astype(BF16)
    h = h_sc[...]

    groups = LANES // ROPE_HALF
    row = lax.broadcasted_iota(jnp.int32, (LANES, rows), 0)
    cos_t = jnp.tile(cos_ref[0], (groups, 1))
    sin_t = jnp.tile(sin_ref[0], (groups, 1))
    sin_s = jnp.where(row < LANES // 2, -sin_t, sin_t)

    def rope_t(xt):
        return xt * cos_t + pltpu.roll(xt, LANES // 2, axis=0) * sin_s


    tk = jnp.dot(h, w_cols_ref[...], preferred_element_type=F32)
    o_cq, o_ckv, o_ff = GROUP_WIDTH, GROUP_WIDTH + Q_RANK, GROUP_WIDTH + Q_RANK + KV_RANK
    fk_ref[0] = tk[:, 0:o_cq].astype(BF16)

    ft = lax.dot_general(w_rows_ref[...], h, NT_DIMS, preferred_element_type=F32)
    fqt_ref[0] = (ft[0:GROUP_WIDTH] * FOX_QSCALE).astype(BF16)
    fvt_ref[0] = ft[GROUP_WIDTH:2 * GROUP_WIDTH].astype(BF16)
    mkr_ref[0] = rope_t(ft[2 * GROUP_WIDTH:2 * GROUP_WIDTH + LANES]).T.astype(BF16)

    lane = lax.broadcasted_iota(jnp.int32, (1, LANES), 1)
    head_lane = lane < N_HEADS
    z = tk[:, o_ff:o_ff + LANES] + bff_ref[...]
    log_f = jnp.where(head_lane, jax.nn.log_sigmoid(z), 0.0)
    tri = tri_ref[...]
    carry = carry_sc[0:1, :]
    for c in range(rows // SCAN_CHUNK):
        part = jnp.dot(tri, _pack3(log_f[c * SCAN_CHUNK:(c + 1) * SCAN_CHUNK, :]),
                       preferred_element_type=F32)
        part = (part + pltpu.roll(part, LANES - N_HEADS, axis=1)
                + pltpu.roll(part, LANES - 2 * N_HEADS, axis=1))
        cum = jnp.where(head_lane, part, 0.0) + carry
        carry = cum[SCAN_CHUNK - 1:SCAN_CHUNK, :]
        faux_ref[0, c * SCAN_CHUNK:(c + 1) * SCAN_CHUNK, :] = _pack3(cum * (-LOG2E))
    carry_sc[...] = jnp.broadcast_to(carry, carry_sc.shape)

    cq = _rms(tk[:, o_cq:o_ckv], gq_ref[...]).astype(BF16)
    qt = lax.dot_general(wuq_ref[...], cq, NT_DIMS, preferred_element_type=F32)
    mqnt_ref[0] = (qt[0:GROUP_WIDTH] * MLA_QSCALE).astype(BF16)
    for g in range(2 * N_HEADS * ROPE_HALF // LANES):
        lo = GROUP_WIDTH + g * LANES
        mqrt_ref[0, g * LANES:(g + 1) * LANES, :] = (
            rope_t(qt[lo:lo + LANES]) * MLA_QSCALE).astype(BF16)

    ckv = _rms(tk[:, o_ckv:o_ff], gkv_ref[...]).astype(BF16)
    mkn_ref[0] = jnp.dot(ckv, wuk_ref[...], preferred_element_type=F32).astype(BF16)
    mvt_ref[0] = lax.dot_general(wuvt_ref[...], ckv, NT_DIMS,
                                 preferred_element_type=F32).astype(BF16)


def _const_spec(shape):
    return pl.BlockSpec(shape, lambda *_: (0,) * len(shape))


def _projection(x, cos_t, sin_t, w):
    b, s, _ = x.shape
    tm = PROJ_ROWS
    row_spec = lambda width: pl.BlockSpec((1, tm, width), lambda i, j: (i, j, 0))
    col_spec = lambda height: pl.BlockSpec((1, height, tm), lambda i, j: (i, 0, j))
    consts = [w["g_attn"], w["w_rows"], w["w_cols"], w["bff"], w["gq"], w["gkv"], w["wuq"],
              w["wuk"], w["wuvt"], w["tri"]]
    row_out = lambda width: jax.ShapeDtypeStruct((b, s, width), BF16)
    col_out = lambda height: jax.ShapeDtypeStruct((b, height, s), BF16)
    return pl.pallas_call(
        _proj_kernel,
        grid=(b, s // tm),
        in_specs=[row_spec(D_MODEL), col_spec(ROPE_HALF), col_spec(ROPE_HALF)]
                 + [_const_spec(c.shape) for c in consts],
        out_specs=[col_spec(GROUP_WIDTH), row_spec(GROUP_WIDTH), col_spec(GROUP_WIDTH),
                   row_spec(LANES), col_spec(GROUP_WIDTH), col_spec(2 * LANES),
                   row_spec(GROUP_WIDTH), row_spec(LANES), col_spec(GROUP_WIDTH)],
        out_shape=[col_out(GROUP_WIDTH), row_out(GROUP_WIDTH), col_out(GROUP_WIDTH),
                   row_out(LANES), col_out(GROUP_WIDTH), col_out(2 * LANES),
                   row_out(GROUP_WIDTH), row_out(LANES), col_out(GROUP_WIDTH)],
        scratch_shapes=[pltpu.VMEM((tm, D_MODEL), BF16), pltpu.VMEM((8, LANES), F32)],
        compiler_params=pltpu.CompilerParams(
            dimension_semantics=("arbitrary", "arbitrary"),
            vmem_limit_bytes=VMEM_LIMIT_BYTES),
        name="input_projection",
    )(x, cos_t, sin_t, *consts)


def _attn_kernel(tasks_ref, qm_ref, qa_ref, km_ref, ka_ref, vt_ref, mask_ref, o_ref,
                 bias_sc, qh_sc, s0_sc, s1_sc, acc_sc, m_sc):
    s_len = km_ref.shape[1]
    tq, tk = ATTN_TQ, ATTN_TK

    @pl.when((pl.program_id(0) == 0) & (pl.program_id(1) == 0))
    def _():
        key_idx = lax.broadcasted_iota(jnp.int32, (tk, tq), 0)
        qry_idx = lax.broadcasted_iota(jnp.int32, (tk, tq), 1)
        bias_sc[...] = jnp.where(key_idx <= qry_idx, 0.0, -jnp.inf)

    def kv_tile(k0):
        return jnp.concatenate([km_ref[0, pl.ds(k0, tk), :], ka_ref[0, pl.ds(k0, tk), :]], axis=-1)

    ones_rows = jnp.ones((ONES_ROWS, tk), BF16)

    qcat = jnp.concatenate([qm_ref[0], qa_ref[0]], axis=0)
    for a in range(2):
        qh_sc[a] = qcat * jnp.tile(mask_ref[0, a], (1, s_len // LANES))
    last_tile = s_len // tq - 1
    acc_sc[last_tile] = jnp.zeros(acc_sc.shape[1:], F32)
    m_sc[last_tile] = jnp.full(m_sc.shape[1:], -jnp.inf, F32)

    def produce(dst, t, diagonal):
        q0 = pl.multiple_of(tasks_ref[0, t] * tq, tq)
        k0 = pl.multiple_of(tasks_ref[1, t] * tk, tk)
        q_pair = jnp.concatenate([qh_sc[a, :, pl.ds(q0, tq)] for a in range(2)], axis=1)
        sc = jnp.dot(kv_tile(k0), q_pair, preferred_element_type=F32)
        maxima = []
        for a in range(2):
            sc_a = sc[:, a * tq:(a + 1) * tq]
            if diagonal:
                sc_a = sc_a + bias_sc[...]
            dst[a] = sc_a
            maxima.append(jnp.max(sc_a, axis=0, keepdims=True))
        return tuple(maxima)

    def consume(src, t, maxima):
        i = tasks_ref[0, t]
        k0 = pl.multiple_of(tasks_ref[1, t] * tk, tk)
        alphas, ps = [], []
        for a in range(2):
            m_run = m_sc[i, a]
            m_new = jnp.maximum(m_run, maxima[a])
            alphas.append(jnp.exp2(m_run - m_new))
            ps.append(jnp.exp2(src[a] - m_new).astype(BF16))
            m_sc[i, a] = m_new
        vt = jnp.concatenate([vt_ref[0, :, pl.ds(k0, tk)], ones_rows], axis=0)
        pv = jnp.dot(vt, jnp.concatenate(ps, axis=1), preferred_element_type=F32)
        for a in range(2):
            pv_a = jnp.concatenate([pv[a * HEAD_DIM:(a + 1) * HEAD_DIM, a * tq:(a + 1) * tq],
                                    pv[2 * HEAD_DIM:, a * tq:(a + 1) * tq]], axis=0)
            acc_sc[i, a] = alphas[a] * acc_sc[i, a] + pv_a

    half = tq // 2

    def produce_diag(dst, i):
        q0 = i * tq
        kcat = kv_tile(q0)
        q_all = jnp.concatenate([qh_sc[a, :, q0:q0 + tq] for a in range(2)], axis=1)
        q_late = jnp.concatenate([qh_sc[a, :, q0 + half:q0 + tq] for a in range(2)], axis=1)
        top = jnp.dot(kcat[0:half], q_all, preferred_element_type=F32)
        bot = jnp.dot(kcat[half:tk], q_late, preferred_element_type=F32)
        maxima = []
        for a in range(2):
            top_a = top[:, a * tq:(a + 1) * tq] + bias_sc[0:half, :]
            bot_a = bot[:, a * half:(a + 1) * half] + bias_sc[half:tk, half:tq]
            dst[a, 0:half, :] = top_a
            dst[a, half:tk, half:tq] = bot_a
            mx_top = jnp.max(top_a, axis=0, keepdims=True)
            mx_bot = jnp.max(bot_a, axis=0, keepdims=True)
            maxima.append(jnp.concatenate(
                [mx_top[:, 0:half], jnp.maximum(mx_top[:, half:tq], mx_bot)], axis=1))
        return tuple(maxima)

    def consume_diag(src, i, maxima):
        q0 = i * tq
        early, late = [], []
        for a in range(2):
            p_top = jnp.exp2(src[a, 0:half, :] - maxima[a]).astype(BF16)
            p_bot = jnp.exp2(src[a, half:tk, half:tq] - maxima[a][:, half:tq]).astype(BF16)
            early.append(p_top[:, 0:half])
            late.append(jnp.concatenate([p_top[:, half:tq], p_bot], axis=0))
        vt = jnp.concatenate([vt_ref[0, :, q0:q0 + tk], ones_rows], axis=0)
        pv_early = jnp.dot(vt[:, 0:half], jnp.concatenate(early, axis=1),
                           preferred_element_type=F32)
        pv_late = jnp.dot(vt, jnp.concatenate(late, axis=1), preferred_element_type=F32)
        for a in range(2):
            pv_a = jnp.concatenate([pv_early[:, a * half:(a + 1) * half],
                                    pv_late[:, a * half:(a + 1) * half]], axis=1)
            acc_sc[i, a] = jnp.concatenate(
                [pv_a[a * HEAD_DIM:(a + 1) * HEAD_DIM], pv_a[2 * HEAD_DIM:]], axis=0)
            m_sc[i, a] = maxima[a]

    bufs = (s0_sc, s1_sc)

    def step(t, parity, diagonal, mx):
        mx_next = produce(bufs[1 - parity], t + 1, diagonal)
        consume(bufs[parity], t, mx)
        return mx_next

    def steps(first, count, diagonal):
        def body(it, mx):
            t = first + it * count
            for u in range(count):
                mx = step(t + u, (first + u) % 2, diagonal, mx)
            return mx
        return body

    n_diag = s_len // tq
    n_tasks = n_diag * (n_diag + 1) // 2
    n_off = n_tasks - n_diag
    assert n_diag % 2 == 0 and n_off % UNROLL_STEPS == 0 and UNROLL_STEPS % 2 == 0
    assert tq == tk
    mx = produce_diag(s0_sc, 0)
    for t in range(n_diag - 1):
        if t + 1 < n_diag - 1:
            mx_next = produce_diag(bufs[(t + 1) % 2], t + 1)
        else:
            mx_next = produce(bufs[(t + 1) % 2], t + 1, True)
        consume_diag(bufs[t % 2], t, mx)
        mx = mx_next
    mx = lax.fori_loop(0, n_off // UNROLL_STEPS, steps(n_diag - 1, UNROLL_STEPS, False), mx)
    consume(bufs[(n_tasks - 1) % 2], n_tasks - 1, mx)

    def finalize(it, _):
        for u in range(FINALIZE_UNROLL):
            i = it * FINALIZE_UNROLL + u
            q0 = pl.multiple_of(i * tq, tq)
            out_t = jnp.concatenate(
                [acc_sc[i, a, 0:HEAD_DIM, :] / acc_sc[i, a, HEAD_DIM:HEAD_DIM + 1, :]
                 for a in range(2)], axis=0)
            o_ref[0, :, pl.ds(q0, tq)] = out_t
        return 0

    assert n_diag % FINALIZE_UNROLL == 0
    lax.fori_loop(0, n_diag // FINALIZE_UNROLL, finalize, 0)


def _attention(qmt, qat, km, ka, vt, mask, qa_batched, qa_per_pairs):
    b, s, _ = km.shape
    pairs = N_HEADS // 2
    tok = lambda f: pl.BlockSpec((1, s, LANES), f)
    feat = lambda f: pl.BlockSpec((1, LANES, s), f)
    qa_map = (lambda i, p: (i, p // qa_per_pairs, 0)) if qa_batched else (lambda i, p: (0, 0, 0))
    n_q = s // ATTN_TQ
    task_list = [(i, i) for i in range(n_q)] + [(i, j) for i in range(n_q) for j in range(i)]
    tasks = jnp.asarray(np.array(task_list, np.int32).T)
    return pl.pallas_call(
        _attn_kernel,
        grid=(b, pairs),
        in_specs=[pl.BlockSpec(memory_space=pltpu.SMEM),
                  feat(lambda i, p: (i, p, 0)), feat(qa_map), tok(lambda i, p: (i, 0, p)),
                  tok(lambda i, p: (i, 0, 0)), feat(lambda i, p: (i, p, 0)),
                  pl.BlockSpec((1, 2, 2 * LANES, LANES), lambda i, p: (p, 0, 0, 0))],
        out_specs=feat(lambda i, p: (i, p, 0)),
        out_shape=jax.ShapeDtypeStruct((b, GROUP_WIDTH, s), F32),
        scratch_shapes=[pltpu.VMEM((ATTN_TK, ATTN_TQ), F32),
                        pltpu.VMEM((2, 2 * LANES, s), BF16),
                        pltpu.VMEM((2, ATTN_TK, ATTN_TQ), F32),
                        pltpu.VMEM((2, ATTN_TK, ATTN_TQ), F32),
                        pltpu.VMEM((n_q, 2, HEAD_DIM + ONES_ROWS, ATTN_TQ), F32),
                        pltpu.VMEM((n_q, 2, 1, ATTN_TQ), F32)],
        compiler_params=pltpu.CompilerParams(
            dimension_semantics=("arbitrary", "arbitrary"),
            vmem_limit_bytes=VMEM_LIMIT_BYTES),
        name="causal_attention",
    )(tasks, qmt, qat, km, ka, vt, mask)


def _mlp_kernel(x_ref, fox_ref, mla_ref, gfox_ref, gmla_ref, wo_ref, gmlp_ref, wup_ref,
                wdown_ref, gfin_ref, o_ref, h_sc, y_sc, *, final):
    k = pl.program_id(1)

    @pl.when(k == 0)
    def _():
        def rms_features(v_t, g_col):
            ms = jnp.mean(v_t * v_t, axis=0, keepdims=True)
            return v_t * lax.rsqrt(ms + EPS) * g_col

        mixed_t = jnp.concatenate([rms_features(fox_ref[0], gfox_ref[...]).astype(BF16),
                                   rms_features(mla_ref[0], gmla_ref[...]).astype(BF16)], axis=0)
        x1 = x_ref[...] + lax.dot_general(mixed_t, wo_ref[...], (((0,), (0,)), ((), ())),
                                          preferred_element_type=F32)
        h_sc[...] = _rms(x1, gmlp_ref[...]).astype(BF16)
        y_sc[...] = x1

    u = jnp.dot(h_sc[...], wup_ref[...], preferred_element_type=F32)
    act = jnp.square(jnp.maximum(u, 0.0)).astype(BF16)
    y_sc[...] += jnp.dot(act, wdown_ref[...], preferred_element_type=F32)

    @pl.when(k == pl.num_programs(1) - 1)
    def _():
        y = y_sc[...]
        o_ref[...] = _rms(y, gfin_ref[...]) if final else y


def _out_mlp(x2d, fox_t, mla_t, w, final):
    n = x2d.shape[0]
    tm = MLP_ROWS
    tiles_per_seq = fox_t.shape[2] // tm
    row = lambda width: pl.BlockSpec((tm, width), lambda i, k: (i, 0))
    feat = pl.BlockSpec((1, GROUP_WIDTH, tm),
                        lambda i, k: (i // tiles_per_seq, 0, i % tiles_per_seq))
    resident = lambda a: pl.BlockSpec(a.shape, lambda i, k: (0,) * a.ndim,
                                      pipeline_mode=pl.Buffered(1))
    wup_spec = pl.BlockSpec((D_MODEL, FF_CHUNK), lambda i, k: (0, k))
    wdown_spec = pl.BlockSpec((FF_CHUNK, D_MODEL), lambda i, k: (k, 0))
    return pl.pallas_call(
        functools.partial(_mlp_kernel, final=final),
        grid=(n // tm, D_FF // FF_CHUNK),
        in_specs=[row(D_MODEL), feat, feat, resident(w["gfox"]), resident(w["gmla"]),
                  resident(w["wo"]), resident(w["gmlp"]), wup_spec, wdown_spec,
                  resident(w["gfin"])],
        out_specs=row(D_MODEL),
        out_shape=jax.ShapeDtypeStruct((n, D_MODEL), F32),
        scratch_shapes=[pltpu.VMEM((tm, D_MODEL), BF16), pltpu.VMEM((tm, D_MODEL), F32)],
        compiler_params=pltpu.CompilerParams(
            dimension_semantics=("arbitrary", "arbitrary"),
            vmem_limit_bytes=VMEM_LIMIT_BYTES),
        name="out_proj_mlp",
    )(x2d, fox_t, mla_t, w["gfox"], w["gmla"], w["wo"], w["gmlp"], w["wup"], w["wdown"],
      w["gfin"])


def _rope_split_cols(wr):
    k = wr.shape[0]
    heads = wr.shape[1]
    per_group = LANES // ROPE_DIM
    first = wr[:, :, :ROPE_HALF].reshape(k, heads // per_group, per_group * ROPE_HALF)
    second = wr[:, :, ROPE_HALF:].reshape(k, heads // per_group, per_group * ROPE_HALF)
    return jnp.concatenate([first, second], axis=-1).reshape(k, heads * ROPE_DIM)


def _prep_weights(attn_norm_g, w_in, b_forget, q_norm_g, w_uq, kv_norm_g, w_ukv, fox_out_g,
                  mla_out_g, w_o, mlp_norm_g, w_up, w_down, final_norm_g):
    row = lambda v: v.reshape(1, -1).astype(F32)
    pad_lanes = lambda a: jnp.pad(a, ((0, 0), (0, LANES - a.shape[1])))
    wkr = w_in[:, OFF_KR:IN_COLS].reshape(D_MODEL, 1, ROPE_DIM)
    wkr = _rope_split_cols(jnp.broadcast_to(wkr, (D_MODEL, LANES // ROPE_DIM, ROPE_DIM)))
    uq = w_uq.reshape(Q_RANK, N_HEADS, MLA_QK_DIM)
    ukv = w_ukv.reshape(KV_RANK, N_HEADS, 2 * HEAD_DIM)
    tri = np.tril(np.ones((SCAN_CHUNK, SCAN_CHUNK), np.float32))
    w_rows = jnp.concatenate([w_in[:, OFF_FQ:OFF_FK], w_in[:, OFF_FV:OFF_FF], wkr], axis=1)
    w_cols = jnp.concatenate([w_in[:, OFF_FK:OFF_FV], w_in[:, OFF_CQ:OFF_CKV],
                              w_in[:, OFF_CKV:OFF_KR], pad_lanes(w_in[:, OFF_FF:OFF_CQ])], axis=1)
    wuq = jnp.concatenate([uq[:, :, :HEAD_DIM].reshape(Q_RANK, GROUP_WIDTH),
                           _rope_split_cols(uq[:, :, HEAD_DIM:])], axis=1)
    return {
        "g_attn": row(attn_norm_g),
        "w_rows": w_rows.T.astype(BF16),
        "w_cols": w_cols.astype(BF16),
        "bff": pad_lanes(row(b_forget)),
        "gq": row(q_norm_g),
        "gkv": row(kv_norm_g),
        "wuq": wuq.T.astype(BF16),
        "wuk": ukv[:, :, :HEAD_DIM].reshape(KV_RANK, GROUP_WIDTH).astype(BF16),
        "wuvt": ukv[:, :, HEAD_DIM:].reshape(KV_RANK, GROUP_WIDTH).T.astype(BF16),
        "tri": jnp.asarray(tri, BF16),
        "gfox": fox_out_g.reshape(-1, 1).astype(F32),
        "gmla": mla_out_g.reshape(-1, 1).astype(F32),
        "wo": w_o.astype(BF16),
        "gmlp": row(mlp_norm_g),
        "wup": w_up.astype(BF16),
        "wdown": w_down.astype(BF16),
        "gfin": row(final_norm_g),
    }


def _head_masks():
    pairs = N_HEADS // 2
    fox = np.zeros((pairs, 2, 2 * LANES), np.float32)
    mla = np.zeros((pairs, 2, 2 * LANES), np.float32)
    per_group = LANES // ROPE_DIM
    for p in range(pairs):
        for a in range(2):
            h = 2 * p + a
            fox[p, a, a * HEAD_DIM:(a + 1) * HEAD_DIM] = 1.0
            mla[p, a, a * HEAD_DIM:(a + 1) * HEAD_DIM] = 1.0
            for piece in range(3):
                fox[p, a, LANES + piece * N_HEADS + h] = 1.0
            hh = h % per_group
            for half in range(2):
                lo = LANES + half * (LANES // 2) + hh * ROPE_HALF
                mla[p, a, lo:lo + ROPE_HALF] = 1.0
    rep = lambda m: jnp.asarray(np.repeat(m[..., None], LANES, axis=-1), BF16)
    return rep(fox), rep(mla)


def kernel(x, positions, attn_norm_g, w_in, b_forget, q_norm_g, w_uq, kv_norm_g, w_ukv,
           fox_out_g, mla_out_g, w_o, mlp_norm_g, w_up, w_down, final_norm_g):
    b, s, d = x.shape
    depth = w_in.shape[0]
    inv_freq = ROPE_THETA ** (-jnp.arange(0, ROPE_DIM, 2, dtype=F32) / ROPE_DIM)
    cos_t, sin_t = _rope_tables(positions, inv_freq)
    fox_mask, mla_mask = _head_masks()
    ones_aux = jnp.ones((1, LANES, s), BF16)

    for l in range(depth):
        w = _prep_weights(attn_norm_g[l], w_in[l], b_forget[l], q_norm_g[l], w_uq[l],
                          kv_norm_g[l], w_ukv[l], fox_out_g[l], mla_out_g[l], w_o[l],
                          mlp_norm_g[l], w_up[l], w_down[l], final_norm_g)
        fqt, fk, fvt, faux, mqnt, mqrt, mkn, mkr, mvt = _projection(x, cos_t, sin_t, w)
        fox = _attention(fqt, ones_aux, fk, faux, fvt, fox_mask, False, 1)
        mla = _attention(mqnt, mqrt, mkn, mkr, mvt, mla_mask, True, 2)
        y = _out_mlp(x.reshape(b * s, d), fox, mla, w, l == depth - 1)
        x = y.reshape(b, s, d)
    return x
```
